```python
import math
import jax, jax.numpy as jnp
from jax import lax
import numpy as np

D_MODEL = 2048
BATCH = 4
SEQ = 4096
DEPTH = 4

D_FF = 5632
NORM_EPS = 1e-5
NEG_INF = -1e30
A_HEADS = 16
A_KV_HEADS = 4
HEAD_DIM = 64
WINDOW = 128
ATTN_BLOCK = 128
ROPE_THETA = 500000.0
ROPE_DIM = HEAD_DIM // 4
A_WIDTH = A_HEADS * HEAD_DIM
KV_WIDTH = A_KV_HEADS * HEAD_DIM
S5_WIDTH = D_MODEL // 2
S5_GROUP = 16
S5_GROUPS = S5_WIDTH // S5_GROUP
S5_STATE = 64
EVEN_IN = A_WIDTH + 2 * KV_WIDTH + S5_WIDTH
EVEN_OUT = A_WIDTH + S5_WIDTH
M_INNER = 2 * D_MODEL
M_HEAD_DIM = 64
M_HEADS = M_INNER // M_HEAD_DIM
M_GROUPS = 8
M_STATE = 128
M_CONV = 4
M_CHUNK = 128
M_CONV_DIM = M_INNER + 2 * M_GROUPS * M_STATE
M_IN = M_INNER + M_CONV_DIM + M_HEADS
N_EVEN = (DEPTH + 1) // 2
N_ODD = DEPTH // 2

kernel_name = "hybrid_swa_s5_ssd_macaron"


def rms_norm(x, g):
    xf = x.astype(jnp.float32)
    xf = xf * lax.rsqrt(jnp.mean(xf * xf, axis=-1, keepdims=True) + NORM_EPS)
    return xf.astype(x.dtype) * g


def swiglu_ffn(x, w_gate, w_up, w_down):
    return (jax.nn.silu(x @ w_gate) * (x @ w_up)) @ w_down


def partial_rotary(t, positions):
    half = ROPE_DIM // 2
    inv_freq = jnp.exp(-math.log(ROPE_THETA) * jnp.arange(half, dtype=jnp.float32) * (2.0 / ROPE_DIM))
    ang = positions.astype(jnp.float32)[:, :, None] * inv_freq
    cos = jnp.cos(ang)[:, :, None, :]
    sin = jnp.sin(ang)[:, :, None, :]
    tr = t[..., :ROPE_DIM].astype(jnp.float32)
    t1, t2 = tr[..., :half], tr[..., half:]
    rot = jnp.concatenate([t1 * cos - t2 * sin, t2 * cos + t1 * sin], axis=-1).astype(t.dtype)
    return jnp.concatenate([rot, t[..., ROPE_DIM:]], axis=-1)


def sliding_window_attention(q, k, v, sinks):
    b, s, _, hd = q.shape
    nb = s // ATTN_BLOCK
    grp = A_HEADS // A_KV_HEADS
    qb = q.reshape(b, nb, ATTN_BLOCK, A_KV_HEADS, grp, hd)

    def banded(t):
        tp = jnp.pad(t, ((0, 0), (ATTN_BLOCK, 0), (0, 0), (0, 0)))
        prev = tp[:, :s].reshape(b, nb, ATTN_BLOCK, A_KV_HEADS, hd)
        cur = t.reshape(b, nb, ATTN_BLOCK, A_KV_HEADS, hd)
        return jnp.concatenate([prev, cur], axis=2)

    kb, vb = banded(k), banded(v)
    scores = jnp.einsum('bnqhgd,bnchd->bnhgqc', qb, kb).astype(jnp.float32) * (1.0 / math.sqrt(hd))
    qpos = jnp.arange(ATTN_BLOCK)[:, None] + ATTN_BLOCK
    cpos = jnp.arange(2 * ATTN_BLOCK)[None, :]
    rel = qpos - cpos
    band = (rel >= 0) & (rel < WINDOW)
    blk_start = jnp.arange(nb)[:, None, None] * ATTN_BLOCK
    valid = band[None] & (blk_start + cpos[None] - ATTN_BLOCK >= 0)
    scores = jnp.where(valid[None, :, None, None], scores, NEG_INF)
    sink = sinks.astype(jnp.float32).reshape(A_KV_HEADS, grp)[None, None, :, :, None, None]
    sink = jnp.broadcast_to(sink, scores.shape[:-1] + (1,))
    probs = jax.nn.softmax(jnp.concatenate([scores, sink], axis=-1), axis=-1)[..., :-1]
    out = jnp.einsum('bnhgqc,bnchd->bnqhgd', probs.astype(v.dtype), vb)
    return out.reshape(b, s, A_HEADS * hd)


def s5_mixer(u, a_re, a_im, log_dt, b_re, b_im, c_re, c_im, d_skip, w_glu, b_glu):
    f32 = jnp.float32
    bsz, s, _ = u.shape
    uf = u.astype(f32).reshape(bsz, s, S5_GROUPS, S5_GROUP)
    are, aim = a_re.astype(f32), a_im.astype(f32)
    dt = jnp.exp(log_dt.astype(f32))[:, None]
    mag = jnp.exp(are * dt)
    abar_re, abar_im = mag * jnp.cos(aim * dt), mag * jnp.sin(aim * dt)
    nr, ni = abar_re - 1.0, abar_im
    den = are * are + aim * aim
    coef_re = (nr * are + ni * aim) / den
    coef_im = (ni * are - nr * aim) / den
    bre, bim = b_re.astype(f32), b_im.astype(f32)
    bbar_re = coef_re[..., None] * bre - coef_im[..., None] * bim
    bbar_im = coef_re[..., None] * bim + coef_im[..., None] * bre
    bu_re = jnp.einsum('gpc,bsgc->bsgp', bbar_re, uf)
    bu_im = jnp.einsum('gpc,bsgc->bsgp', bbar_im, uf)
    a_re_t = jnp.broadcast_to(abar_re, (s, S5_GROUPS, S5_STATE))
    a_im_t = jnp.broadcast_to(abar_im, (s, S5_GROUPS, S5_STATE))

    def combine(left, right):
        a1r, a1i, b1r, b1i = left
        a2r, a2i, b2r, b2i = right
        return (a1r * a2r - a1i * a2i, a1r * a2i + a1i * a2r,
                a2r * b1r - a2i * b1i + b2r, a2r * b1i + a2i * b1r + b2i)

    def scan_one(br, bi):
        _, _, hr, hi = lax.associative_scan(combine, (a_re_t, a_im_t, br, bi), axis=0)
        return hr, hi

    h_re, h_im = jax.vmap(scan_one)(bu_re, bu_im)
    y = (jnp.einsum('gcp,bsgp->bsgc', c_re.astype(f32), h_re)
         - jnp.einsum('gcp,bsgp->bsgc', c_im.astype(f32), h_im))
    y = (y + d_skip.astype(f32) * uf).reshape(bsz, s, S5_WIDTH)
    g = jax.nn.gelu(y)
    out = g * jax.nn.sigmoid(g @ w_glu.astype(f32) + b_glu.astype(f32))
    return out.astype(u.dtype)


def attn_s5_mixer(h, positions, w_in, sinks, a_re, a_im, log_dt, b_re, b_im, c_re, c_im,
                  d_skip, w_glu, b_glu, w_out):
    b, s, _ = h.shape
    proj = h @ w_in
    q = proj[..., :A_WIDTH].reshape(b, s, A_HEADS, HEAD_DIM)
    k = proj[..., A_WIDTH:A_WIDTH + KV_WIDTH].reshape(b, s, A_KV_HEADS, HEAD_DIM)
    v = proj[..., A_WIDTH + KV_WIDTH:A_WIDTH + 2 * KV_WIDTH].reshape(b, s, A_KV_HEADS, HEAD_DIM)
    u = proj[..., A_WIDTH + 2 * KV_WIDTH:]
    q = partial_rotary(q, positions)
    k = partial_rotary(k, positions)
    attn = sliding_window_attention(q, k, v, sinks)
    ssm = s5_mixer(u, a_re, a_im, log_dt, b_re, b_im, c_re, c_im, d_skip, w_glu, b_glu)
    return jnp.concatenate([attn, ssm], axis=-1) @ w_out


def ssd_chunked(x, dt, a, bm, cm):
    bsz, s, _, p = x.shape
    nc = s // M_CHUNK
    hpg = M_HEADS // M_GROUPS
    xc = (x * dt[..., None]).reshape(bsz, nc, M_CHUNK, M_GROUPS, hpg, p)
    bc = bm.reshape(bsz, nc, M_CHUNK, M_GROUPS, M_STATE)
    cc = cm.reshape(bsz, nc, M_CHUNK, M_GROUPS, M_STATE)
    adt = (a * dt).reshape(bsz, nc, M_CHUNK, M_GROUPS, hpg).transpose(0, 1, 3, 4, 2)
    a_cum = jnp.cumsum(adt, axis=-1)
    seg = a_cum[..., :, None] - a_cum[..., None, :]
    causal = jnp.tril(jnp.ones((M_CHUNK, M_CHUNK), dtype=bool))
    decay = jnp.exp(jnp.where(causal, seg, -jnp.inf))
    cb = jnp.einsum('bclgn,bcsgn->bcgls', cc, bc)
    y_diag = jnp.einsum('bcgls,bcgjls,bcsgjp->bclgjp', cb, decay, xc)
    decay_states = jnp.exp(a_cum[..., -1:] - a_cum)
    states = jnp.einsum('bclgn,bcgjl,bclgjp->bcgjpn', bc, decay_states, xc)
    chunk_decay = jnp.exp(a_cum[..., -1])

    def step(carry, inp):
        st, dec = inp
        return carry * dec[..., None, None] + st, carry

    init = jnp.zeros((bsz, M_GROUPS, hpg, p, M_STATE), jnp.float32)
    _, prev = lax.scan(step, init, (jnp.moveaxis(states, 1, 0), jnp.moveaxis(chunk_decay, 1, 0)))
    prev = jnp.moveaxis(prev, 0, 1)
    y_off = jnp.einsum('bclgn,bcgjpn,bcgjl->bclgjp', cc, prev, jnp.exp(a_cum))
    return (y_diag + y_off).reshape(bsz, s, M_HEADS, p)


def mamba2_mixer(h, w_in, conv_w, conv_b, dt_bias, a_log, d_skip, norm_g, w_out):
    f32 = jnp.float32
    bsz, s, _ = h.shape
    zxbcdt = h @ w_in
    z = zxbcdt[..., :M_INNER]
    xbc = zxbcdt[..., M_INNER:M_INNER + M_CONV_DIM]
    dt_raw = zxbcdt[..., M_INNER + M_CONV_DIM:]
    xpad = jnp.pad(xbc, ((0, 0), (M_CONV - 1, 0), (0, 0)))
    conv = conv_b
    for tap in range(M_CONV):
        conv = conv + conv_w[tap] * xpad[:, tap:tap + s]
    xbc = jax.nn.silu(conv)
    xs = xbc[..., :M_INNER].reshape(bsz, s, M_HEADS, M_HEAD_DIM).astype(f32)
    bm = xbc[..., M_INNER:M_INNER + M_GROUPS * M_STATE].reshape(bsz, s, M_GROUPS, M_STATE).astype(f32)
    cm = xbc[..., M_INNER + M_GROUPS * M_STATE:].reshape(bsz, s, M_GROUPS, M_STATE).astype(f32)
    dt = jax.nn.softplus(dt_raw.astype(f32) + dt_bias.astype(f32))
    a = -jnp.exp(a_log.astype(f32))
    y = ssd_chunked(xs, dt, a, bm, cm) + d_skip.astype(f32)[:, None] * xs
    y = y.reshape(bsz, s, M_INNER) * jax.nn.silu(z.astype(f32))
    y = y.reshape(bsz, s, M_GROUPS, M_INNER // M_GROUPS)
    y = y * lax.rsqrt(jnp.mean(y * y, axis=-1, keepdims=True) + NORM_EPS)
    y = y.reshape(bsz, s, M_INNER).astype(h.dtype) * norm_g
    return y @ w_out


def setup_inputs(seed: int = 0) -> dict:
    key = jax.random.key(seed)
    ks = iter(jax.random.split(key, 48))
    f32 = jnp.float32

    def nrm(shape, scale):
        return jax.random.normal(next(ks), shape, f32) * scale

    def gain(shape):
        return 1.0 + nrm(shape, 0.02)

    x = jax.random.normal(next(ks), (BATCH, SEQ, D_MODEL), f32)
    positions = jnp.broadcast_to(jnp.arange(SEQ, dtype=jnp.int32)[None, :], (BATCH, SEQ))
    a_im0 = jnp.pi * jnp.arange(S5_STATE, dtype=f32)
    dt0 = jnp.exp(jax.random.uniform(next(ks), (N_ODD, M_HEADS), f32, math.log(1e-3), math.log(1e-1)))
    return {
        "x": x,
        "positions": positions,
        "norm_ffn1": gain((DEPTH, D_MODEL)),
        "ffn1_gate": nrm((DEPTH, D_MODEL, D_FF), D_MODEL ** -0.5),
        "ffn1_up": nrm((DEPTH, D_MODEL, D_FF), D_MODEL ** -0.5),
        "ffn1_down": nrm((DEPTH, D_FF, D_MODEL), D_FF ** -0.5),
        "norm_mix": gain((DEPTH, D_MODEL)),
        "norm_ffn2": gain((DEPTH, D_MODEL)),
        "ffn2_gate": nrm((DEPTH, D_MODEL, D_FF), D_MODEL ** -0.5),
        "ffn2_up": nrm((DEPTH, D_MODEL, D_FF), D_MODEL ** -0.5),
        "ffn2_down": nrm((DEPTH, D_FF, D_MODEL), D_FF ** -0.5),
        "ev_w_in": nrm((N_EVEN, D_MODEL, EVEN_IN), D_MODEL ** -0.5),
        "ev_sinks": nrm((N_EVEN, A_HEADS), 0.5),
        "s5_a_re": -0.5 + nrm((N_EVEN, S5_GROUPS, S5_STATE), 0.01),
        "s5_a_im": a_im0 + nrm((N_EVEN, S5_GROUPS, S5_STATE), 0.01),
        "s5_log_dt": jax.random.uniform(next(ks), (N_EVEN, S5_GROUPS), f32, math.log(1e-3), math.log(1e-1)),
        "s5_b_re": nrm((N_EVEN, S5_GROUPS, S5_STATE, S5_GROUP), (2 * S5_GROUP) ** -0.5),
        "s5_b_im": nrm((N_EVEN, S5_GROUPS, S5_STATE, S5_GROUP), (2 * S5_GROUP) ** -0.5),
        "s5_c_re": nrm((N_EVEN, S5_GROUPS, S5_GROUP, S5_STATE), S5_STATE ** -0.5),
        "s5_c_im": nrm((N_EVEN, S5_GROUPS, S5_GROUP, S5_STATE), S5_STATE ** -0.5),
        "s5_d": nrm((N_EVEN, S5_GROUPS, S5_GROUP), 1.0),
        "s5_w_glu": nrm((N_EVEN, S5_WIDTH, S5_WIDTH), S5_WIDTH ** -0.5),
        "s5_b_glu": nrm((N_EVEN, S5_WIDTH), 0.01),
        "ev_w_out": nrm((N_EVEN, EVEN_OUT, D_MODEL), EVEN_OUT ** -0.5),
        "m_w_in": nrm((N_ODD, D_MODEL, M_IN), D_MODEL ** -0.5),
        "m_conv_w": nrm((N_ODD, M_CONV, M_CONV_DIM), M_CONV ** -0.5),
        "m_conv_b": nrm((N_ODD, M_CONV_DIM), 0.01),
        "m_dt_bias": dt0 + jnp.log(-jnp.expm1(-dt0)),
        "m_a_log": jnp.log(jax.random.uniform(next(ks), (N_ODD, M_HEADS), f32, 1.0, 16.0)),
        "m_d": gain((N_ODD, M_HEADS)),
        "m_norm": gain((N_ODD, M_INNER)),
        "m_w_out": nrm((N_ODD, M_INNER, D_MODEL), M_INNER ** -0.5),
        "final_norm": gain((D_MODEL,)),
    }


def reference(x, positions, norm_ffn1, ffn1_gate, ffn1_up, ffn1_down, norm_mix,
              norm_ffn2, ffn2_gate, ffn2_up, ffn2_down,
              ev_w_in, ev_sinks, s5_a_re, s5_a_im, s5_log_dt, s5_b_re, s5_b_im,
              s5_c_re, s5_c_im, s5_d, s5_w_glu, s5_b_glu, ev_w_out,
              m_w_in, m_conv_w, m_conv_b, m_dt_bias, m_a_log, m_d, m_norm, m_w_out,
              final_norm):
    for layer in range(DEPTH):
        x = x + 0.5 * swiglu_ffn(rms_norm(x, norm_ffn1[layer]), ffn1_gate[layer], ffn1_up[layer], ffn1_down[layer])
        hn = rms_norm(x, norm_mix[layer])
        if layer % 2 == 0:
            e = layer // 2
            mix = attn_s5_mixer(hn, positions, ev_w_in[e], ev_sinks[e], s5_a_re[e], s5_a_im[e],
                                s5_log_dt[e], s5_b_re[e], s5_b_im[e], s5_c_re[e], s5_c_im[e],
                                s5_d[e], s5_w_glu[e], s5_b_glu[e], ev_w_out[e])
        else:
            o = layer // 2
            mix = mamba2_mixer(hn, m_w_in[o], m_conv_w[o], m_conv_b[o], m_dt_bias[o], m_a_log[o],
                               m_d[o], m_norm[o], m_w_out[o])
        x = x + mix
        x = x + 0.5 * swiglu_ffn(rms_norm(x, norm_ffn2[layer]), ffn2_gate[layer], ffn2_up[layer], ffn2_down[layer])
    return rms_norm(x, final_norm)
```

```python
import functools
import math

import jax
import jax.numpy as jnp
from jax import lax
from jax.experimental import pallas as pl
from jax.experimental.pallas import tpu as pltpu

F32 = jnp.float32
BF16 = jnp.bfloat16

NORM_EPS = 1e-5
NEG_INF = -1e30
LANES = 128
SUBLANES = 8
VMEM_LIMIT_BYTES = 56 * 1024 * 1024

HEAD_DIM = 64
ATTN_BLOCK = 128
ROPE_THETA = 500000.0
ROPE_DIM = HEAD_DIM // 4
ATTN_BLOCKS_PER_STEP = 4
S5_GROUP = 16
S5_STATE = 64
S5_SEGMENTS = SUBLANES
S5_GROUPS_PER_TILE = 16
M_HEAD_DIM = 64
M_STATE = 128
M_CHUNK = 128
M_CONV = 4


def _params(*sem):
    return pltpu.CompilerParams(dimension_semantics=sem, vmem_limit_bytes=VMEM_LIMIT_BYTES)


def _rms(x, g):
    return x * lax.rsqrt(jnp.mean(x * x, axis=-1, keepdims=True) + NORM_EPS) * g


def _ffn_body(x_ref, g_ref, wg_ref, wu_ref, wd_ref, o_ref, xn_ref):
    j = pl.program_id(1)

    @pl.when(j == 0)
    def _():
        xn_ref[...] = _rms(x_ref[...], g_ref[...]).astype(BF16)
        o_ref[...] = jnp.zeros_like(o_ref)

    xn = xn_ref[...]
    gate = jnp.dot(xn, wg_ref[...], preferred_element_type=F32)
    up = jnp.dot(xn, wu_ref[...], preferred_element_type=F32)
    h = (jax.nn.silu(gate) * up).astype(BF16)
    o_ref[...] += jnp.dot(h, wd_ref[...], preferred_element_type=F32)

    @pl.when(j == pl.num_programs(1) - 1)
    def _():
        o_ref[...] = x_ref[...] + 0.5 * o_ref[...]


def _ffn(x, g, wg, wu, wd, layer, *, tm=512, tf=512):
    t, d = x.shape
    dff = wg.shape[-1]
    return pl.pallas_call(
        _ffn_body,
        grid=(t // tm, dff // tf),
        in_specs=[
            pl.BlockSpec((tm, d), lambda i, j: (i, 0)),
            pl.BlockSpec((None, 1, d), lambda i, j: (layer, 0, 0)),
            pl.BlockSpec((None, d, tf), lambda i, j: (layer, 0, j)),
            pl.BlockSpec((None, d, tf), lambda i, j: (layer, 0, j)),
            pl.BlockSpec((None, tf, d), lambda i, j: (layer, j, 0)),
        ],
        out_specs=pl.BlockSpec((tm, d), lambda i, j: (i, 0)),
        out_shape=jax.ShapeDtypeStruct((t, d), F32),
        scratch_shapes=[pltpu.VMEM((tm, d), BF16)],
        compiler_params=_params("parallel", "arbitrary"),
        name="swiglu_ffn",
    )(x, g, wg, wu, wd)


def _final_norm_body(x_ref, g_ref, o_ref):
    o_ref[...] = _rms(x_ref[...], g_ref[...])


def _final_norm(x, g, *, tm=1024):
    t, d = x.shape
    return pl.pallas_call(
        _final_norm_body,
        grid=(t // tm,),
        in_specs=[pl.BlockSpec((tm, d), lambda i: (i, 0)), pl.BlockSpec((1, d), lambda i: (0, 0))],
        out_specs=pl.BlockSpec((tm, d), lambda i: (i, 0)),
        out_shape=jax.ShapeDtypeStruct((t, d), F32),
        compiler_params=_params("parallel"),
        name="final_norm",
    )(x, g)


def _outproj_body(n_terms, x_ref, *refs):
    o_ref = refs[-1]
    acc = x_ref[...]
    for i in range(n_terms):
        acc = acc + jnp.dot(refs[2 * i][...], refs[2 * i + 1][...], preferred_element_type=F32)
    o_ref[...] = acc


def _outproj(x, terms, *, tm):
    t, d = x.shape
    in_specs = [pl.BlockSpec((tm, d), lambda i: (i, 0))]
    args = [x]
    for a, a_spec, w in terms:
        in_specs += [a_spec, pl.BlockSpec(w.shape, lambda i: (0, 0))]
        args += [a, w]
    return pl.pallas_call(
        functools.partial(_outproj_body, len(terms)),
        grid=(t // tm,),
        in_specs=in_specs,
        out_specs=pl.BlockSpec((tm, d), lambda i: (i, 0)),
        out_shape=jax.ShapeDtypeStruct((t, d), F32),
        compiler_params=_params("parallel"),
        name="residual_outproj",
    )(*args)


def _even_inproj_body(n_rot, x_ref, g_ref, w_ref, pos_ref, invf_ref, q_ref, k_ref, v_ref, u_ref, xn_ref):
    xn_ref[...] = _rms(x_ref[...], g_ref[...]).astype(BF16)
    ang = pos_ref[...] * invf_ref[...]
    cos, sin = jnp.cos(ang), jnp.sin(ang)
    lane = lax.broadcasted_iota(jnp.int32, ang.shape, 1) % HEAD_DIM
    half = ROPE_DIM // 2
    sin_from_lower = jnp.where((lane >= half) & (lane < ROPE_DIM), sin, 0.0)
    sin_from_upper = jnp.where(lane < half, sin, 0.0)

    cw = 512
    qw, kw, vw = q_ref.shape[-1], k_ref.shape[-1], v_ref.shape[-1]
    for c0 in range(0, w_ref.shape[-1], cw):
        p = jnp.dot(xn_ref[...], w_ref[:, c0:c0 + cw], preferred_element_type=F32)
        for l0 in range(0, cw, LANES):
            col = c0 + l0
            t = p[:, l0:l0 + LANES]
            if col < n_rot:
                t = (t * cos + pltpu.roll(t, half, axis=1) * sin_from_lower
                     - pltpu.roll(t, LANES - half, axis=1) * sin_from_upper)
            if col < qw:
                q_ref[:, col:col + LANES] = t.astype(BF16)
            elif col < qw + kw:
                k_ref[:, col - qw:col - qw + LANES] = t.astype(BF16)
            elif col < qw + kw + vw:
                v_ref[:, col - qw - kw:col - qw - kw + LANES] = t.astype(BF16)
            else:
                c = col - qw - kw - vw
                u_ref[:, c:c + LANES] = t


def _even_inproj(x, g, w, pos_b, invf, *, batch, seq, qw, kw, vw, uw):
    t, d = x.shape
    tm = seq // S5_SEGMENTS
    nseg = S5_SEGMENTS
    return pl.pallas_call(
        functools.partial(_even_inproj_body, qw + kw),
        grid=(t // tm,),
        in_specs=[
            pl.BlockSpec((tm, d), lambda i: (i, 0)),
            pl.BlockSpec((1, d), lambda i: (0, 0)),
            pl.BlockSpec(w.shape, lambda i: (0, 0)),
            pl.BlockSpec((tm, LANES), lambda i: (i, 0)),
            pl.BlockSpec((1, LANES), lambda i: (0, 0)),
        ],
        out_specs=[
            pl.BlockSpec((tm, qw), lambda i: (i, 0)),
            pl.BlockSpec((tm, kw), lambda i: (i, 0)),
            pl.BlockSpec((tm, vw), lambda i: (i, 0)),
            pl.BlockSpec((None, tm, uw), lambda i: (i // nseg, 0, i % nseg)),
        ],
        out_shape=[
            jax.ShapeDtypeStruct((t, qw), BF16),
            jax.ShapeDtypeStruct((t, kw), BF16),
            jax.ShapeDtypeStruct((t, vw), BF16),
            jax.ShapeDtypeStruct((batch, tm, nseg * uw), F32),
        ],
        scratch_shapes=[pltpu.VMEM((tm, d), BF16)],
        compiler_params=_params("parallel"),
        name="even_inproj_rotary",
    )(x, g, w, pos_b, invf)


def _attn_body(pairs_per_kv, q_ref, kc_ref, kp_ref, vc_ref, vp_ref, sink_ref, o_ref):
    m_step = pl.program_id(1)
    blk = ATTN_BLOCK
    kk = jnp.concatenate([kp_ref[...], kc_ref[...]], axis=0)
    vv = jnp.concatenate([vp_ref[...], vc_ref[...]], axis=0)
    lane = lax.broadcasted_iota(jnp.int32, (2 * blk, LANES), 1)
    lower = lane < HEAD_DIM
    row = lax.broadcasted_iota(jnp.int32, (2 * blk, 2 * blk), 0) % blk
    col = lax.broadcasted_iota(jnp.int32, (2 * blk, 2 * blk), 1)
    band = (col > row) & (col <= row + blk)
    n_pairs = q_ref.shape[-1] // LANES
    for t in range(ATTN_BLOCKS_PER_STEP):
        first = (m_step == 0) if t == 0 else False
        valid = band & ((col >= blk) | jnp.logical_not(first))
        for c in range(n_pairs):
            hk = c // pairs_per_kv
            qc = q_ref[t * blk:(t + 1) * blk, c * LANES:(c + 1) * LANES]
            q2 = jnp.concatenate([qc, qc], axis=0)
            lo_rows = lax.broadcasted_iota(jnp.int32, q2.shape, 0) < blk
            q2 = jnp.where(lower[:, :] == lo_rows, q2, jnp.zeros_like(q2))
            kh = kk[t * blk:t * blk + 2 * blk, hk * LANES:(hk + 1) * LANES]
            s = lax.dot_general(q2, kh, (((1,), (1,)), ((), ())), preferred_element_type=F32)
            s = jnp.where(valid, s * (1.0 / math.sqrt(HEAD_DIM)), NEG_INF)
            sink = sink_ref[c]
            mx = jnp.maximum(jnp.max(s, axis=-1, keepdims=True), sink)
            p = jnp.exp(s - mx)
            den = jnp.sum(p, axis=-1, keepdims=True) + jnp.exp(sink - mx)
            p = (p / den).astype(BF16)
            p2 = jnp.concatenate([p[:blk], p[blk:]], axis=1)
            vh = vv[t * blk:t * blk + 2 * blk, hk * LANES:(hk + 1) * LANES]
            v2 = jnp.concatenate([jnp.where(lower, vh, jnp.zeros_like(vh)),
                                  jnp.where(lower, jnp.zeros_like(vh), vh)], axis=0)
            o = jnp.dot(p2, v2, preferred_element_type=F32)
            o_ref[t * blk:(t + 1) * blk, c * LANES:(c + 1) * LANES] = o.astype(BF16)


def _attention(q, kd, vd, sink_rows, *, batch, seq, pairs_per_kv):
    t, qw = q.shape
    kw = kd.shape[-1]
    rows = ATTN_BLOCK * ATTN_BLOCKS_PER_STEP
    steps = seq // rows
    bps = ATTN_BLOCKS_PER_STEP
    nblk = seq // ATTN_BLOCK

    def cur(b, m):
        return (b * steps + m, 0)

    def prev(b, m):
        return (b * nblk + jnp.maximum(m * bps - 1, 0), 0)

    return pl.pallas_call(
        functools.partial(_attn_body, pairs_per_kv),
        grid=(batch, steps),
        in_specs=[
            pl.BlockSpec((rows, qw), cur),
            pl.BlockSpec((rows, kw), cur),
            pl.BlockSpec((ATTN_BLOCK, kw), prev),
            pl.BlockSpec((rows, kw), cur),
            pl.BlockSpec((ATTN_BLOCK, kw), prev),
            pl.BlockSpec(sink_rows.shape, lambda b, m: (0, 0, 0)),
        ],
        out_specs=pl.BlockSpec((rows, qw), cur),
        out_shape=jax.ShapeDtypeStruct((t, qw), BF16),
        compiler_params=_params("parallel", "arbitrary"),
        name="swa_attention",
    )(q, kd, kd, vd, vd, sink_rows)


def _s5_discretise_body(are_ref, aim_ref, logdt_ref, bre_ref, bim_ref, abr_ref, abi_ref, bbr_ref, bbi_ref):
    are, aim = are_ref[...], aim_ref[...]
    dt = jnp.exp(logdt_ref[...])
    mag = jnp.exp(are * dt)
    abar_re, abar_im = mag * jnp.cos(aim * dt), mag * jnp.sin(aim * dt)
    nr, ni = abar_re - 1.0, abar_im
    den = are * are + aim * aim
    coef_re = (nr * are + ni * aim) / den
    coef_im = (ni * are - nr * aim) / den
    bre, bim = bre_ref[...], bim_ref[...]
    abr_ref[...] = abar_re
    abi_ref[...] = abar_im
    bbr_ref[...] = coef_re * bre - coef_im * bim
    bbi_ref[...] = coef_re * bim + coef_im * bre


def _s5_discretise(a_re, a_im, log_dt, b_re, b_im):
    g, p = a_re.shape
    c = b_re.shape[-1]
    col = lambda v: v.reshape(g * p, 1)
    logdt_col = jnp.broadcast_to(log_dt[:, None], (g, p)).reshape(g * p, 1)
    outs = pl.pallas_call(
        _s5_discretise_body,
        out_shape=[jax.ShapeDtypeStruct((g * p, 1), F32)] * 2 + [jax.ShapeDtypeStruct((g * p, c), F32)] * 2,
        name="s5_discretise",
    )(col(a_re), col(a_im), logdt_col, b_re.reshape(g * p, c), b_im.reshape(g * p, c))
    abr, abi, bbr, bbi = outs
    return abr.reshape(g, p), abi.reshape(g, p), bbr.reshape(g, p, c), bbi.reshape(g, p, c)


def _cmul(ar, ai, br, bi):
    return ar * br - ai * bi, ar * bi + ai * br


def _s5_body(n_tiles, steps_per_block, u_ref, wb_ref, wc_ref, are_ref, aim_ref, d_ref, wglu_ref, bglu_ref,
             o_ref, bu_ref, h_ref, state_ref, y_ref):
    phase = pl.program_id(1)
    blk = pl.program_id(2)
    nblk = pl.num_programs(2)
    sw = are_ref.shape[-1]
    ncol = sw // LANES
    cw = wc_ref.shape[-1]

    @pl.when((phase == 0) & (blk == 0))
    def _():
        state_ref[...] = jnp.zeros_like(state_ref)

    def scan_tile(kt, store):
        ub = u_ref[:, kt * cw:(kt + 1) * cw].astype(BF16)
        bu_ref[...] = jnp.dot(ub, wb_ref[kt], preferred_element_type=F32)
        ar = [jnp.broadcast_to(are_ref[kt, :, c * LANES:(c + 1) * LANES], (SUBLANES, LANES)) for c in range(ncol)]
        ai = [jnp.broadcast_to(aim_ref[kt, :, c * LANES:(c + 1) * LANES], (SUBLANES, LANES)) for c in range(ncol)]
        init = tuple(state_ref[kt, :, c * LANES:(c + 1) * LANES] for c in range(2 * ncol))

        def step(j, carry):
            r0 = pl.multiple_of(j * SUBLANES, SUBLANES)
            out = []
            for c in range(ncol):
                hr, hi = carry[c], carry[ncol + c]
                br = bu_ref[pl.ds(r0, SUBLANES), c * LANES:(c + 1) * LANES]
                bi = bu_ref[pl.ds(r0, SUBLANES), sw + c * LANES:sw + (c + 1) * LANES]
                pr, pi = _cmul(ar[c], ai[c], hr, hi)
                out.append((pr + br, pi + bi))
            if store:
                for c in range(ncol):
                    h_ref[pl.ds(r0, SUBLANES), c * LANES:(c + 1) * LANES] = out[c][0]
                    h_ref[pl.ds(r0, SUBLANES), sw + c * LANES:sw + (c + 1) * LANES] = out[c][1]
            return tuple(o[0] for o in out) + tuple(o[1] for o in out)

        final = lax.fori_loop(0, steps_per_block, step, init, unroll=2)
        for c in range(2 * ncol):
            state_ref[kt, :, c * LANES:(c + 1) * LANES] = final[c]

    @pl.when(phase == 0)
    def _():
        for kt in range(n_tiles):
            scan_tile(kt, False)

    @pl.when((phase == 0) & (blk == nblk - 1))
    def _():
        seg_len = steps_per_block * nblk
        for kt in range(n_tiles):
            pr, pi = are_ref[kt], aim_ref[kt]
            k = 1
            while k < seg_len:
                pr, pi = _cmul(pr, pi, pr, pi)
                k *= 2
            fr, fi = state_ref[kt, :, :sw], state_ref[kt, :, sw:]
            hr, hi = jnp.zeros_like(fr), jnp.zeros_like(fi)
            srow = lax.broadcasted_iota(jnp.int32, fr.shape, 0)
            for s in range(1, S5_SEGMENTS):
                qr, qi = _cmul(pr, pi, hr, hi)
                nr = pltpu.roll(fr + qr, 1, axis=0)
                ni = pltpu.roll(fi + qi, 1, axis=0)
                hr = jnp.where(srow == s, nr, hr)
                hi = jnp.where(srow == s, ni, hi)
            state_ref[kt, :, :sw] = hr
            state_ref[kt, :, sw:] = hi

    @pl.when(phase == 1)
    def _():
        for kt in range(n_tiles):
            scan_tile(kt, True)
            y_ref[:, kt * cw:(kt + 1) * cw] = jnp.dot(h_ref[...].astype(BF16), wc_ref[kt],
                                                      preferred_element_type=F32)
        y = y_ref[...] + d_ref[...] * u_ref[...]
        gl = jax.nn.gelu(y)
        gate = jnp.dot(gl.astype(BF16), wglu_ref[...], preferred_element_type=F32) + bglu_ref[...]
        o_ref[...] = (gl * jax.nn.sigmoid(gate)).astype(BF16)


def _s5(u_perm, wb, wc, a_re_t, a_im_t, d_row, w_glu, b_glu, *, rows_per_block=256):
    batch, seq, width = u_perm.shape
    n_tiles, cw, sw2 = wb.shape
    nblk = seq // rows_per_block
    steps = rows_per_block // SUBLANES
    const3 = lambda b, p, k: (0, 0, 0)
    const2 = lambda b, p, k: (0, 0)
    return pl.pallas_call(
        functools.partial(_s5_body, n_tiles, steps),
        grid=(batch, 2, nblk),
        in_specs=[
            pl.BlockSpec((None, rows_per_block, width), lambda b, p, k: (b, k, 0)),
            pl.BlockSpec(wb.shape, const3),
            pl.BlockSpec(wc.shape, const3),
            pl.BlockSpec(a_re_t.shape, const3),
            pl.BlockSpec(a_im_t.shape, const3),
            pl.BlockSpec(d_row.shape, const2),
            pl.BlockSpec(w_glu.shape, const2),
            pl.BlockSpec(b_glu.shape, const2),
        ],
        out_specs=pl.BlockSpec((None, rows_per_block, width), lambda b, p, k: (b, k * p, 0)),
        out_shape=jax.ShapeDtypeStruct((batch, seq, width), BF16),
        scratch_shapes=[
            pltpu.VMEM((rows_per_block, sw2), F32),
            pltpu.VMEM((rows_per_block, sw2), F32),
            pltpu.VMEM((n_tiles, SUBLANES, sw2), F32),
            pltpu.VMEM((rows_per_block, width), F32),
        ],
        compiler_params=_params("parallel", "arbitrary", "arbitrary"),
        name="s5_scan_glu",
    )(u_perm, wb, wc, a_re_t, a_im_t, d_row, w_glu, b_glu)


def _odd_inproj_body(x_ref, g_ref, w_ref, wdt_ref, o_ref, dt_ref, xn_ref):
    @pl.when(pl.program_id(1) == 0)
    def _():
        xn_ref[...] = _rms(x_ref[...], g_ref[...]).astype(BF16)
        dt_ref[...] = jnp.dot(xn_ref[...], wdt_ref[...], preferred_element_type=F32)

    o_ref[...] = jnp.dot(xn_ref[...], w_ref[...], preferred_element_type=F32)


def _odd_inproj(x, g, w, wdt, *, tm=1024, tn=1024):
    t, d = x.shape
    n = w.shape[-1]
    return pl.pallas_call(
        _odd_inproj_body,
        grid=(t // tm, n // tn),
        in_specs=[
            pl.BlockSpec((tm, d), lambda i, j: (i, 0)),
            pl.BlockSpec((1, d), lambda i, j: (0, 0)),
            pl.BlockSpec((d, tn), lambda i, j: (0, j)),
            pl.BlockSpec(wdt.shape, lambda i, j: (0, 0)),
        ],
        out_specs=[
            pl.BlockSpec((tm, tn), lambda i, j: (i, j)),
            pl.BlockSpec((tm, wdt.shape[-1]), lambda i, j: (i, 0)),
        ],
        out_shape=[jax.ShapeDtypeStruct((t, n), F32), jax.ShapeDtypeStruct((t, wdt.shape[-1]), F32)],
        scratch_shapes=[pltpu.VMEM((tm, d), BF16)],
        compiler_params=_params("parallel", "arbitrary"),
        name="odd_inproj",
    )(x, g, w, wdt)


def _ssd_body(n_groups, z_ref, xs_ref, bm_ref, cm_ref, dt_ref, cw_ref, cb_ref, dtb_ref, alog_ref, dsk_ref, ng_ref,
              o_ref, ext_ref, halo_ref, state_ref, y_ref):
    c = pl.program_id(1)
    L = M_CHUNK
    inner = xs_ref.shape[-1]
    gw = bm_ref.shape[-1] // n_groups
    hw = inner // n_groups

    @pl.when(c == 0)
    def _():
        halo_ref[...] = jnp.zeros_like(halo_ref)
        state_ref[...] = jnp.zeros_like(state_ref)

    ext_ref[0:SUBLANES, :] = halo_ref[...]
    ext_ref[SUBLANES:SUBLANES + L, 0:inner] = xs_ref[...]
    ext_ref[SUBLANES:SUBLANES + L, inner:inner + n_groups * gw] = bm_ref[...]
    ext_ref[SUBLANES:SUBLANES + L, inner + n_groups * gw:] = cm_ref[...]
    halo_ref[...] = ext_ref[L:L + SUBLANES, :]
    acc = cb_ref[...] + cw_ref[M_CONV - 1:M_CONV, :] * ext_ref[SUBLANES:SUBLANES + L, :]
    for k in range(1, M_CONV):
        acc = acc + cw_ref[M_CONV - 1 - k:M_CONV - k, :] * ext_ref[SUBLANES - k:SUBLANES - k + L, :]
    ext_ref[SUBLANES:SUBLANES + L, :] = jax.nn.silu(acc)

    def xs_of(c0, w):
        return ext_ref[SUBLANES:SUBLANES + L, c0:c0 + w]

    dt = jax.nn.softplus(dt_ref[...] + dtb_ref[...])
    a = -jnp.exp(alog_ref[...])
    acum = a * dt
    rowi = lax.broadcasted_iota(jnp.int32, acum.shape, 0)
    k = 1
    while k < L:
        acum = acum + jnp.where(rowi >= k, pltpu.roll(acum, k, axis=0), 0.0)
        k *= 2
    acum_t = acum.T
    dt_t = dt.T
    a_last = acum[L - 1:L, :]

    li = lax.broadcasted_iota(jnp.int32, (L, L), 0)
    si = lax.broadcasted_iota(jnp.int32, (L, L), 1)
    causal = li >= si
    lane = lax.broadcasted_iota(jnp.int32, (L, LANES), 1)
    lower = lane < M_HEAD_DIM
    heads_per_group = hw // M_HEAD_DIM

    for g in range(n_groups):
        bg = xs_of(inner + g * gw, gw)
        cg = xs_of(inner + n_groups * gw + g * gw, gw)
        cgb, bgb = cg.astype(BF16), bg.astype(BF16)
        cb = lax.dot_general(cgb, bgb, (((1,), (1,)), ((), ())), preferred_element_type=F32)
        prev = state_ref[:, g * hw:(g + 1) * hw]
        y_off = jnp.dot(cgb, prev.astype(BF16), preferred_element_type=F32)
        xd_parts = []
        for pr in range(heads_per_group // 2):
            h0 = g * heads_per_group + 2 * pr
            col0 = g * hw + pr * LANES
            xp = xs_of(col0, LANES)
            ms = []
            for h in (h0, h0 + 1):
                seg = acum[:, h:h + 1] - acum_t[h:h + 1, :]
                decay = jnp.exp(jnp.where(causal, seg, -jnp.inf))
                ms.append((cb * decay * dt_t[h:h + 1, :]).astype(BF16))
            lhs = jnp.concatenate(ms, axis=1)
            xpb = xp.astype(BF16)
            rhs = jnp.concatenate([jnp.where(lower, xpb, jnp.zeros_like(xpb)),
                                   jnp.where(lower, jnp.zeros_like(xpb), xpb)], axis=0)
            y_diag = jnp.dot(lhs, rhs, preferred_element_type=F32)
            al = jnp.where(lower, acum[:, h0:h0 + 1], acum[:, h0 + 1:h0 + 2])
            dtl = jnp.where(lower, dt[:, h0:h0 + 1], dt[:, h0 + 1:h0 + 2])
            alast = jnp.where(lower[0:1], a_last[:, h0:h0 + 1], a_last[:, h0 + 1:h0 + 2])
            y_ref[:, col0:col0 + LANES] = y_diag + jnp.exp(al) * y_off[:, pr * LANES:(pr + 1) * LANES]
            xd_parts.append((xp * (dtl * jnp.exp(alast - al))).astype(BF16))
            state_ref[:, col0:col0 + LANES] = state_ref[:, col0:col0 + LANES] * jnp.exp(alast)
        xd = jnp.concatenate(xd_parts, axis=1)
        state_ref[:, g * hw:(g + 1) * hw] += lax.dot_general(
            bgb, xd, (((0,), (0,)), ((), ())), preferred_element_type=F32)

    for g in range(n_groups):
        sl = slice(g * hw, (g + 1) * hw)
        y = y_ref[:, sl] + dsk_ref[:, sl] * xs_of(g * hw, hw)
        y = y * jax.nn.silu(z_ref[:, sl])
        y = y * lax.rsqrt(jnp.mean(y * y, axis=-1, keepdims=True) + NORM_EPS)
        o_ref[:, sl] = (y * ng_ref[:, sl]).astype(BF16)


def _ssd(zxbc, dt_raw, conv_w, conv_b, dt_bias, a_log, d_row, norm_g, *, batch, seq, inner, n_groups):
    t = zxbc.shape[0]
    nchunks = seq // M_CHUNK
    gn = n_groups * M_STATE
    conv_dim = inner + 2 * gn
    row = lambda b, c: (b * nchunks + c, 0)
    const = lambda b, c: (0, 0)
    return pl.pallas_call(
        functools.partial(_ssd_body, n_groups),
        grid=(batch, nchunks),
        in_specs=[
            pl.BlockSpec((M_CHUNK, inner), row),
            pl.BlockSpec((M_CHUNK, inner), lambda b, c: (b * nchunks + c, 1)),
            pl.BlockSpec((M_CHUNK, gn), lambda b, c: (b * nchunks + c, 2 * inner // gn)),
            pl.BlockSpec((M_CHUNK, gn), lambda b, c: (b * nchunks + c, 2 * inner // gn + 1)),
            pl.BlockSpec((M_CHUNK, LANES), row),
            pl.BlockSpec((M_CONV, conv_dim), const),
            pl.BlockSpec((1, conv_dim), const),
            pl.BlockSpec((1, LANES), const),
            pl.BlockSpec((1, LANES), const),
            pl.BlockSpec((1, inner), const),
            pl.BlockSpec((1, inner), const),
        ],
        out_specs=pl.BlockSpec((M_CHUNK, inner), row),
        out_shape=jax.ShapeDtypeStruct((t, inner), BF16),
        scratch_shapes=[
            pltpu.VMEM((M_CHUNK + SUBLANES, conv_dim), F32),
            pltpu.VMEM((SUBLANES, conv_dim), F32),
            pltpu.VMEM((M_STATE, inner), F32),
            pltpu.VMEM((M_CHUNK, inner), F32),
        ],
        compiler_params=_params("parallel", "arbitrary"),
        name="ssd_chunk",
    )(zxbc, zxbc, zxbc, zxbc, dt_raw, conv_w, conv_b, dt_bias, a_log, d_row, norm_g)


def _dup_heads(w, n_heads):
    d = w.shape[0]
    w = w.reshape(d, n_heads, 1, HEAD_DIM)
    return jnp.broadcast_to(w, (d, n_heads, 2, HEAD_DIM)).reshape(d, n_heads * 2 * HEAD_DIM)


def _block_diag_tiles(blocks, n_tiles):
    g, r, c = blocks.shape
    gpt = g // n_tiles
    eye = jnp.eye(gpt, dtype=blocks.dtype)
    b = blocks.reshape(n_tiles, gpt, r, c)
    out = jnp.einsum('tgrc,gh->tgrhc', b, eye)
    return out.reshape(n_tiles, gpt * r, gpt * c)


def _rope_inv_freq_row():
    half = ROPE_DIM // 2
    inv_freq = jnp.exp(-math.log(ROPE_THETA) * jnp.arange(half, dtype=F32) * (2.0 / ROPE_DIM))
    lane = jnp.arange(LANES) % HEAD_DIM
    return jnp.where(lane < ROPE_DIM, inv_freq[lane % half], 0.0).reshape(1, LANES).astype(F32)


def _even_mixer(x, pos_b, norm_g, w_in, sinks, a_re, a_im, log_dt, b_re, b_im, c_re, c_im, d_skip, w_glu, b_glu,
                w_out, *, batch, seq):
    d = x.shape[-1]
    n_q = sinks.shape[0]
    a_width = n_q * HEAD_DIM
    groups, state = a_re.shape
    s5_width = groups * S5_GROUP
    kv_width = (w_in.shape[-1] - a_width - s5_width) // 2
    n_kv = kv_width // HEAD_DIM
    grp = n_q // n_kv
    assert grp % 2 == 0 and a_width % LANES == 0

    wq = w_in[:, :a_width]
    wk = _dup_heads(w_in[:, a_width:a_width + kv_width], n_kv)
    wv = _dup_heads(w_in[:, a_width + kv_width:a_width + 2 * kv_width], n_kv)
    wu = w_in[:, a_width + 2 * kv_width:]
    w_all = jnp.concatenate([wq, wk, wv, wu], axis=1).astype(BF16)
    q, kd, vd, u_cat = _even_inproj(x, norm_g, w_all, pos_b, _rope_inv_freq_row(), batch=batch, seq=seq,
                                    qw=a_width, kw=2 * kv_width, vw=2 * kv_width, uw=s5_width)

    sink_rows = jnp.repeat(sinks.astype(F32).reshape(n_q // 2, 2), ATTN_BLOCK, axis=1).reshape(n_q // 2, 2 * ATTN_BLOCK, 1)
    attn = _attention(q, kd, vd, sink_rows, batch=batch, seq=seq, pairs_per_kv=grp // 2)

    abr, abi, bbr, bbi = _s5_discretise(a_re.astype(F32), a_im.astype(F32), log_dt.astype(F32),
                                        b_re.astype(F32), b_im.astype(F32))
    n_tiles = groups // S5_GROUPS_PER_TILE
    to_in = lambda bb: _block_diag_tiles(jnp.swapaxes(bb, 1, 2), n_tiles)
    wb = jnp.concatenate([to_in(bbr), to_in(bbi)], axis=-1).astype(BF16)
    to_out = lambda cc: _block_diag_tiles(jnp.swapaxes(cc.astype(F32), 1, 2), n_tiles)
    wc = jnp.concatenate([to_out(c_re), -to_out(c_im)], axis=1).astype(BF16)
    a_re_t = abr.reshape(n_tiles, 1, -1)
    a_im_t = abi.reshape(n_tiles, 1, -1)

    seg_len = seq // S5_SEGMENTS
    u_perm = u_cat.reshape(batch, seg_len, S5_SEGMENTS, s5_width).reshape(batch, seq, s5_width)
    ssm_perm = _s5(u_perm, wb, wc, a_re_t, a_im_t, d_skip.astype(F32).reshape(1, s5_width),
                   w_glu.astype(BF16), b_glu.astype(F32).reshape(1, s5_width))
    ssm_cat = ssm_perm.reshape(batch, seg_len, S5_SEGMENTS * s5_width)

    tm = seg_len
    nseg = S5_SEGMENTS
    w_o = w_out.astype(BF16)
    terms = [
        (attn, pl.BlockSpec((tm, a_width), lambda i: (i, 0)), w_o[:a_width]),
        (ssm_cat, pl.BlockSpec((None, tm, s5_width), lambda i: (i // nseg, 0, i % nseg)), w_o[a_width:]),
    ]
    return _outproj(x, terms, tm=tm)


def _odd_mixer(x, norm_g, w_in, conv_w, conv_b, dt_bias, a_log, d_skip, norm_out, w_out, *, batch, seq):
    inner = norm_out.shape[0]
    heads = a_log.shape[0]
    conv_dim = conv_w.shape[-1]
    n_groups = (conv_dim - inner) // (2 * M_STATE)
    assert heads <= LANES and inner // heads == M_HEAD_DIM
    pad = LANES - heads
    w_main = w_in[:, :inner + conv_dim].astype(BF16)
    w_dt = jnp.pad(w_in[:, inner + conv_dim:], ((0, 0), (0, pad))).astype(BF16)
    zxbc, dt_raw = _odd_inproj(x, norm_g, w_main, w_dt)
    row = lambda v: jnp.pad(v.astype(F32), (0, pad)).reshape(1, LANES)
    y = _ssd(zxbc, dt_raw, conv_w.astype(F32), conv_b.astype(F32).reshape(1, conv_dim), row(dt_bias), row(a_log),
             jnp.repeat(d_skip.astype(F32), M_HEAD_DIM).reshape(1, inner), norm_out.astype(F32).reshape(1, inner),
             batch=batch, seq=seq, inner=inner, n_groups=n_groups)
    tm = 512
    return _outproj(x, [(y, pl.BlockSpec((tm, inner), lambda i: (i, 0)), w_out.astype(BF16))], tm=tm)


def kernel(x, positions, norm_ffn1, ffn1_gate, ffn1_up, ffn1_down, norm_mix, norm_ffn2, ffn2_gate, ffn2_up,
           ffn2_down, ev_w_in, ev_sinks, s5_a_re, s5_a_im, s5_log_dt, s5_b_re, s5_b_im, s5_c_re, s5_c_im, s5_d,
           s5_w_glu, s5_b_glu, ev_w_out, m_w_in, m_conv_w, m_conv_b, m_dt_bias, m_a_log, m_d, m_norm, m_w_out,
           final_norm):
    batch, seq, d = x.shape
    depth = norm_ffn1.shape[0]
    t = batch * seq
    h = x.reshape(t, d).astype(F32)
    pos_b = jnp.broadcast_to(positions.reshape(t, 1).astype(F32), (t, LANES))
    g1, g2, gm = (v.astype(F32).reshape(depth, 1, d) for v in (norm_ffn1, norm_ffn2, norm_mix))
    w1 = tuple(w.astype(BF16) for w in (ffn1_gate, ffn1_up, ffn1_down))
    w2 = tuple(w.astype(BF16) for w in (ffn2_gate, ffn2_up, ffn2_down))
    for layer in range(depth):
        h = _ffn(h, g1, *w1, layer)
        if layer % 2 == 0:
            e = layer // 2
            h = _even_mixer(h, pos_b, gm[layer], ev_w_in[e], ev_sinks[e], s5_a_re[e], s5_a_im[e], s5_log_dt[e],
                            s5_b_re[e], s5_b_im[e], s5_c_re[e], s5_c_im[e], s5_d[e].reshape(-1), s5_w_glu[e],
                            s5_b_glu[e], ev_w_out[e], batch=batch, seq=seq)
        else:
            o = layer // 2
            h = _odd_mixer(h, gm[layer], m_w_in[o], m_conv_w[o], m_conv_b[o], m_dt_bias[o], m_a_log[o], m_d[o],
                           m_norm[o], m_w_out[o], batch=batch, seq=seq)
        h = _ffn(h, g2, *w2, layer)
    out = _final_norm(h, final_norm.astype(F32).reshape(1, d))
    return out.reshape(batch, seq, d).astype(x.dtype)
```

```python
import functools
import math

import jax
import jax.numpy as jnp
from jax import lax
from jax.experimental import pallas as pl
from jax.experimental.pallas import tpu as pltpu

F32 = jnp.float32
BF16 = jnp.bfloat16

NORM_EPS = 1e-5
NEG_INF = -1e30
LANES = 128
SUBLANES = 8
VMEM_LIMIT_BYTES = 56 * 1024 * 1024

HEAD_DIM = 64
ATTN_BLOCK = 128
ROPE_THETA = 500000.0
ROPE_DIM = HEAD_DIM // 4
ATTN_BLOCKS_PER_STEP = 4
S5_GROUP = 16
S5_STATE = 64
S5_SEGMENTS = SUBLANES
S5_GROUPS_PER_TILE = 16
M_HEAD_DIM = 64
M_STATE = 128
M_CHUNK = 128
M_CONV = 4


def _params(*sem):
    return pltpu.CompilerParams(dimension_semantics=sem, vmem_limit_bytes=VMEM_LIMIT_BYTES)


def _rms(x, g):
    return x * lax.rsqrt(jnp.mean(x * x, axis=-1, keepdims=True) + NORM_EPS) * g


def _ffn_body(final, x_ref, g_ref, wg_ref, wu_ref, wd_ref, *rest):
    o_ref, xn_ref = rest[-2:]
    j = pl.program_id(1)

    @pl.when(j == 0)
    def _():
        x = x_ref[...]
        xn_ref[...] = _rms(x, g_ref[...]).astype(BF16)
        o_ref[...] = x

    xn = xn_ref[...]
    gate = jnp.dot(xn, wg_ref[...], preferred_element_type=F32)
    up = jnp.dot(xn, wu_ref[...], preferred_element_type=F32)
    h = (jax.nn.silu(gate) * up * 0.5).astype(BF16)
    o_ref[...] += jnp.dot(h, wd_ref[...], preferred_element_type=F32)

    if final:
        @pl.when(j == pl.num_programs(1) - 1)
        def _():
            o_ref[...] = _rms(o_ref[...], rest[0][...])


def _ffn(x, g, wg, wu, wd, layer, final_g=None, *, tm=1024, tf=512):
    t, d = x.shape
    dff = wg.shape[-1]
    in_specs = [
        pl.BlockSpec((tm, d), lambda i, j: (i, 0)),
        pl.BlockSpec((None, 1, d), lambda i, j: (layer, 0, 0)),
        pl.BlockSpec((None, d, tf), lambda i, j: (layer, 0, j)),
        pl.BlockSpec((None, d, tf), lambda i, j: (layer, 0, j)),
        pl.BlockSpec((None, tf, d), lambda i, j: (layer, j, 0)),
    ]
    args = [x, g, wg, wu, wd]
    if final_g is not None:
        in_specs.append(pl.BlockSpec((1, d), lambda i, j: (0, 0)))
        args.append(final_g)
    return pl.pallas_call(
        functools.partial(_ffn_body, final_g is not None),
        grid=(t // tm, dff // tf),
        in_specs=in_specs,
        out_specs=pl.BlockSpec((tm, d), lambda i, j: (i, 0)),
        out_shape=jax.ShapeDtypeStruct((t, d), F32),
        scratch_shapes=[pltpu.VMEM((tm, d), BF16)],
        compiler_params=_params("parallel", "arbitrary"),
        name="swiglu_ffn",
    )(*args)


def _outproj_body(n_terms, x_ref, *refs):
    o_ref = refs[-1]
    acc = x_ref[...]
    for i in range(n_terms):
        acc = acc + jnp.dot(refs[2 * i][...], refs[2 * i + 1][...], preferred_element_type=F32)
    o_ref[...] = acc


def _outproj(x, terms, *, tm):
    t, d = x.shape
    in_specs = [pl.BlockSpec((tm, d), lambda i: (i, 0))]
    args = [x]
    for a, a_spec, w in terms:
        in_specs += [a_spec, pl.BlockSpec(w.shape, lambda i: (0, 0))]
        args += [a, w]
    return pl.pallas_call(
        functools.partial(_outproj_body, len(terms)),
        grid=(t // tm,),
        in_specs=in_specs,
        out_specs=pl.BlockSpec((tm, d), lambda i: (i, 0)),
        out_shape=jax.ShapeDtypeStruct((t, d), F32),
        compiler_params=_params("parallel"),
        name="residual_outproj",
    )(*args)


def _even_inproj_body(n_rot, x_ref, g_ref, w_ref, pos_ref, invf_ref, q_ref, k_ref, v_ref, u_ref, xn_ref):
    xn_ref[...] = _rms(x_ref[...], g_ref[...]).astype(BF16)
    ang = pos_ref[...] * invf_ref[...]
    cos, sin = jnp.cos(ang), jnp.sin(ang)
    lane = lax.broadcasted_iota(jnp.int32, ang.shape, 1) % HEAD_DIM
    half = ROPE_DIM // 2
    sin_from_lower = jnp.where((lane >= half) & (lane < ROPE_DIM), sin, 0.0)
    sin_from_upper = jnp.where(lane < half, sin, 0.0)

    cw = 512
    qw, kw, vw = q_ref.shape[-1], k_ref.shape[-1], v_ref.shape[-1]
    for c0 in range(0, w_ref.shape[-1], cw):
        p = jnp.dot(xn_ref[...], w_ref[:, c0:c0 + cw], preferred_element_type=F32)
        for l0 in range(0, cw, LANES):
            col = c0 + l0
            t = p[:, l0:l0 + LANES]
            if col < n_rot:
                t = (t * cos + pltpu.roll(t, half, axis=1) * sin_from_lower
                     - pltpu.roll(t, LANES - half, axis=1) * sin_from_upper)
            if col < qw:
                q_ref[:, col:col + LANES] = t.astype(BF16)
            elif col < qw + kw:
                k_ref[:, col - qw:col - qw + LANES] = t.astype(BF16)
            elif col < qw + kw + vw:
                v_ref[:, col - qw - kw:col - qw - kw + LANES] = t.astype(BF16)
            else:
                c = col - qw - kw - vw
                u_ref[:, c:c + LANES] = t


def _even_inproj(x, g, w, pos_b, invf, *, batch, seq, qw, kw, vw, uw):
    t, d = x.shape
    tm = seq // S5_SEGMENTS
    nseg = S5_SEGMENTS
    return pl.pallas_call(
        functools.partial(_even_inproj_body, qw + kw),
        grid=(t // tm,),
        in_specs=[
            pl.BlockSpec((tm, d), lambda i: (i, 0)),
            pl.BlockSpec((1, d), lambda i: (0, 0)),
            pl.BlockSpec(w.shape, lambda i: (0, 0)),
            pl.BlockSpec((tm, LANES), lambda i: (i, 0)),
            pl.BlockSpec((1, LANES), lambda i: (0, 0)),
        ],
        out_specs=[
            pl.BlockSpec((tm, qw), lambda i: (i, 0)),
            pl.BlockSpec((tm, kw), lambda i: (i, 0)),
            pl.BlockSpec((tm, vw), lambda i: (i, 0)),
            pl.BlockSpec((None, tm, uw), lambda i: (i // nseg, 0, i % nseg)),
        ],
        out_shape=[
            jax.ShapeDtypeStruct((t, qw), BF16),
            jax.ShapeDtypeStruct((t, kw), BF16),
            jax.ShapeDtypeStruct((t, vw), BF16),
            jax.ShapeDtypeStruct((batch, tm, nseg * uw), F32),
        ],
        scratch_shapes=[pltpu.VMEM((tm, d), BF16)],
        compiler_params=_params("parallel"),
        name="even_inproj_rotary",
    )(x, g, w, pos_b, invf)


def _attn_body(pairs_per_kv, q_ref, kc_ref, kp_ref, vc_ref, vp_ref, sink_ref, o_ref):
    m_step = pl.program_id(1)
    blk = ATTN_BLOCK
    kk = jnp.concatenate([kp_ref[...], kc_ref[...]], axis=0)
    vv = jnp.concatenate([vp_ref[...], vc_ref[...]], axis=0)
    lane = lax.broadcasted_iota(jnp.int32, (2 * blk, LANES), 1)
    lower = lane < HEAD_DIM
    row = lax.broadcasted_iota(jnp.int32, (2 * blk, 2 * blk), 0) % blk
    col = lax.broadcasted_iota(jnp.int32, (2 * blk, 2 * blk), 1)
    band = (col > row) & (col <= row + blk)
    n_pairs = q_ref.shape[-1] // LANES
    for t in range(ATTN_BLOCKS_PER_STEP):
        first = (m_step == 0) if t == 0 else False
        valid = band & ((col >= blk) | jnp.logical_not(first))
        for c in range(n_pairs):
            hk = c // pairs_per_kv
            qc = q_ref[t * blk:(t + 1) * blk, c * LANES:(c + 1) * LANES]
            q2 = jnp.concatenate([qc, qc], axis=0)
            lo_rows = lax.broadcasted_iota(jnp.int32, q2.shape, 0) < blk
            q2 = jnp.where(lower[:, :] == lo_rows, q2, jnp.zeros_like(q2))
            kh = kk[t * blk:t * blk + 2 * blk, hk * LANES:(hk + 1) * LANES]
            s = lax.dot_general(q2, kh, (((1,), (1,)), ((), ())), preferred_element_type=F32)
            s = jnp.where(valid, s * (1.0 / math.sqrt(HEAD_DIM)), NEG_INF)
            sink = sink_ref[c]
            mx = jnp.maximum(jnp.max(s, axis=-1, keepdims=True), sink)
            p = jnp.exp(s - mx)
            den = jnp.sum(p, axis=-1, keepdims=True) + jnp.exp(sink - mx)
            p = (p / den).astype(BF16)
            p2 = jnp.concatenate([p[:blk], p[blk:]], axis=1)
            vh = vv[t * blk:t * blk + 2 * blk, hk * LANES:(hk + 1) * LANES]
            v2 = jnp.concatenate([jnp.where(lower, vh, jnp.zeros_like(vh)),
                                  jnp.where(lower, jnp.zeros_like(vh), vh)], axis=0)
            o = jnp.dot(p2, v2, preferred_element_type=F32)
            o_ref[t * blk:(t + 1) * blk, c * LANES:(c + 1) * LANES] = o.astype(BF16)


def _attention(q, kd, vd, sink_rows, *, batch, seq, pairs_per_kv):
    t, qw = q.shape
    kw = kd.shape[-1]
    rows = ATTN_BLOCK * ATTN_BLOCKS_PER_STEP
    steps = seq // rows
    bps = ATTN_BLOCKS_PER_STEP
    nblk = seq // ATTN_BLOCK

    def cur(b, m):
        return (b * steps + m, 0)

    def prev(b, m):
        return (b * nblk + jnp.maximum(m * bps - 1, 0), 0)

    return pl.pallas_call(
        functools.partial(_attn_body, pairs_per_kv),
        grid=(batch, steps),
        in_specs=[
            pl.BlockSpec((rows, qw), cur),
            pl.BlockSpec((rows, kw), cur),
            pl.BlockSpec((ATTN_BLOCK, kw), prev),
            pl.BlockSpec((rows, kw), cur),
            pl.BlockSpec((ATTN_BLOCK, kw), prev),
            pl.BlockSpec(sink_rows.shape, lambda b, m: (0, 0, 0)),
        ],
        out_specs=pl.BlockSpec((rows, qw), cur),
        out_shape=jax.ShapeDtypeStruct((t, qw), BF16),
        compiler_params=_params("parallel", "arbitrary"),
        name="swa_attention",
    )(q, kd, kd, vd, vd, sink_rows)


def _s5_discretise_body(are_ref, aim_ref, logdt_ref, bre_ref, bim_ref, abr_ref, abi_ref, bbr_ref, bbi_ref):
    are, aim = are_ref[...], aim_ref[...]
    dt = jnp.exp(logdt_ref[...])
    mag = jnp.exp(are * dt)
    abar_re, abar_im = mag * jnp.cos(aim * dt), mag * jnp.sin(aim * dt)
    nr, ni = abar_re - 1.0, abar_im
    den = are * are + aim * aim
    coef_re = (nr * are + ni * aim) / den
    coef_im = (ni * are - nr * aim) / den
    bre, bim = bre_ref[...], bim_ref[...]
    abr_ref[...] = abar_re
    abi_ref[...] = abar_im
    bbr_ref[...] = coef_re * bre - coef_im * bim
    bbi_ref[...] = coef_re * bim + coef_im * bre


def _s5_discretise(a_re, a_im, log_dt, b_re, b_im):
    g, p = a_re.shape
    c = b_re.shape[-1]
    col = lambda v: v.reshape(g * p, 1)
    logdt_col = jnp.broadcast_to(log_dt[:, None], (g, p)).reshape(g * p, 1)
    outs = pl.pallas_call(
        _s5_discretise_body,
        out_shape=[jax.ShapeDtypeStruct((g * p, 1), F32)] * 2 + [jax.ShapeDtypeStruct((g * p, c), F32)] * 2,
        name="s5_discretise",
    )(col(a_re), col(a_im), logdt_col, b_re.reshape(g * p, c), b_im.reshape(g * p, c))
    abr, abi, bbr, bbi = outs
    return abr.reshape(g, p), abi.reshape(g, p), bbr.reshape(g, p, c), bbi.reshape(g, p, c)


def _cmul(ar, ai, br, bi):
    return ar * br - ai * bi, ar * bi + ai * br


def _s5_body(n_tiles, steps_per_block, u_ref, wb_ref, wc_ref, are_ref, aim_ref, d_ref, wglu_ref, bglu_ref,
             o_ref, us_ref, ys_ref, state_ref):
    phase = pl.program_id(1)
    blk = pl.program_id(2)
    nblk = pl.num_programs(2)
    sw = are_ref.shape[-1]
    ncol = sw // LANES
    cw = wc_ref.shape[-1]
    nslab = us_ref.shape[0]
    nseg = S5_SEGMENTS

    @pl.when((phase == 0) & (blk == 0))
    def _():
        state_ref[...] = jnp.zeros_like(state_ref)

    for s in range(nseg):
        for c in range(nslab):
            us_ref[c, pl.ds(s, steps_per_block, stride=nseg), :] = u_ref[:, (s * nslab + c) * LANES:
                                                                         (s * nslab + c + 1) * LANES]

    def u_cols(c0, n):
        return jnp.concatenate([us_ref[c0 + i] for i in range(n)], axis=1)

    def scan_tile(kt, store):
        ub = u_cols(kt * (cw // LANES), cw // LANES).astype(BF16)
        bu = jnp.dot(ub, wb_ref[kt], preferred_element_type=F32)
        ar = [jnp.broadcast_to(are_ref[kt, :, c * LANES:(c + 1) * LANES], (SUBLANES, LANES)) for c in range(ncol)]
        ai = [jnp.broadcast_to(aim_ref[kt, :, c * LANES:(c + 1) * LANES], (SUBLANES, LANES)) for c in range(ncol)]
        h = [state_ref[kt, :, c * LANES:(c + 1) * LANES] for c in range(2 * ncol)]
        rows_out = []
        for j in range(steps_per_block):
            r0 = j * SUBLANES
            hr_new, hi_new = [], []
            for c in range(ncol):
                pr, pi = _cmul(ar[c], ai[c], h[c], h[ncol + c])
                hr_new.append(pr + bu[r0:r0 + SUBLANES, c * LANES:(c + 1) * LANES])
                hi_new.append(pi + bu[r0:r0 + SUBLANES, sw + c * LANES:sw + (c + 1) * LANES])
            h = hr_new + hi_new
            if store:
                rows_out.append(jnp.concatenate(h, axis=1))
        for c in range(2 * ncol):
            state_ref[kt, :, c * LANES:(c + 1) * LANES] = h[c]
        return jnp.concatenate(rows_out, axis=0) if store else None

    @pl.when(phase == 0)
    def _():
        for kt in range(n_tiles):
            scan_tile(kt, False)

    @pl.when((phase == 0) & (blk == nblk - 1))
    def _():
        seg_len = steps_per_block * nblk
        for kt in range(n_tiles):
            pr, pi = are_ref[kt], aim_ref[kt]
            k = 1
            while k < seg_len:
                pr, pi = _cmul(pr, pi, pr, pi)
                k *= 2
            fr, fi = state_ref[kt, :, :sw], state_ref[kt, :, sw:]
            hr, hi = jnp.zeros_like(fr), jnp.zeros_like(fi)
            srow = lax.broadcasted_iota(jnp.int32, fr.shape, 0)
            for s in range(1, S5_SEGMENTS):
                qr, qi = _cmul(pr, pi, hr, hi)
                nr = pltpu.roll(fr + qr, 1, axis=0)
                ni = pltpu.roll(fi + qi, 1, axis=0)
                hr = jnp.where(srow == s, nr, hr)
                hi = jnp.where(srow == s, ni, hi)
            state_ref[kt, :, :sw] = hr
            state_ref[kt, :, sw:] = hi

    @pl.when(phase == 1)
    def _():
        ys = [jnp.dot(scan_tile(kt, True).astype(BF16), wc_ref[kt], preferred_element_type=F32)
              for kt in range(n_tiles)]
        y = jnp.concatenate(ys, axis=1) + d_ref[...] * u_cols(0, nslab)
        gl = jax.nn.gelu(y)
        gate = jnp.dot(gl.astype(BF16), wglu_ref[...], preferred_element_type=F32) + bglu_ref[...]
        out = gl * jax.nn.sigmoid(gate)
        for c in range(nslab):
            ys_ref[c] = out[:, c * LANES:(c + 1) * LANES]
        for s in range(nseg):
            for c in range(nslab):
                o_ref[:, (s * nslab + c) * LANES:(s * nslab + c + 1) * LANES] = (
                    ys_ref[c, pl.ds(s, steps_per_block, stride=nseg), :].astype(BF16))


def _s5(u_cat, wb, wc, a_re_t, a_im_t, d_row, w_glu, b_glu, *, steps_per_block=32):
    batch, seg_len, cat_width = u_cat.shape
    width = cat_width // S5_SEGMENTS
    n_tiles, cw, sw2 = wb.shape
    nblk = seg_len // steps_per_block
    steps = steps_per_block
    rows_per_block = steps * S5_SEGMENTS
    assert seg_len & (seg_len - 1) == 0, "segment length must be a power of two (abar ** seg_len by squaring)"
    const3 = lambda b, p, k: (0, 0, 0)
    const2 = lambda b, p, k: (0, 0)
    return pl.pallas_call(
        functools.partial(_s5_body, n_tiles, steps),
        grid=(batch, 2, nblk),
        in_specs=[
            pl.BlockSpec((None, steps, cat_width), lambda b, p, k: (b, k, 0)),
            pl.BlockSpec(wb.shape, const3),
            pl.BlockSpec(wc.shape, const3),
            pl.BlockSpec(a_re_t.shape, const3),
            pl.BlockSpec(a_im_t.shape, const3),
            pl.BlockSpec(d_row.shape, const2),
            pl.BlockSpec(w_glu.shape, const2),
            pl.BlockSpec(b_glu.shape, const2),
        ],
        out_specs=pl.BlockSpec((None, steps, cat_width), lambda b, p, k: (b, k * p, 0)),
        out_shape=jax.ShapeDtypeStruct((batch, seg_len, cat_width), BF16),
        scratch_shapes=[
            pltpu.VMEM((width // LANES, rows_per_block, LANES), F32),
            pltpu.VMEM((width // LANES, rows_per_block, LANES), F32),
            pltpu.VMEM((n_tiles, SUBLANES, sw2), F32),
        ],
        compiler_params=_params("parallel", "arbitrary", "arbitrary"),
        name="s5_scan_glu",
    )(u_cat, wb, wc, a_re_t, a_im_t, d_row, w_glu, b_glu)


HALO_ROWS = 16


def _odd_inproj_body(n_plain, tiles_per_seq, x_ref, xh_ref, g_ref, w_ref, wdt_ref, cw_ref, cb_ref,
                     o_ref, dt_ref, xn_ref):
    i, j = pl.program_id(0), pl.program_id(1)

    @pl.when(j == 0)
    def _():
        g = g_ref[...]
        xn_ref[0:HALO_ROWS, :] = _rms(xh_ref[...], g).astype(BF16)
        xn_ref[HALO_ROWS:, :] = _rms(x_ref[...], g).astype(BF16)
        dt_ref[...] = jnp.dot(xn_ref[HALO_ROWS:, :], wdt_ref[...], preferred_element_type=F32)

    @pl.when(j < n_plain)
    def _():
        o_ref[...] = jnp.dot(xn_ref[HALO_ROWS:, :], w_ref[...], preferred_element_type=F32)

    @pl.when(j >= n_plain)
    def _():
        p = jnp.dot(xn_ref[...], w_ref[...], preferred_element_type=F32)
        rowi = lax.broadcasted_iota(jnp.int32, p.shape, 0)
        seq_start = (i % tiles_per_seq) == 0
        ext = jnp.where((rowi >= HALO_ROWS) | jnp.logical_not(seq_start), p, 0.0)
        acc = cb_ref[...] + cw_ref[M_CONV - 1:M_CONV, :] * p[HALO_ROWS:]
        for k in range(1, M_CONV):
            acc = acc + cw_ref[M_CONV - 1 - k:M_CONV - k, :] * ext[HALO_ROWS - k:ext.shape[0] - k]
        o_ref[...] = jax.nn.silu(acc)


def _odd_inproj(x, g, w, wdt, conv_w, conv_b, *, seq, n_plain_cols, tm=1024, tn=1024):
    t, d = x.shape
    n = w.shape[-1]
    assert seq % tm == 0 and n_plain_cols % tn == 0
    n_plain = n_plain_cols // tn
    halo_blocks = tm // HALO_ROWS
    conv_col = lambda i, j: (0, jnp.maximum(j - n_plain, 0))
    return pl.pallas_call(
        functools.partial(_odd_inproj_body, n_plain, seq // tm),
        grid=(t // tm, n // tn),
        in_specs=[
            pl.BlockSpec((tm, d), lambda i, j: (i, 0)),
            pl.BlockSpec((HALO_ROWS, d), lambda i, j: (jnp.maximum(i * halo_blocks - 1, 0), 0)),
            pl.BlockSpec((1, d), lambda i, j: (0, 0)),
            pl.BlockSpec((d, tn), lambda i, j: (0, j)),
            pl.BlockSpec(wdt.shape, lambda i, j: (0, 0)),
            pl.BlockSpec((M_CONV, tn), conv_col),
            pl.BlockSpec((1, tn), conv_col),
        ],
        out_specs=[
            pl.BlockSpec((tm, tn), lambda i, j: (i, j)),
            pl.BlockSpec((tm, wdt.shape[-1]), lambda i, j: (i, 0)),
        ],
        out_shape=[jax.ShapeDtypeStruct((t, n), F32), jax.ShapeDtypeStruct((t, wdt.shape[-1]), F32)],
        scratch_shapes=[pltpu.VMEM((HALO_ROWS + tm, d), BF16)],
        compiler_params=_params("parallel", "arbitrary"),
        name="odd_inproj_conv",
    )(x, x, g, w, wdt, conv_w, conv_b)


def _ssd_body(n_groups, xs_ref, bm_ref, cm_ref, dt_ref, dtb_ref, alog_ref, dsk_ref, o_ref, state_ref):
    c = pl.program_id(1)
    L = M_CHUNK
    inner = xs_ref.shape[-1]
    gw = bm_ref.shape[-1] // n_groups
    hw = inner // n_groups

    @pl.when(c == 0)
    def _():
        state_ref[...] = jnp.zeros_like(state_ref)

    dt = jax.nn.softplus(dt_ref[...] + dtb_ref[...])
    a = -jnp.exp(alog_ref[...])
    acum = a * dt
    rowi = lax.broadcasted_iota(jnp.int32, acum.shape, 0)
    k = 1
    while k < L:
        acum = acum + jnp.where(rowi >= k, pltpu.roll(acum, k, axis=0), 0.0)
        k *= 2
    src_t = (acum - jnp.log(dt)).T
    a_last = acum[L - 1:L, :]

    li = lax.broadcasted_iota(jnp.int32, (L, L), 0)
    si = lax.broadcasted_iota(jnp.int32, (L, L), 1)
    causal = li >= si
    lane = lax.broadcasted_iota(jnp.int32, (L, LANES), 1)
    lower = lane < M_HEAD_DIM
    heads_per_group = hw // M_HEAD_DIM

    for g in range(n_groups):
        cgb = cm_ref[:, g * gw:(g + 1) * gw].astype(BF16)
        bgb = bm_ref[:, g * gw:(g + 1) * gw].astype(BF16)
        cb = lax.dot_general(cgb, bgb, (((1,), (1,)), ((), ())), preferred_element_type=F32)
        prev = state_ref[:, g * hw:(g + 1) * hw]
        y_off = jnp.dot(cgb, prev.astype(BF16), preferred_element_type=F32)
        xd_parts = []
        for pr in range(heads_per_group // 2):
            h0 = g * heads_per_group + 2 * pr
            col0 = g * hw + pr * LANES
            xp = xs_ref[:, col0:col0 + LANES]
            ms = []
            for h in (h0, h0 + 1):
                seg = acum[:, h:h + 1] - src_t[h:h + 1, :]
                ms.append((cb * jnp.exp(jnp.where(causal, seg, -jnp.inf))).astype(BF16))
            lhs = jnp.concatenate(ms, axis=1)
            xpb = xp.astype(BF16)
            rhs = jnp.concatenate([jnp.where(lower, xpb, jnp.zeros_like(xpb)),
                                   jnp.where(lower, jnp.zeros_like(xpb), xpb)], axis=0)
            y_diag = jnp.dot(lhs, rhs, preferred_element_type=F32)
            al = jnp.where(lower, acum[:, h0:h0 + 1], acum[:, h0 + 1:h0 + 2])
            dtl = jnp.where(lower, dt[:, h0:h0 + 1], dt[:, h0 + 1:h0 + 2])
            alast = jnp.where(lower[0:1], a_last[:, h0:h0 + 1], a_last[:, h0 + 1:h0 + 2])
            o_ref[:, col0:col0 + LANES] = (y_diag + jnp.exp(al) * y_off[:, pr * LANES:(pr + 1) * LANES]
                                           + dsk_ref[:, col0:col0 + LANES] * xp)
            xd_parts.append((xp * (dtl * jnp.exp(alast - al))).astype(BF16))
            state_ref[:, col0:col0 + LANES] = state_ref[:, col0:col0 + LANES] * jnp.exp(alast)
        xd = jnp.concatenate(xd_parts, axis=1)
        state_ref[:, g * hw:(g + 1) * hw] += lax.dot_general(
            bgb, xd, (((0,), (0,)), ((), ())), preferred_element_type=F32)


def _ssd(zxbc, dt_raw, dt_bias, a_log, d_row, *, batch, seq, inner, n_groups):
    t = zxbc.shape[0]
    nchunks = seq // M_CHUNK
    gn = n_groups * M_STATE
    row = lambda b, c: (b * nchunks + c, 0)
    const = lambda b, c: (0, 0)
    return pl.pallas_call(
        functools.partial(_ssd_body, n_groups),
        grid=(batch, nchunks),
        in_specs=[
            pl.BlockSpec((M_CHUNK, inner), lambda b, c: (b * nchunks + c, 1)),
            pl.BlockSpec((M_CHUNK, gn), lambda b, c: (b * nchunks + c, 2 * inner // gn)),
            pl.BlockSpec((M_CHUNK, gn), lambda b, c: (b * nchunks + c, 2 * inner // gn + 1)),
            pl.BlockSpec((M_CHUNK, LANES), row),
            pl.BlockSpec((1, LANES), const),
            pl.BlockSpec((1, LANES), const),
            pl.BlockSpec((1, inner), const),
        ],
        out_specs=pl.BlockSpec((M_CHUNK, inner), row),
        out_shape=jax.ShapeDtypeStruct((t, inner), F32),
        scratch_shapes=[pltpu.VMEM((M_STATE, inner), F32)],
        compiler_params=_params("parallel", "arbitrary"),
        name="ssd_chunk",
    )(zxbc, zxbc, zxbc, dt_raw, dt_bias, a_log, d_row)


def _gated_outproj_body(group_w, x_ref, y_ref, z_ref, ng_ref, w_ref, o_ref):
    @pl.when(pl.program_id(1) == 0)
    def _():
        o_ref[...] = x_ref[...]

    yg = y_ref[...] * jax.nn.silu(z_ref[...])
    parts = []
    for c0 in range(0, yg.shape[-1], group_w):
        v = yg[:, c0:c0 + group_w]
        v = v * lax.rsqrt(jnp.mean(v * v, axis=-1, keepdims=True) + NORM_EPS)
        parts.append((v * ng_ref[:, c0:c0 + group_w]).astype(BF16))
    o_ref[...] += jnp.dot(jnp.concatenate(parts, axis=1), w_ref[...], preferred_element_type=F32)


def _gated_outproj(x, y, zxbc, norm_g, w, *, group_w, tm=512, tk=2048):
    t, d = x.shape
    inner = y.shape[-1]
    assert tk % group_w == 0
    return pl.pallas_call(
        functools.partial(_gated_outproj_body, group_w),
        grid=(t // tm, inner // tk),
        in_specs=[
            pl.BlockSpec((tm, d), lambda i, k: (i, 0)),
            pl.BlockSpec((tm, tk), lambda i, k: (i, k)),
            pl.BlockSpec((tm, tk), lambda i, k: (i, k)),
            pl.BlockSpec((1, tk), lambda i, k: (0, k)),
            pl.BlockSpec((tk, d), lambda i, k: (k, 0)),
        ],
        out_specs=pl.BlockSpec((tm, d), lambda i, k: (i, 0)),
        out_shape=jax.ShapeDtypeStruct((t, d), F32),
        compiler_params=_params("parallel", "arbitrary"),
        name="gated_norm_outproj",
    )(x, y, zxbc, norm_g, w)


def _dup_heads(w, n_heads):
    d = w.shape[0]
    w = w.reshape(d, n_heads, 1, HEAD_DIM)
    return jnp.broadcast_to(w, (d, n_heads, 2, HEAD_DIM)).reshape(d, n_heads * 2 * HEAD_DIM)


def _block_diag_tiles(blocks, n_tiles):
    g, r, c = blocks.shape
    gpt = g // n_tiles
    eye = jnp.eye(gpt, dtype=blocks.dtype)
    b = blocks.reshape(n_tiles, gpt, r, c)
    out = jnp.einsum('tgrc,gh->tgrhc', b, eye)
    return out.reshape(n_tiles, gpt * r, gpt * c)


def _rope_inv_freq_row():
    half = ROPE_DIM // 2
    inv_freq = jnp.exp(-math.log(ROPE_THETA) * jnp.arange(half, dtype=F32) * (2.0 / ROPE_DIM))
    lane = jnp.arange(LANES) % HEAD_DIM
    return jnp.where(lane < ROPE_DIM, inv_freq[lane % half], 0.0).reshape(1, LANES).astype(F32)


def _even_mixer(x, pos_b, norm_g, w_in, sinks, a_re, a_im, log_dt, b_re, b_im, c_re, c_im, d_skip, w_glu, b_glu,
                w_out, *, batch, seq):
    d = x.shape[-1]
    n_q = sinks.shape[0]
    a_width = n_q * HEAD_DIM
    groups, state = a_re.shape
    s5_width = groups * S5_GROUP
    kv_width = (w_in.shape[-1] - a_width - s5_width) // 2
    n_kv = kv_width // HEAD_DIM
    grp = n_q // n_kv
    assert grp % 2 == 0 and a_width % LANES == 0

    wq = w_in[:, :a_width]
    wk = _dup_heads(w_in[:, a_width:a_width + kv_width], n_kv)
    wv = _dup_heads(w_in[:, a_width + kv_width:a_width + 2 * kv_width], n_kv)
    wu = w_in[:, a_width + 2 * kv_width:]
    w_all = jnp.concatenate([wq, wk, wv, wu], axis=1).astype(BF16)
    q, kd, vd, u_cat = _even_inproj(x, norm_g, w_all, pos_b, _rope_inv_freq_row(), batch=batch, seq=seq,
                                    qw=a_width, kw=2 * kv_width, vw=2 * kv_width, uw=s5_width)

    sink_rows = jnp.repeat(sinks.astype(F32).reshape(n_q // 2, 2), ATTN_BLOCK, axis=1).reshape(n_q // 2, 2 * ATTN_BLOCK, 1)
    attn = _attention(q, kd, vd, sink_rows, batch=batch, seq=seq, pairs_per_kv=grp // 2)

    abr, abi, bbr, bbi = _s5_discretise(a_re.astype(F32), a_im.astype(F32), log_dt.astype(F32),
                                        b_re.astype(F32), b_im.astype(F32))
    n_tiles = groups // S5_GROUPS_PER_TILE
    to_in = lambda bb: _block_diag_tiles(jnp.swapaxes(bb, 1, 2), n_tiles)
    wb = jnp.concatenate([to_in(bbr), to_in(bbi)], axis=-1).astype(BF16)
    to_out = lambda cc: _block_diag_tiles(jnp.swapaxes(cc.astype(F32), 1, 2), n_tiles)
    wc = jnp.concatenate([to_out(c_re), -to_out(c_im)], axis=1).astype(BF16)
    a_re_t = abr.reshape(n_tiles, 1, -1)
    a_im_t = abi.reshape(n_tiles, 1, -1)

    seg_len = seq // S5_SEGMENTS
    ssm_cat = _s5(u_cat, wb, wc, a_re_t, a_im_t, d_skip.astype(F32).reshape(1, s5_width),
                  w_glu.astype(BF16), b_glu.astype(F32).reshape(1, s5_width))

    tm = seg_len
    nseg = S5_SEGMENTS
    w_o = w_out.astype(BF16)
    terms = [
        (attn, pl.BlockSpec((tm, a_width), lambda i: (i, 0)), w_o[:a_width]),
        (ssm_cat, pl.BlockSpec((None, tm, s5_width), lambda i: (i // nseg, 0, i % nseg)), w_o[a_width:]),
    ]
    return _outproj(x, terms, tm=tm)


def _odd_mixer(x, norm_g, w_in, conv_w, conv_b, dt_bias, a_log, d_skip, norm_out, w_out, *, batch, seq):
    inner = norm_out.shape[0]
    heads = a_log.shape[0]
    conv_dim = conv_w.shape[-1]
    n_groups = (conv_dim - inner) // (2 * M_STATE)
    assert heads <= LANES and inner // heads == M_HEAD_DIM
    pad = LANES - heads
    w_main = w_in[:, :inner + conv_dim].astype(BF16)
    w_dt = jnp.pad(w_in[:, inner + conv_dim:], ((0, 0), (0, pad))).astype(BF16)
    zxbc, dt_raw = _odd_inproj(x, norm_g, w_main, w_dt, conv_w.astype(F32), conv_b.astype(F32).reshape(1, conv_dim),
                               seq=seq, n_plain_cols=inner)
    row = lambda v: jnp.pad(v.astype(F32), (0, pad)).reshape(1, LANES)
    y = _ssd(zxbc, dt_raw, row(dt_bias), row(a_log), jnp.repeat(d_skip.astype(F32), M_HEAD_DIM).reshape(1, inner),
             batch=batch, seq=seq, inner=inner, n_groups=n_groups)
    return _gated_outproj(x, y, zxbc, norm_out.astype(F32).reshape(1, inner), w_out.astype(BF16),
                          group_w=inner // n_groups)


def kernel(x, positions, norm_ffn1, ffn1_gate, ffn1_up, ffn1_down, norm_mix, norm_ffn2, ffn2_gate, ffn2_up,
           ffn2_down, ev_w_in, ev_sinks, s5_a_re, s5_a_im, s5_log_dt, s5_b_re, s5_b_im, s5_c_re, s5_c_im, s5_d,
           s5_w_glu, s5_b_glu, ev_w_out, m_w_in, m_conv_w, m_conv_b, m_dt_bias, m_a_log, m_d, m_norm, m_w_out,
           final_norm):
    batch, seq, d = x.shape
    depth = norm_ffn1.shape[0]
    t = batch * seq
    h = x.reshape(t, d).astype(F32)
    pos_b = jnp.broadcast_to(positions.reshape(t, 1).astype(F32), (t, LANES))
    g1, g2, gm = (v.astype(F32).reshape(depth, 1, d) for v in (norm_ffn1, norm_ffn2, norm_mix))
    w1 = tuple(w.astype(BF16) for w in (ffn1_gate, ffn1_up, ffn1_down))
    w2 = tuple(w.astype(BF16) for w in (ffn2_gate, ffn2_up, ffn2_down))
    for layer in range(depth):
        h = _ffn(h, g1, *w1, layer)
        if layer % 2 == 0:
            e = layer // 2
            h = _even_mixer(h, pos_b, gm[layer], ev_w_in[e], ev_sinks[e], s5_a_re[e], s5_a_im[e], s5_log_dt[e],
                            s5_b_re[e], s5_b_im[e], s5_c_re[e], s5_c_im[e], s5_d[e].reshape(-1), s5_w_glu[e],
                            s5_b_glu[e], ev_w_out[e], batch=batch, seq=seq)
        else:
            o = layer // 2
            h = _odd_mixer(h, gm[layer], m_w_in[o], m_conv_w[o], m_conv_b[o], m_dt_bias[o], m_a_log[o], m_d[o],
                           m_norm[o], m_w_out[o], batch=batch, seq=seq)
        h = _ffn(h, g2, *w2, layer, final_norm.astype(F32).reshape(1, d) if layer == depth - 1 else None)
    return h.reshape(batch, seq, d).astype(x.dtype)
```

```python
import functools
import math

import jax
import jax.numpy as jnp
from jax import lax
from jax.experimental import pallas as pl
from jax.experimental.pallas import tpu as pltpu

F32 = jnp.float32
BF16 = jnp.bfloat16

NORM_EPS = 1e-5
NEG_INF = -1e30
LANES = 128
SUBLANES = 8
VMEM_LIMIT_BYTES = 56 * 1024 * 1024

HEAD_DIM = 64
ATTN_BLOCK = 128
ROPE_THETA = 500000.0
ROPE_DIM = HEAD_DIM // 4
ATTN_BLOCKS_PER_STEP = 4
S5_GROUP = 16
S5_STATE = 64
S5_SEGMENTS = SUBLANES
S5_GROUPS_PER_TILE = 16
M_HEAD_DIM = 64
M_STATE = 128
M_CHUNK = 128
M_CONV = 4


def _params(*sem):
    return pltpu.CompilerParams(dimension_semantics=sem, vmem_limit_bytes=VMEM_LIMIT_BYTES)


def _rms(x, g):
    return x * lax.rsqrt(jnp.mean(x * x, axis=-1, keepdims=True) + NORM_EPS) * g


def _ffn_body(final, x_ref, g_ref, wg_ref, wu_ref, wd_ref, *rest):
    o_ref, xn_ref = rest[-2:]
    j = pl.program_id(1)

    @pl.when(j == 0)
    def _():
        x = x_ref[...]
        xn_ref[...] = _rms(x, g_ref[...]).astype(BF16)
        o_ref[...] = x

    xn = xn_ref[...]
    gate = jnp.dot(xn, wg_ref[...], preferred_element_type=F32)
    up = jnp.dot(xn, wu_ref[...], preferred_element_type=F32)
    h = (jax.nn.silu(gate) * up * 0.5).astype(BF16)
    o_ref[...] += jnp.dot(h, wd_ref[...], preferred_element_type=F32)

    if final:
        @pl.when(j == pl.num_programs(1) - 1)
        def _():
            o_ref[...] = _rms(o_ref[...], rest[0][...])


def _ffn(x, g, wg, wu, wd, layer, final_g=None, *, tm=1024, tf=512):
    t, d = x.shape
    dff = wg.shape[-1]
    in_specs = [
        pl.BlockSpec((tm, d), lambda i, j: (i, 0)),
        pl.BlockSpec((None, 1, d), lambda i, j: (layer, 0, 0)),
        pl.BlockSpec((None, d, tf), lambda i, j: (layer, 0, j)),
        pl.BlockSpec((None, d, tf), lambda i, j: (layer, 0, j)),
        pl.BlockSpec((None, tf, d), lambda i, j: (layer, j, 0)),
    ]
    args = [x, g, wg, wu, wd]
    if final_g is not None:
        in_specs.append(pl.BlockSpec((1, d), lambda i, j: (0, 0)))
        args.append(final_g)
    return pl.pallas_call(
        functools.partial(_ffn_body, final_g is not None),
        grid=(t // tm, dff // tf),
        in_specs=in_specs,
        out_specs=pl.BlockSpec((tm, d), lambda i, j: (i, 0)),
        out_shape=jax.ShapeDtypeStruct((t, d), F32),
        scratch_shapes=[pltpu.VMEM((tm, d), BF16)],
        compiler_params=_params("parallel", "arbitrary"),
        name="swiglu_ffn",
    )(*args)


def _outproj_body(n_terms, x_ref, *refs):
    o_ref = refs[-1]
    acc = x_ref[...]
    for i in range(n_terms):
        acc = acc + jnp.dot(refs[2 * i][...], refs[2 * i + 1][...], preferred_element_type=F32)
    o_ref[...] = acc


def _outproj(x, terms, *, tm):
    t, d = x.shape
    in_specs = [pl.BlockSpec((tm, d), lambda i: (i, 0))]
    args = [x]
    for a, a_spec, w in terms:
        in_specs += [a_spec, pl.BlockSpec(w.shape, lambda i: (0, 0))]
        args += [a, w]
    return pl.pallas_call(
        functools.partial(_outproj_body, len(terms)),
        grid=(t // tm,),
        in_specs=in_specs,
        out_specs=pl.BlockSpec((tm, d), lambda i: (i, 0)),
        out_shape=jax.ShapeDtypeStruct((t, d), F32),
        compiler_params=_params("parallel"),
        name="residual_outproj",
    )(*args)


def _even_inproj_body(n_rot, x_ref, g_ref, w_ref, pos_ref, invf_ref, q_ref, k_ref, v_ref, u_ref, xn_ref):
    xn_ref[...] = _rms(x_ref[...], g_ref[...]).astype(BF16)
    ang = pos_ref[...] * invf_ref[...]
    cos, sin = jnp.cos(ang), jnp.sin(ang)
    lane = lax.broadcasted_iota(jnp.int32, ang.shape, 1) % HEAD_DIM
    half = ROPE_DIM // 2
    sin_from_lower = jnp.where((lane >= half) & (lane < ROPE_DIM), sin, 0.0)
    sin_from_upper = jnp.where(lane < half, sin, 0.0)

    cw = 512
    qw, kw, vw = q_ref.shape[-1], k_ref.shape[-1], v_ref.shape[-1]
    for c0 in range(0, w_ref.shape[-1], cw):
        p = jnp.dot(xn_ref[...], w_ref[:, c0:c0 + cw], preferred_element_type=F32)
        for l0 in range(0, cw, LANES):
            col = c0 + l0
            t = p[:, l0:l0 + LANES]
            if col < n_rot:
                t = (t * cos + pltpu.roll(t, half, axis=1) * sin_from_lower
                     - pltpu.roll(t, LANES - half, axis=1) * sin_from_upper)
            if col < qw:
                q_ref[:, col:col + LANES] = t.astype(BF16)
            elif col < qw + kw:
                k_ref[:, col - qw:col - qw + LANES] = t.astype(BF16)
            elif col < qw + kw + vw:
                v_ref[:, col - qw - kw:col - qw - kw + LANES] = t.astype(BF16)
            else:
                c = col - qw - kw - vw
                u_ref[:, c:c + LANES] = t


def _even_inproj(x, g, w, pos_b, invf, *, batch, seq, qw, kw, vw, uw):
    t, d = x.shape
    tm = seq // S5_SEGMENTS
    nseg = S5_SEGMENTS
    return pl.pallas_call(
        functools.partial(_even_inproj_body, qw + kw),
        grid=(t // tm,),
        in_specs=[
            pl.BlockSpec((tm, d), lambda i: (i, 0)),
            pl.BlockSpec((1, d), lambda i: (0, 0)),
            pl.BlockSpec(w.shape, lambda i: (0, 0)),
            pl.BlockSpec((tm, LANES), lambda i: (i, 0)),
            pl.BlockSpec((1, LANES), lambda i: (0, 0)),
        ],
        out_specs=[
            pl.BlockSpec((tm, qw), lambda i: (i, 0)),
            pl.BlockSpec((tm, kw), lambda i: (i, 0)),
            pl.BlockSpec((tm, vw), lambda i: (i, 0)),
            pl.BlockSpec((None, tm, uw), lambda i: (i // nseg, 0, i % nseg)),
        ],
        out_shape=[
            jax.ShapeDtypeStruct((t, qw), BF16),
            jax.ShapeDtypeStruct((t, kw), BF16),
            jax.ShapeDtypeStruct((t, vw), BF16),
            jax.ShapeDtypeStruct((batch, tm, nseg * uw), F32),
        ],
        scratch_shapes=[pltpu.VMEM((tm, d), BF16)],
        compiler_params=_params("parallel"),
        name="even_inproj_rotary",
    )(x, g, w, pos_b, invf)


def _attn_body(pairs_per_kv, q_ref, kc_ref, kp_ref, vc_ref, vp_ref, sink_ref, o_ref):
    m_step = pl.program_id(1)
    blk = ATTN_BLOCK
    kk = jnp.concatenate([kp_ref[...], kc_ref[...]], axis=0)
    vv = jnp.concatenate([vp_ref[...], vc_ref[...]], axis=0)
    lower = lax.broadcasted_iota(jnp.int32, (blk, LANES), 1) < HEAD_DIM
    key = lax.broadcasted_iota(jnp.int32, (2 * blk, 2 * blk), 0)
    qrow = lax.broadcasted_iota(jnp.int32, (2 * blk, 2 * blk), 1) % blk
    band = (key > qrow) & (key <= qrow + blk)
    upper_rows = lax.broadcasted_iota(jnp.int32, (LANES, blk), 0) >= HEAD_DIM
    n_pairs = q_ref.shape[-1] // LANES
    scale = 1.0 / math.sqrt(HEAD_DIM)

    def scores(t, c):
        hk = c // pairs_per_kv
        qc = q_ref[t * blk:(t + 1) * blk, c * LANES:(c + 1) * LANES] * scale
        q2 = jnp.concatenate([jnp.where(lower, qc, jnp.zeros_like(qc)),
                              jnp.where(lower, jnp.zeros_like(qc), qc)], axis=0)
        kh = kk[t * blk:t * blk + 2 * blk, hk * LANES:(hk + 1) * LANES]
        return lax.dot_general(kh, q2, (((1,), (1,)), ((), ())), preferred_element_type=F32)

    def finish(t, c, s):
        hk = c // pairs_per_kv
        first = (m_step == 0) if t == 0 else False
        valid = band & ((key >= blk) | jnp.logical_not(first))
        s = jnp.where(valid, s, NEG_INF)
        sink = sink_ref[c]
        mx = jnp.maximum(jnp.max(s, axis=0, keepdims=True), sink)
        p = jnp.exp(s - mx)
        den = jnp.sum(p, axis=0, keepdims=True) + jnp.exp(sink - mx)
        p = (p * (1.0 / den)).astype(BF16)
        vh = vv[t * blk:t * blk + 2 * blk, hk * LANES:(hk + 1) * LANES]
        r = lax.dot_general(vh, p, (((0,), (0,)), ((), ())), preferred_element_type=F32)
        o_t = jnp.where(upper_rows, r[:, blk:], r[:, :blk])
        o_ref[t * blk:(t + 1) * blk, c * LANES:(c + 1) * LANES] = o_t.T.astype(BF16)

    units = [(t, c) for t in range(ATTN_BLOCKS_PER_STEP) for c in range(n_pairs)]
    ahead = 4
    pending = [scores(*u) for u in units[:ahead]]
    for n, (t, c) in enumerate(units):
        if n + ahead < len(units):
            pending.append(scores(*units[n + ahead]))
        finish(t, c, pending.pop(0))


def _attention(q, kd, vd, sink_rows, *, batch, seq, pairs_per_kv):
    t, qw = q.shape
    kw = kd.shape[-1]
    rows = ATTN_BLOCK * ATTN_BLOCKS_PER_STEP
    steps = seq // rows
    bps = ATTN_BLOCKS_PER_STEP
    nblk = seq // ATTN_BLOCK

    def cur(b, m):
        return (b * steps + m, 0)

    def prev(b, m):
        return (b * nblk + jnp.maximum(m * bps - 1, 0), 0)

    return pl.pallas_call(
        functools.partial(_attn_body, pairs_per_kv),
        grid=(batch, steps),
        in_specs=[
            pl.BlockSpec((rows, qw), cur),
            pl.BlockSpec((rows, kw), cur),
            pl.BlockSpec((ATTN_BLOCK, kw), prev),
            pl.BlockSpec((rows, kw), cur),
            pl.BlockSpec((ATTN_BLOCK, kw), prev),
            pl.BlockSpec(sink_rows.shape, lambda b, m: (0, 0, 0)),
        ],
        out_specs=pl.BlockSpec((rows, qw), cur),
        out_shape=jax.ShapeDtypeStruct((t, qw), BF16),
        compiler_params=_params("parallel", "arbitrary"),
        name="swa_attention",
    )(q, kd, kd, vd, vd, sink_rows)


def _s5_discretise_body(are_ref, aim_ref, logdt_ref, bre_ref, bim_ref, abr_ref, abi_ref, bbr_ref, bbi_ref):
    are, aim = are_ref[...], aim_ref[...]
    dt = jnp.exp(logdt_ref[...])
    mag = jnp.exp(are * dt)
    abar_re, abar_im = mag * jnp.cos(aim * dt), mag * jnp.sin(aim * dt)
    nr, ni = abar_re - 1.0, abar_im
    den = are * are + aim * aim
    coef_re = (nr * are + ni * aim) / den
    coef_im = (ni * are - nr * aim) / den
    bre, bim = bre_ref[...], bim_ref[...]
    abr_ref[...] = abar_re
    abi_ref[...] = abar_im
    bbr_ref[...] = coef_re * bre - coef_im * bim
    bbi_ref[...] = coef_re * bim + coef_im * bre


def _s5_discretise(a_re, a_im, log_dt, b_re, b_im):
    g, p = a_re.shape
    c = b_re.shape[-1]
    col = lambda v: v.reshape(g * p, 1)
    logdt_col = jnp.broadcast_to(log_dt[:, None], (g, p)).reshape(g * p, 1)
    outs = pl.pallas_call(
        _s5_discretise_body,
        out_shape=[jax.ShapeDtypeStruct((g * p, 1), F32)] * 2 + [jax.ShapeDtypeStruct((g * p, c), F32)] * 2,
        name="s5_discretise",
    )(col(a_re), col(a_im), logdt_col, b_re.reshape(g * p, c), b_im.reshape(g * p, c))
    abr, abi, bbr, bbi = outs
    return abr.reshape(g, p), abi.reshape(g, p), bbr.reshape(g, p, c), bbi.reshape(g, p, c)


def _cmul(ar, ai, br, bi):
    return ar * br - ai * bi, ar * bi + ai * br


def _s5_body(n_tiles, steps_per_block, u_ref, wb_ref, wc_ref, are_ref, aim_ref, d_ref, wglu_ref, bglu_ref,
             o_ref, us_ref, ys_ref, state_ref):
    phase = pl.program_id(1)
    blk = pl.program_id(2)
    nblk = pl.num_programs(2)
    sw = are_ref.shape[-1]
    ncol = sw // LANES
    cw = wc_ref.shape[-1]
    nslab = us_ref.shape[0]
    nseg = S5_SEGMENTS

    @pl.when((phase == 0) & (blk == 0))
    def _():
        state_ref[...] = jnp.zeros_like(state_ref)

    for s in range(nseg):
        for c in range(nslab):
            us_ref[c, pl.ds(s, steps_per_block, stride=nseg), :] = u_ref[:, (s * nslab + c) * LANES:
                                                                         (s * nslab + c + 1) * LANES]

    def u_cols(c0, n):
        return jnp.concatenate([us_ref[c0 + i] for i in range(n)], axis=1)

    def input_drive(kt):
        ub = u_cols(kt * (cw // LANES), cw // LANES).astype(BF16)
        return jnp.dot(ub, wb_ref[kt], preferred_element_type=F32)

    def scan_tile(kt, bu, store):
        ar = [jnp.broadcast_to(are_ref[kt, :, c * LANES:(c + 1) * LANES], (SUBLANES, LANES)) for c in range(ncol)]
        ai = [jnp.broadcast_to(aim_ref[kt, :, c * LANES:(c + 1) * LANES], (SUBLANES, LANES)) for c in range(ncol)]
        h = [state_ref[kt, :, c * LANES:(c + 1) * LANES] for c in range(2 * ncol)]
        rows_out = []
        for j in range(steps_per_block):
            r0 = j * SUBLANES
            hr_new, hi_new = [], []
            for c in range(ncol):
                pr, pi = _cmul(ar[c], ai[c], h[c], h[ncol + c])
                hr_new.append(pr + bu[r0:r0 + SUBLANES, c * LANES:(c + 1) * LANES])
                hi_new.append(pi + bu[r0:r0 + SUBLANES, sw + c * LANES:sw + (c + 1) * LANES])
            h = hr_new + hi_new
            if store:
                rows_out.append(jnp.concatenate(h, axis=1))
        for c in range(2 * ncol):
            state_ref[kt, :, c * LANES:(c + 1) * LANES] = h[c]
        return jnp.concatenate(rows_out, axis=0) if store else None

    @pl.when(phase == 0)
    def _():
        bus = [input_drive(kt) for kt in range(n_tiles)]
        for kt in range(n_tiles):
            scan_tile(kt, bus[kt], False)

    @pl.when((phase == 0) & (blk == nblk - 1))
    def _():
        seg_len = steps_per_block * nblk
        for kt in range(n_tiles):
            pr, pi = are_ref[kt], aim_ref[kt]
            k = 1
            while k < seg_len:
                pr, pi = _cmul(pr, pi, pr, pi)
                k *= 2
            fr, fi = state_ref[kt, :, :sw], state_ref[kt, :, sw:]
            hr, hi = jnp.zeros_like(fr), jnp.zeros_like(fi)
            srow = lax.broadcasted_iota(jnp.int32, fr.shape, 0)
            for s in range(1, S5_SEGMENTS):
                qr, qi = _cmul(pr, pi, hr, hi)
                nr = pltpu.roll(fr + qr, 1, axis=0)
                ni = pltpu.roll(fi + qi, 1, axis=0)
                hr = jnp.where(srow == s, nr, hr)
                hi = jnp.where(srow == s, ni, hi)
            state_ref[kt, :, :sw] = hr
            state_ref[kt, :, sw:] = hi

    @pl.when(phase == 1)
    def _():
        bus = [input_drive(kt) for kt in range(n_tiles)]
        ys = [jnp.dot(scan_tile(kt, bus[kt], True).astype(BF16), wc_ref[kt], preferred_element_type=F32)
              for kt in range(n_tiles)]
        y = jnp.concatenate(ys, axis=1) + d_ref[...] * u_cols(0, nslab)
        gl = jax.nn.gelu(y)
        gate = jnp.dot(gl.astype(BF16), wglu_ref[...], preferred_element_type=F32) + bglu_ref[...]
        out = gl * jax.nn.sigmoid(gate)
        for c in range(nslab):
            ys_ref[c] = out[:, c * LANES:(c + 1) * LANES]
        for s in range(nseg):
            for c in range(nslab):
                o_ref[:, (s * nslab + c) * LANES:(s * nslab + c + 1) * LANES] = (
                    ys_ref[c, pl.ds(s, steps_per_block, stride=nseg), :].astype(BF16))


def _s5(u_cat, wb, wc, a_re_t, a_im_t, d_row, w_glu, b_glu, *, steps_per_block=32):
    batch, seg_len, cat_width = u_cat.shape
    width = cat_width // S5_SEGMENTS
    n_tiles, cw, sw2 = wb.shape
    nblk = seg_len // steps_per_block
    steps = steps_per_block
    rows_per_block = steps * S5_SEGMENTS
    assert seg_len & (seg_len - 1) == 0, "segment length must be a power of two (abar ** seg_len by squaring)"
    const3 = lambda b, p, k: (0, 0, 0)
    const2 = lambda b, p, k: (0, 0)
    return pl.pallas_call(
        functools.partial(_s5_body, n_tiles, steps),
        grid=(batch, 2, nblk),
        in_specs=[
            pl.BlockSpec((None, steps, cat_width), lambda b, p, k: (b, k, 0)),
            pl.BlockSpec(wb.shape, const3),
            pl.BlockSpec(wc.shape, const3),
            pl.BlockSpec(a_re_t.shape, const3),
            pl.BlockSpec(a_im_t.shape, const3),
            pl.BlockSpec(d_row.shape, const2),
            pl.BlockSpec(w_glu.shape, const2),
            pl.BlockSpec(b_glu.shape, const2),
        ],
        out_specs=pl.BlockSpec((None, steps, cat_width), lambda b, p, k: (b, k * p, 0)),
        out_shape=jax.ShapeDtypeStruct((batch, seg_len, cat_width), BF16),
        scratch_shapes=[
            pltpu.VMEM((width // LANES, rows_per_block, LANES), F32),
            pltpu.VMEM((width // LANES, rows_per_block, LANES), F32),
            pltpu.VMEM((n_tiles, SUBLANES, sw2), F32),
        ],
        compiler_params=_params("parallel", "arbitrary", "arbitrary"),
        name="s5_scan_glu",
    )(u_cat, wb, wc, a_re_t, a_im_t, d_row, w_glu, b_glu)


HALO_ROWS = 16


def _odd_inproj_body(n_plain, tiles_per_seq, x_ref, xh_ref, g_ref, w_ref, wdt_ref, cw_ref, cb_ref,
                     o_ref, dt_ref, xn_ref):
    i, j = pl.program_id(0), pl.program_id(1)

    @pl.when(j == 0)
    def _():
        g = g_ref[...]
        xn_ref[0:HALO_ROWS, :] = _rms(xh_ref[...], g).astype(BF16)
        xn_ref[HALO_ROWS:, :] = _rms(x_ref[...], g).astype(BF16)
        dt_ref[...] = jnp.dot(xn_ref[HALO_ROWS:, :], wdt_ref[...], preferred_element_type=F32)

    @pl.when(j < n_plain)
    def _():
        o_ref[...] = jnp.dot(xn_ref[HALO_ROWS:, :], w_ref[...], preferred_element_type=F32)

    @pl.when(j >= n_plain)
    def _():
        p = jnp.dot(xn_ref[...], w_ref[...], preferred_element_type=F32)
        rowi = lax.broadcasted_iota(jnp.int32, p.shape, 0)
        seq_start = (i % tiles_per_seq) == 0
        ext = jnp.where((rowi >= HALO_ROWS) | jnp.logical_not(seq_start), p, 0.0)
        acc = cb_ref[...] + cw_ref[M_CONV - 1:M_CONV, :] * p[HALO_ROWS:]
        for k in range(1, M_CONV):
            acc = acc + cw_ref[M_CONV - 1 - k:M_CONV - k, :] * ext[HALO_ROWS - k:ext.shape[0] - k]
        o_ref[...] = jax.nn.silu(acc)


def _odd_inproj(x, g, w, wdt, conv_w, conv_b, *, seq, n_plain_cols, tm=1024, tn=1024):
    t, d = x.shape
    n = w.shape[-1]
    assert seq % tm == 0 and n_plain_cols % tn == 0
    n_plain = n_plain_cols // tn
    halo_blocks = tm // HALO_ROWS
    conv_col = lambda i, j: (0, jnp.maximum(j - n_plain, 0))
    return pl.pallas_call(
        functools.partial(_odd_inproj_body, n_plain, seq // tm),
        grid=(t // tm, n // tn),
        in_specs=[
            pl.BlockSpec((tm, d), lambda i, j: (i, 0)),
            pl.BlockSpec((HALO_ROWS, d), lambda i, j: (jnp.maximum(i * halo_blocks - 1, 0), 0)),
            pl.BlockSpec((1, d), lambda i, j: (0, 0)),
            pl.BlockSpec((d, tn), lambda i, j: (0, j)),
            pl.BlockSpec(wdt.shape, lambda i, j: (0, 0)),
            pl.BlockSpec((M_CONV, tn), conv_col),
            pl.BlockSpec((1, tn), conv_col),
        ],
        out_specs=[
            pl.BlockSpec((tm, tn), lambda i, j: (i, j)),
            pl.BlockSpec((tm, wdt.shape[-1]), lambda i, j: (i, 0)),
        ],
        out_shape=[jax.ShapeDtypeStruct((t, n), F32), jax.ShapeDtypeStruct((t, wdt.shape[-1]), F32)],
        scratch_shapes=[pltpu.VMEM((HALO_ROWS + tm, d), BF16)],
        compiler_params=_params("parallel", "arbitrary"),
        name="odd_inproj_conv",
    )(x, x, g, w, wdt, conv_w, conv_b)


def _ssd_body(n_groups, z_ref, xs_ref, bm_ref, cm_ref, dt_ref, dtb_ref, alog_ref, dsk_ref, ng_ref,
              o_ref, state_ref, y_ref):
    c = pl.program_id(1)
    L = M_CHUNK
    inner = xs_ref.shape[-1]
    gw = bm_ref.shape[-1] // n_groups
    hw = inner // n_groups

    @pl.when(c == 0)
    def _():
        state_ref[...] = jnp.zeros_like(state_ref)

    dt = jax.nn.softplus(dt_ref[...] + dtb_ref[...])
    a = -jnp.exp(alog_ref[...])
    acum = a * dt
    rowi = lax.broadcasted_iota(jnp.int32, acum.shape, 0)
    k = 1
    while k < L:
        acum = acum + jnp.where(rowi >= k, pltpu.roll(acum, k, axis=0), 0.0)
        k *= 2
    src_t = (acum - jnp.log(dt)).T
    a_last = acum[L - 1:L, :]

    li = lax.broadcasted_iota(jnp.int32, (L, L), 0)
    si = lax.broadcasted_iota(jnp.int32, (L, L), 1)
    causal = li >= si
    lane = lax.broadcasted_iota(jnp.int32, (L, LANES), 1)
    lower = lane < M_HEAD_DIM
    heads_per_group = hw // M_HEAD_DIM

    for g in range(n_groups):
        cgb = cm_ref[:, g * gw:(g + 1) * gw].astype(BF16)
        bgb = bm_ref[:, g * gw:(g + 1) * gw].astype(BF16)
        cb = lax.dot_general(cgb, bgb, (((1,), (1,)), ((), ())), preferred_element_type=F32)
        prev = state_ref[:, g * hw:(g + 1) * hw]
        y_off = jnp.dot(cgb, prev.astype(BF16), preferred_element_type=F32)
        xd_parts = []
        for pr in range(heads_per_group // 2):
            h0 = g * heads_per_group + 2 * pr
            col0 = g * hw + pr * LANES
            xp = xs_ref[:, col0:col0 + LANES]
            ms = []
            for h in (h0, h0 + 1):
                seg = acum[:, h:h + 1] - src_t[h:h + 1, :]
                ms.append((cb * jnp.exp(jnp.where(causal, seg, -jnp.inf))).astype(BF16))
            lhs = jnp.concatenate(ms, axis=1)
            xpb = xp.astype(BF16)
            rhs = jnp.concatenate([jnp.where(lower, xpb, jnp.zeros_like(xpb)),
                                   jnp.where(lower, jnp.zeros_like(xpb), xpb)], axis=0)
            y_diag = jnp.dot(lhs, rhs, preferred_element_type=F32)
            al = jnp.where(lower, acum[:, h0:h0 + 1], acum[:, h0 + 1:h0 + 2])
            dtl = jnp.where(lower, dt[:, h0:h0 + 1], dt[:, h0 + 1:h0 + 2])
            alast = jnp.where(lower[0:1], a_last[:, h0:h0 + 1], a_last[:, h0 + 1:h0 + 2])
            y_ref[:, col0:col0 + LANES] = (y_diag + jnp.exp(al) * y_off[:, pr * LANES:(pr + 1) * LANES]
                                           + dsk_ref[:, col0:col0 + LANES] * xp)
            xd_parts.append((xp * (dtl * jnp.exp(alast - al))).astype(BF16))
            state_ref[:, col0:col0 + LANES] = state_ref[:, col0:col0 + LANES] * jnp.exp(alast)
        xd = jnp.concatenate(xd_parts, axis=1)
        state_ref[:, g * hw:(g + 1) * hw] += lax.dot_general(
            bgb, xd, (((0,), (0,)), ((), ())), preferred_element_type=F32)

    for g in range(n_groups):
        sl = slice(g * hw, (g + 1) * hw)
        y = y_ref[:, sl] * jax.nn.silu(z_ref[:, sl])
        y = y * lax.rsqrt(jnp.mean(y * y, axis=-1, keepdims=True) + NORM_EPS)
        o_ref[:, sl] = (y * ng_ref[:, sl]).astype(BF16)


def _ssd(zxbc, dt_raw, dt_bias, a_log, d_row, norm_g, *, batch, seq, inner, n_groups):
    t = zxbc.shape[0]
    nchunks = seq // M_CHUNK
    gn = n_groups * M_STATE
    row = lambda b, c: (b * nchunks + c, 0)
    const = lambda b, c: (0, 0)
    return pl.pallas_call(
        functools.partial(_ssd_body, n_groups),
        grid=(batch, nchunks),
        in_specs=[
            pl.BlockSpec((M_CHUNK, inner), row),
            pl.BlockSpec((M_CHUNK, inner), lambda b, c: (b * nchunks + c, 1)),
            pl.BlockSpec((M_CHUNK, gn), lambda b, c: (b * nchunks + c, 2 * inner // gn)),
            pl.BlockSpec((M_CHUNK, gn), lambda b, c: (b * nchunks + c, 2 * inner // gn + 1)),
            pl.BlockSpec((M_CHUNK, LANES), row),
            pl.BlockSpec((1, LANES), const),
            pl.BlockSpec((1, LANES), const),
            pl.BlockSpec((1, inner), const),
            pl.BlockSpec((1, inner), const),
        ],
        out_specs=pl.BlockSpec((M_CHUNK, inner), row),
        out_shape=jax.ShapeDtypeStruct((t, inner), BF16),
        scratch_shapes=[pltpu.VMEM((M_STATE, inner), F32),
                        pltpu.VMEM((M_CHUNK, inner), F32)],
        compiler_params=_params("parallel", "arbitrary"),
        name="ssd_chunk",
    )(zxbc, zxbc, zxbc, zxbc, dt_raw, dt_bias, a_log, d_row, norm_g)


def _dup_heads(w, n_heads):
    d = w.shape[0]
    w = w.reshape(d, n_heads, 1, HEAD_DIM)
    return jnp.broadcast_to(w, (d, n_heads, 2, HEAD_DIM)).reshape(d, n_heads * 2 * HEAD_DIM)


def _block_diag_tiles(blocks, n_tiles):
    g, r, c = blocks.shape
    gpt = g // n_tiles
    eye = jnp.eye(gpt, dtype=blocks.dtype)
    b = blocks.reshape(n_tiles, gpt, r, c)
    out = jnp.einsum('tgrc,gh->tgrhc', b, eye)
    return out.reshape(n_tiles, gpt * r, gpt * c)


def _rope_inv_freq_row():
    half = ROPE_DIM // 2
    inv_freq = jnp.exp(-math.log(ROPE_THETA) * jnp.arange(half, dtype=F32) * (2.0 / ROPE_DIM))
    lane = jnp.arange(LANES) % HEAD_DIM
    return jnp.where(lane < ROPE_DIM, inv_freq[lane % half], 0.0).reshape(1, LANES).astype(F32)


def _even_mixer(x, pos_b, norm_g, w_in, sinks, a_re, a_im, log_dt, b_re, b_im, c_re, c_im, d_skip, w_glu, b_glu,
                w_out, *, batch, seq):
    d = x.shape[-1]
    n_q = sinks.shape[0]
    a_width = n_q * HEAD_DIM
    groups, state = a_re.shape
    s5_width = groups * S5_GROUP
    kv_width = (w_in.shape[-1] - a_width - s5_width) // 2
    n_kv = kv_width // HEAD_DIM
    grp = n_q // n_kv
    assert grp % 2 == 0 and a_width % LANES == 0

    wq = w_in[:, :a_width]
    wk = _dup_heads(w_in[:, a_width:a_width + kv_width], n_kv)
    wv = _dup_heads(w_in[:, a_width + kv_width:a_width + 2 * kv_width], n_kv)
    wu = w_in[:, a_width + 2 * kv_width:]
    w_all = jnp.concatenate([wq, wk, wv, wu], axis=1).astype(BF16)
    q, kd, vd, u_cat = _even_inproj(x, norm_g, w_all, pos_b, _rope_inv_freq_row(), batch=batch, seq=seq,
                                    qw=a_width, kw=2 * kv_width, vw=2 * kv_width, uw=s5_width)

    sink_rows = jnp.repeat(sinks.astype(F32).reshape(n_q // 2, 2), ATTN_BLOCK, axis=1).reshape(n_q // 2, 1, 2 * ATTN_BLOCK)
    attn = _attention(q, kd, vd, sink_rows, batch=batch, seq=seq, pairs_per_kv=grp // 2)

    abr, abi, bbr, bbi = _s5_discretise(a_re.astype(F32), a_im.astype(F32), log_dt.astype(F32),
                                        b_re.astype(F32), b_im.astype(F32))
    n_tiles = groups // S5_GROUPS_PER_TILE
    to_in = lambda bb: _block_diag_tiles(jnp.swapaxes(bb, 1, 2), n_tiles)
    wb = jnp.concatenate([to_in(bbr), to_in(bbi)], axis=-1).astype(BF16)
    to_out = lambda cc: _block_diag_tiles(jnp.swapaxes(cc.astype(F32), 1, 2), n_tiles)
    wc = jnp.concatenate([to_out(c_re), -to_out(c_im)], axis=1).astype(BF16)
    a_re_t = abr.reshape(n_tiles, 1, -1)
    a_im_t = abi.reshape(n_tiles, 1, -1)

    seg_len = seq // S5_SEGMENTS
    ssm_cat = _s5(u_cat, wb, wc, a_re_t, a_im_t, d_skip.astype(F32).reshape(1, s5_width),
                  w_glu.astype(BF16), b_glu.astype(F32).reshape(1, s5_width))

    tm = seg_len
    nseg = S5_SEGMENTS
    w_o = w_out.astype(BF16)
    terms = [
        (attn, pl.BlockSpec((tm, a_width), lambda i: (i, 0)), w_o[:a_width]),
        (ssm_cat, pl.BlockSpec((None, tm, s5_width), lambda i: (i // nseg, 0, i % nseg)), w_o[a_width:]),
    ]
    return _outproj(x, terms, tm=tm)


def _odd_mixer(x, norm_g, w_in, conv_w, conv_b, dt_bias, a_log, d_skip, norm_out, w_out, *, batch, seq):
    inner = norm_out.shape[0]
    heads = a_log.shape[0]
    conv_dim = conv_w.shape[-1]
    n_groups = (conv_dim - inner) // (2 * M_STATE)
    assert heads <= LANES and inner // heads == M_HEAD_DIM
    pad = LANES - heads
    w_main = w_in[:, :inner + conv_dim].astype(BF16)
    w_dt = jnp.pad(w_in[:, inner + conv_dim:], ((0, 0), (0, pad))).astype(BF16)
    zxbc, dt_raw = _odd_inproj(x, norm_g, w_main, w_dt, conv_w.astype(F32), conv_b.astype(F32).reshape(1, conv_dim),
                               seq=seq, n_plain_cols=inner)
    row = lambda v: jnp.pad(v.astype(F32), (0, pad)).reshape(1, LANES)
    y = _ssd(zxbc, dt_raw, row(dt_bias), row(a_log), jnp.repeat(d_skip.astype(F32), M_HEAD_DIM).reshape(1, inner),
             norm_out.astype(F32).reshape(1, inner), batch=batch, seq=seq, inner=inner, n_groups=n_groups)
    tm = 512
    return _outproj(x, [(y, pl.BlockSpec((tm, inner), lambda i: (i, 0)), w_out.astype(BF16))], tm=tm)


def kernel(x, positions, norm_ffn1, ffn1_gate, ffn1_up, ffn1_down, norm_mix, norm_ffn2, ffn2_gate, ffn2_up,
           ffn2_down, ev_w_in, ev_sinks, s5_a_re, s5_a_im, s5_log_dt, s5_b_re, s5_b_im, s5_c_re, s5_c_im, s5_d,
           s5_w_glu, s5_b_glu, ev_w_out, m_w_in, m_conv_w, m_conv_b, m_dt_bias, m_a_log, m_d, m_norm, m_w_out,
           final_norm):
    batch, seq, d = x.shape
    depth = norm_ffn1.shape[0]
    t = batch * seq
    h = x.reshape(t, d).astype(F32)
    pos_b = jnp.broadcast_to(positions.reshape(t, 1).astype(F32), (t, LANES))
    g1, g2, gm = (v.astype(F32).reshape(depth, 1, d) for v in (norm_ffn1, norm_ffn2, norm_mix))
    w1 = tuple(w.astype(BF16) for w in (ffn1_gate, ffn1_up, ffn1_down))
    w2 = tuple(w.astype(BF16) for w in (ffn2_gate, ffn2_up, ffn2_down))
    for layer in range(depth):
        h = _ffn(h, g1, *w1, layer)
        if layer % 2 == 0:
            e = layer // 2
            h = _even_mixer(h, pos_b, gm[layer], ev_w_in[e], ev_sinks[e], s5_a_re[e], s5_a_im[e], s5_log_dt[e],
                            s5_b_re[e], s5_b_im[e], s5_c_re[e], s5_c_im[e], s5_d[e].reshape(-1), s5_w_glu[e],
                            s5_b_glu[e], ev_w_out[e], batch=batch, seq=seq)
        else:
            o = layer // 2
            h = _odd_mixer(h, gm[layer], m_w_in[o], m_conv_w[o], m_conv_b[o], m_dt_bias[o], m_a_log[o], m_d[o],
                           m_norm[o], m_w_out[o], batch=batch, seq=seq)
        h = _ffn(h, g2, *w2, layer, final_norm.astype(F32).reshape(1, d) if layer == depth - 1 else None)
    return h.reshape(batch, seq, d).astype(x.dtype)
```

```python
import functools
import math

import jax
import jax.numpy as jnp
from jax import lax
from jax.experimental import pallas as pl
from jax.experimental.pallas import tpu as pltpu

F32 = jnp.float32
BF16 = jnp.bfloat16

NORM_EPS = 1e-5
NEG_INF = -1e30
LANES = 128
SUBLANES = 8
VMEM_LIMIT_BYTES = 60 * 1024 * 1024

HEAD_DIM = 64
ATTN_BLOCK = 128
ROPE_THETA = 500000.0
ROPE_DIM = HEAD_DIM // 4
ATTN_BLOCKS_PER_STEP = 4
S5_GROUP = 16
S5_STATE = 64
S5_SEGMENTS = SUBLANES
S5_GROUPS_PER_TILE = 16
M_HEAD_DIM = 64
M_STATE = 128
M_CHUNK = 128
M_CONV = 4


def _params(*sem):
    return pltpu.CompilerParams(dimension_semantics=sem, vmem_limit_bytes=VMEM_LIMIT_BYTES)


def _rms(x, g):
    return x * lax.rsqrt(jnp.mean(x * x, axis=-1, keepdims=True) + NORM_EPS) * g


def _ffn_body(final, cast_next, x_ref, g_ref, wg_ref, wu_ref, wd_ref, *rest):
    xn_ref = rest[-1]
    o_ref = rest[-5] if cast_next else rest[-2]
    j = pl.program_id(1)
    if cast_next:
        n_in = 1 if final else 0
        for src, dst in zip(rest[n_in:n_in + 3], rest[-4:-1]):
            dst[...] = src[...].astype(BF16)

    @pl.when(j == 0)
    def _():
        x = x_ref[...]
        xn_ref[...] = _rms(x, g_ref[...]).astype(BF16)
        o_ref[...] = x

    xn = xn_ref[...]
    gate = jnp.dot(xn, wg_ref[...], preferred_element_type=F32)
    up = jnp.dot(xn, wu_ref[...], preferred_element_type=F32)
    h = (jax.nn.silu(gate) * up * 0.5).astype(BF16)
    o_ref[...] += jnp.dot(h, wd_ref[...], preferred_element_type=F32)

    if final:
        @pl.when(j == pl.num_programs(1) - 1)
        def _():
            o_ref[...] = _rms(o_ref[...], rest[0][...])


def _ffn(x, g, wg, wu, wd, layer, final_g=None, next_f32=None, *, tm=1024, tf=512):
    t, d = x.shape
    dff = wg.shape[-1]
    ni, nj = t // tm, dff // tf
    in_specs = [
        pl.BlockSpec((tm, d), lambda i, j: (i, 0)),
        pl.BlockSpec((None, 1, d), lambda i, j: (layer, 0, 0)),
        pl.BlockSpec((d, tf), lambda i, j: (0, j)),
        pl.BlockSpec((d, tf), lambda i, j: (0, j)),
        pl.BlockSpec((tf, d), lambda i, j: (j, 0)),
    ]
    args = [x, g, wg, wu, wd]
    out_specs = [pl.BlockSpec((tm, d), lambda i, j: (i, 0))]
    out_shape = [jax.ShapeDtypeStruct((t, d), F32)]
    if final_g is not None:
        in_specs.append(pl.BlockSpec((1, d), lambda i, j: (0, 0)))
        args.append(final_g)
    if next_f32 is not None:
        ng, nu, nd, nl = next_f32
        assert d % ni == 0 and dff % nj == 0
        up_blk, dn_blk = (d // ni, dff // nj), (dff // nj, d // ni)
        in_specs += [pl.BlockSpec((None,) + up_blk, lambda i, j: (nl, i, j)),
                     pl.BlockSpec((None,) + up_blk, lambda i, j: (nl, i, j)),
                     pl.BlockSpec((None,) + dn_blk, lambda i, j: (nl, j, i))]
        args += [ng, nu, nd]
        out_specs += [pl.BlockSpec(up_blk, lambda i, j: (i, j)), pl.BlockSpec(up_blk, lambda i, j: (i, j)),
                      pl.BlockSpec(dn_blk, lambda i, j: (j, i))]
        out_shape += [jax.ShapeDtypeStruct((d, dff), BF16)] * 2 + [jax.ShapeDtypeStruct((dff, d), BF16)]
    outs = pl.pallas_call(
        functools.partial(_ffn_body, final_g is not None, next_f32 is not None),
        grid=(ni, nj),
        in_specs=in_specs,
        out_specs=out_specs,
        out_shape=out_shape,
        scratch_shapes=[pltpu.VMEM((tm, d), BF16)],
        compiler_params=_params("parallel", "arbitrary"),
        name="swiglu_ffn",
    )(*args)
    return outs[0], tuple(outs[1:])


def _outproj_body(n_terms, x_ref, *refs):
    o_ref = refs[-1]
    acc = x_ref[...]
    for i in range(n_terms):
        acc = acc + jnp.dot(refs[2 * i][...], refs[2 * i + 1][...], preferred_element_type=F32)
    o_ref[...] = acc


def _outproj(x, terms, *, tm):
    t, d = x.shape
    in_specs = [pl.BlockSpec((tm, d), lambda i: (i, 0))]
    args = [x]
    for a, a_spec, w in terms:
        in_specs += [a_spec, pl.BlockSpec(w.shape, lambda i: (0, 0))]
        args += [a, w]
    return pl.pallas_call(
        functools.partial(_outproj_body, len(terms)),
        grid=(t // tm,),
        in_specs=in_specs,
        out_specs=pl.BlockSpec((tm, d), lambda i: (i, 0)),
        out_shape=jax.ShapeDtypeStruct((t, d), F32),
        compiler_params=_params("parallel"),
        name="residual_outproj",
    )(*args)


def _even_inproj_body(n_rot, x_ref, g_ref, w_ref, pos_ref, invf_ref, q_ref, k_ref, v_ref, u_ref, xn_ref):
    xn_ref[...] = _rms(x_ref[...], g_ref[...]).astype(BF16)
    ang = pos_ref[...] * invf_ref[...]
    cos, sin = jnp.cos(ang), jnp.sin(ang)
    lane = lax.broadcasted_iota(jnp.int32, ang.shape, 1) % HEAD_DIM
    half = ROPE_DIM // 2
    sin_from_lower = jnp.where((lane >= half) & (lane < ROPE_DIM), sin, 0.0)
    sin_from_upper = jnp.where(lane < half, sin, 0.0)

    cw = 512
    qw, kw, vw = q_ref.shape[-1], k_ref.shape[-1], v_ref.shape[-1]
    for c0 in range(0, w_ref.shape[-1], cw):
        p = jnp.dot(xn_ref[...], w_ref[:, c0:c0 + cw], preferred_element_type=F32)
        for l0 in range(0, cw, LANES):
            col = c0 + l0
            t = p[:, l0:l0 + LANES]
            if col < n_rot:
                t = (t * cos + pltpu.roll(t, half, axis=1) * sin_from_lower
                     - pltpu.roll(t, LANES - half, axis=1) * sin_from_upper)
            if col < qw:
                q_ref[:, col:col + LANES] = t.astype(BF16)
            elif col < qw + kw:
                k_ref[:, col - qw:col - qw + LANES] = t.astype(BF16)
            elif col < qw + kw + vw:
                v_ref[:, col - qw - kw:col - qw - kw + LANES] = t.astype(BF16)
            else:
                c = col - qw - kw - vw
                u_ref[:, c:c + LANES] = t


def _even_inproj(x, g, w, pos_b, invf, *, batch, seq, qw, kw, vw, uw):
    t, d = x.shape
    tm = seq // S5_SEGMENTS
    nseg = S5_SEGMENTS
    return pl.pallas_call(
        functools.partial(_even_inproj_body, qw + kw),
        grid=(t // tm,),
        in_specs=[
            pl.BlockSpec((tm, d), lambda i: (i, 0)),
            pl.BlockSpec((1, d), lambda i: (0, 0)),
            pl.BlockSpec(w.shape, lambda i: (0, 0)),
            pl.BlockSpec((tm, LANES), lambda i: (i, 0)),
            pl.BlockSpec((1, LANES), lambda i: (0, 0)),
        ],
        out_specs=[
            pl.BlockSpec((tm, qw), lambda i: (i, 0)),
            pl.BlockSpec((tm, kw), lambda i: (i, 0)),
            pl.BlockSpec((tm, vw), lambda i: (i, 0)),
            pl.BlockSpec((None, tm, uw), lambda i: (i // nseg, 0, i % nseg)),
        ],
        out_shape=[
            jax.ShapeDtypeStruct((t, qw), BF16),
            jax.ShapeDtypeStruct((t, kw), BF16),
            jax.ShapeDtypeStruct((t, vw), BF16),
            jax.ShapeDtypeStruct((batch, tm, nseg * uw), F32),
        ],
        scratch_shapes=[pltpu.VMEM((tm, d), BF16)],
        compiler_params=_params("parallel"),
        name="even_inproj_rotary",
    )(x, g, w, pos_b, invf)


def _attn_body(pairs_per_kv, q_ref, kc_ref, kp_ref, vc_ref, vp_ref, sink_ref, o_ref):
    m_step = pl.program_id(1)
    blk = ATTN_BLOCK
    kk = jnp.concatenate([kp_ref[...], kc_ref[...]], axis=0)
    vv = jnp.concatenate([vp_ref[...], vc_ref[...]], axis=0)
    lower = lax.broadcasted_iota(jnp.int32, (blk, LANES), 1) < HEAD_DIM
    key = lax.broadcasted_iota(jnp.int32, (2 * blk, 2 * blk), 0)
    qrow = lax.broadcasted_iota(jnp.int32, (2 * blk, 2 * blk), 1) % blk
    band = (key > qrow) & (key <= qrow + blk)
    upper_rows = lax.broadcasted_iota(jnp.int32, (LANES, blk), 0) >= HEAD_DIM
    n_pairs = q_ref.shape[-1] // LANES
    scale = 1.0 / math.sqrt(HEAD_DIM)

    def scores(t, c):
        hk = c // pairs_per_kv
        qc = q_ref[t * blk:(t + 1) * blk, c * LANES:(c + 1) * LANES] * scale
        q2 = jnp.concatenate([jnp.where(lower, qc, jnp.zeros_like(qc)),
                              jnp.where(lower, jnp.zeros_like(qc), qc)], axis=0)
        kh = kk[t * blk:t * blk + 2 * blk, hk * LANES:(hk + 1) * LANES]
        return lax.dot_general(kh, q2, (((1,), (1,)), ((), ())), preferred_element_type=F32)

    def finish(t, c, s):
        hk = c // pairs_per_kv
        first = (m_step == 0) if t == 0 else False
        valid = band & ((key >= blk) | jnp.logical_not(first))
        s = jnp.where(valid, s, NEG_INF)
        sink = sink_ref[c]
        mx = jnp.maximum(jnp.max(s, axis=0, keepdims=True), sink)
        p = jnp.exp(s - mx)
        den = jnp.sum(p, axis=0, keepdims=True) + jnp.exp(sink - mx)
        p = (p * (1.0 / den)).astype(BF16)
        vh = vv[t * blk:t * blk + 2 * blk, hk * LANES:(hk + 1) * LANES]
        r = lax.dot_general(vh, p, (((0,), (0,)), ((), ())), preferred_element_type=F32)
        o_t = jnp.where(upper_rows, r[:, blk:], r[:, :blk])
        o_ref[t * blk:(t + 1) * blk, c * LANES:(c + 1) * LANES] = o_t.T.astype(BF16)

    units = [(t, c) for t in range(ATTN_BLOCKS_PER_STEP) for c in range(n_pairs)]
    ahead = 4
    pending = [scores(*u) for u in units[:ahead]]
    for n, (t, c) in enumerate(units):
        if n + ahead < len(units):
            pending.append(scores(*units[n + ahead]))
        finish(t, c, pending.pop(0))


def _attention(q, kd, vd, sink_rows, *, batch, seq, pairs_per_kv):
    t, qw = q.shape
    kw = kd.shape[-1]
    rows = ATTN_BLOCK * ATTN_BLOCKS_PER_STEP
    steps = seq // rows
    bps = ATTN_BLOCKS_PER_STEP
    nblk = seq // ATTN_BLOCK

    def cur(b, m):
        return (b * steps + m, 0)

    def prev(b, m):
        return (b * nblk + jnp.maximum(m * bps - 1, 0), 0)

    return pl.pallas_call(
        functools.partial(_attn_body, pairs_per_kv),
        grid=(batch, steps),
        in_specs=[
            pl.BlockSpec((rows, qw), cur),
            pl.BlockSpec((rows, kw), cur),
            pl.BlockSpec((ATTN_BLOCK, kw), prev),
            pl.BlockSpec((rows, kw), cur),
            pl.BlockSpec((ATTN_BLOCK, kw), prev),
            pl.BlockSpec(sink_rows.shape, lambda b, m: (0, 0, 0)),
        ],
        out_specs=pl.BlockSpec((rows, qw), cur),
        out_shape=jax.ShapeDtypeStruct((t, qw), BF16),
        compiler_params=_params("parallel", "arbitrary"),
        name="swa_attention",
    )(q, kd, kd, vd, vd, sink_rows)


def _s5_discretise_body(are_ref, aim_ref, logdt_ref, bre_ref, bim_ref, abr_ref, abi_ref, bbr_ref, bbi_ref):
    are, aim = are_ref[...], aim_ref[...]
    dt = jnp.exp(logdt_ref[...])
    mag = jnp.exp(are * dt)
    abar_re, abar_im = mag * jnp.cos(aim * dt), mag * jnp.sin(aim * dt)
    nr, ni = abar_re - 1.0, abar_im
    den = are * are + aim * aim
    coef_re = (nr * are + ni * aim) / den
    coef_im = (ni * are - nr * aim) / den
    bre, bim = bre_ref[...], bim_ref[...]
    abr_ref[...] = abar_re
    abi_ref[...] = abar_im
    bbr_ref[...] = coef_re * bre - coef_im * bim
    bbi_ref[...] = coef_re * bim + coef_im * bre


def _s5_discretise(a_re, a_im, log_dt, b_re, b_im):
    g, p = a_re.shape
    c = b_re.shape[-1]
    col = lambda v: v.reshape(g * p, 1)
    logdt_col = jnp.broadcast_to(log_dt[:, None], (g, p)).reshape(g * p, 1)
    outs = pl.pallas_call(
        _s5_discretise_body,
        out_shape=[jax.ShapeDtypeStruct((g * p, 1), F32)] * 2 + [jax.ShapeDtypeStruct((g * p, c), F32)] * 2,
        name="s5_discretise",
    )(col(a_re), col(a_im), logdt_col, b_re.reshape(g * p, c), b_im.reshape(g * p, c))
    abr, abi, bbr, bbi = outs
    return abr.reshape(g, p), abi.reshape(g, p), bbr.reshape(g, p, c), bbi.reshape(g, p, c)


def _cmul(ar, ai, br, bi):
    return ar * br - ai * bi, ar * bi + ai * br


def _s5_body(n_tiles, steps_per_block, u_ref, wb_ref, wc_ref, are_ref, aim_ref, d_ref, wglu_ref, bglu_ref,
             o_ref, us_ref, ys_ref, state_ref):
    phase = pl.program_id(1)
    blk = pl.program_id(2)
    nblk = pl.num_programs(2)
    sw = are_ref.shape[-1]
    ncol = sw // LANES
    cw = wc_ref.shape[-1]
    nslab = us_ref.shape[0]
    nseg = S5_SEGMENTS

    @pl.when((phase == 0) & (blk == 0))
    def _():
        state_ref[...] = jnp.zeros_like(state_ref)

    for s in range(nseg):
        for c in range(nslab):
            us_ref[c, pl.ds(s, steps_per_block, stride=nseg), :] = u_ref[:, (s * nslab + c) * LANES:
                                                                         (s * nslab + c + 1) * LANES]

    def u_cols(c0, n):
        return jnp.concatenate([us_ref[c0 + i] for i in range(n)], axis=1)

    def input_drive(kt):
        ub = u_cols(kt * (cw // LANES), cw // LANES).astype(BF16)
        return jnp.dot(ub, wb_ref[kt], preferred_element_type=F32)

    def scan_tile(kt, bu, store):
        ar = [jnp.broadcast_to(are_ref[kt, :, c * LANES:(c + 1) * LANES], (SUBLANES, LANES)) for c in range(ncol)]
        ai = [jnp.broadcast_to(aim_ref[kt, :, c * LANES:(c + 1) * LANES], (SUBLANES, LANES)) for c in range(ncol)]
        h = [state_ref[kt, :, c * LANES:(c + 1) * LANES] for c in range(2 * ncol)]
        rows_out = []
        for j in range(steps_per_block):
            r0 = j * SUBLANES
            hr_new, hi_new = [], []
            for c in range(ncol):
                pr, pi = _cmul(ar[c], ai[c], h[c], h[ncol + c])
                hr_new.append(pr + bu[r0:r0 + SUBLANES, c * LANES:(c + 1) * LANES])
                hi_new.append(pi + bu[r0:r0 + SUBLANES, sw + c * LANES:sw + (c + 1) * LANES])
            h = hr_new + hi_new
            if store:
                rows_out.append(jnp.concatenate(h, axis=1))
        for c in range(2 * ncol):
            state_ref[kt, :, c * LANES:(c + 1) * LANES] = h[c]
        return jnp.concatenate(rows_out, axis=0) if store else None

    @pl.when(phase == 0)
    def _():
        bus = [input_drive(kt) for kt in range(n_tiles)]
        for kt in range(n_tiles):
            scan_tile(kt, bus[kt], False)

    @pl.when((phase == 0) & (blk == nblk - 1))
    def _():
        seg_len = steps_per_block * nblk
        for kt in range(n_tiles):
            pr, pi = are_ref[kt], aim_ref[kt]
            k = 1
            while k < seg_len:
                pr, pi = _cmul(pr, pi, pr, pi)
                k *= 2
            fr, fi = state_ref[kt, :, :sw], state_ref[kt, :, sw:]
            hr, hi = jnp.zeros_like(fr), jnp.zeros_like(fi)
            srow = lax.broadcasted_iota(jnp.int32, fr.shape, 0)
            for s in range(1, S5_SEGMENTS):
                qr, qi = _cmul(pr, pi, hr, hi)
                nr = pltpu.roll(fr + qr, 1, axis=0)
                ni = pltpu.roll(fi + qi, 1, axis=0)
                hr = jnp.where(srow == s, nr, hr)
                hi = jnp.where(srow == s, ni, hi)
            state_ref[kt, :, :sw] = hr
            state_ref[kt, :, sw:] = hi

    @pl.when(phase == 1)
    def _():
        bus = [input_drive(kt) for kt in range(n_tiles)]
        ys = [jnp.dot(scan_tile(kt, bus[kt], True).astype(BF16), wc_ref[kt], preferred_element_type=F32)
              for kt in range(n_tiles)]
        y = jnp.concatenate(ys, axis=1) + d_ref[...] * u_cols(0, nslab)
        gl = jax.nn.gelu(y)
        gate = jnp.dot(gl.astype(BF16), wglu_ref[...], preferred_element_type=F32) + bglu_ref[...]
        out = gl * jax.nn.sigmoid(gate)
        for c in range(nslab):
            ys_ref[c] = out[:, c * LANES:(c + 1) * LANES]
        for s in range(nseg):
            for c in range(nslab):
                o_ref[:, (s * nslab + c) * LANES:(s * nslab + c + 1) * LANES] = (
                    ys_ref[c, pl.ds(s, steps_per_block, stride=nseg), :].astype(BF16))


def _s5(u_cat, wb, wc, a_re_t, a_im_t, d_row, w_glu, b_glu, *, steps_per_block=32):
    batch, seg_len, cat_width = u_cat.shape
    width = cat_width // S5_SEGMENTS
    n_tiles, cw, sw2 = wb.shape
    nblk = seg_len // steps_per_block
    steps = steps_per_block
    rows_per_block = steps * S5_SEGMENTS
    assert seg_len & (seg_len - 1) == 0, "segment length must be a power of two (abar ** seg_len by squaring)"
    const3 = lambda b, p, k: (0, 0, 0)
    const2 = lambda b, p, k: (0, 0)
    return pl.pallas_call(
        functools.partial(_s5_body, n_tiles, steps),
        grid=(batch, 2, nblk),
        in_specs=[
            pl.BlockSpec((None, steps, cat_width), lambda b, p, k: (b, k, 0)),
            pl.BlockSpec(wb.shape, const3),
            pl.BlockSpec(wc.shape, const3),
            pl.BlockSpec(a_re_t.shape, const3),
            pl.BlockSpec(a_im_t.shape, const3),
            pl.BlockSpec(d_row.shape, const2),
            pl.BlockSpec(w_glu.shape, const2),
            pl.BlockSpec(b_glu.shape, const2),
        ],
        out_specs=pl.BlockSpec((None, steps, cat_width), lambda b, p, k: (b, k * p, 0)),
        out_shape=jax.ShapeDtypeStruct((batch, seg_len, cat_width), BF16),
        scratch_shapes=[
            pltpu.VMEM((width // LANES, rows_per_block, LANES), F32),
            pltpu.VMEM((width // LANES, rows_per_block, LANES), F32),
            pltpu.VMEM((n_tiles, SUBLANES, sw2), F32),
        ],
        compiler_params=_params("parallel", "arbitrary", "arbitrary"),
        name="s5_scan_glu",
    )(u_cat, wb, wc, a_re_t, a_im_t, d_row, w_glu, b_glu)


HALO_ROWS = 16


def _odd_inproj_body(n_plain, tiles_per_seq, x_ref, xh_ref, g_ref, w_ref, wdt_ref, cw_ref, cb_ref,
                     o_ref, dt_ref, xn_ref):
    i, j = pl.program_id(0), pl.program_id(1)

    @pl.when(j == 0)
    def _():
        g = g_ref[...]
        xn_ref[0:HALO_ROWS, :] = _rms(xh_ref[...], g).astype(BF16)
        xn_ref[HALO_ROWS:, :] = _rms(x_ref[...], g).astype(BF16)
        dt_ref[...] = jnp.dot(xn_ref[HALO_ROWS:, :], wdt_ref[...], preferred_element_type=F32)

    @pl.when(j < n_plain)
    def _():
        o_ref[...] = jnp.dot(xn_ref[HALO_ROWS:, :], w_ref[...], preferred_element_type=F32)

    @pl.when(j >= n_plain)
    def _():
        p = jnp.dot(xn_ref[...], w_ref[...], preferred_element_type=F32)
        seq_start = (i % tiles_per_seq) == 0
        halo = jnp.where(seq_start, 0.0, p[:HALO_ROWS])
        ext = jnp.concatenate([halo, p[HALO_ROWS:]], axis=0)
        acc = cb_ref[...] + cw_ref[M_CONV - 1:M_CONV, :] * p[HALO_ROWS:]
        for k in range(1, M_CONV):
            acc = acc + cw_ref[M_CONV - 1 - k:M_CONV - k, :] * ext[HALO_ROWS - k:ext.shape[0] - k]
        o_ref[...] = jax.nn.silu(acc)


def _odd_inproj(x, g, w, wdt, conv_w, conv_b, *, seq, n_plain_cols, tm=1024, tn=1024):
    t, d = x.shape
    n = w.shape[-1]
    assert seq % tm == 0 and n_plain_cols % tn == 0
    n_plain = n_plain_cols // tn
    halo_blocks = tm // HALO_ROWS
    conv_col = lambda i, j: (0, jnp.maximum(j - n_plain, 0))
    return pl.pallas_call(
        functools.partial(_odd_inproj_body, n_plain, seq // tm),
        grid=(t // tm, n // tn),
        in_specs=[
            pl.BlockSpec((tm, d), lambda i, j: (i, 0)),
            pl.BlockSpec((HALO_ROWS, d), lambda i, j: (jnp.maximum(i * halo_blocks - 1, 0), 0)),
            pl.BlockSpec((1, d), lambda i, j: (0, 0)),
            pl.BlockSpec((d, tn), lambda i, j: (0, j)),
            pl.BlockSpec(wdt.shape, lambda i, j: (0, 0)),
            pl.BlockSpec((M_CONV, tn), conv_col),
            pl.BlockSpec((1, tn), conv_col),
        ],
        out_specs=[
            pl.BlockSpec((tm, tn), lambda i, j: (i, j)),
            pl.BlockSpec((tm, wdt.shape[-1]), lambda i, j: (i, 0)),
        ],
        out_shape=[jax.ShapeDtypeStruct((t, n), F32), jax.ShapeDtypeStruct((t, wdt.shape[-1]), F32)],
        scratch_shapes=[pltpu.VMEM((HALO_ROWS + tm, d), BF16)],
        compiler_params=_params("parallel", "arbitrary"),
        name="odd_inproj_conv",
    )(x, x, g, w, wdt, conv_w, conv_b)


def _ssd_body(n_groups, z_ref, xs_ref, bm_ref, cm_ref, dt_ref, dtb_ref, alog_ref, dsk_ref, ng_ref,
              o_ref, state_ref, y_ref):
    c = pl.program_id(1)
    L = M_CHUNK
    inner = xs_ref.shape[-1]
    gw = bm_ref.shape[-1] // n_groups
    hw = inner // n_groups

    @pl.when(c == 0)
    def _():
        state_ref[...] = jnp.zeros_like(state_ref)

    dt = jax.nn.softplus(dt_ref[...] + dtb_ref[...])
    a = -jnp.exp(alog_ref[...])
    acum = a * dt
    rowi = lax.broadcasted_iota(jnp.int32, acum.shape, 0)
    k = 1
    while k < L:
        acum = acum + jnp.where(rowi >= k, pltpu.roll(acum, k, axis=0), 0.0)
        k *= 2
    src = acum - jnp.log(dt)
    src_t = src.T
    a_last = acum[L - 1:L, :]

    li = lax.broadcasted_iota(jnp.int32, (L, L), 0)
    si = lax.broadcasted_iota(jnp.int32, (L, L), 1)
    causal = li >= si
    lane = lax.broadcasted_iota(jnp.int32, (L, LANES), 1)
    lower = lane < M_HEAD_DIM
    heads_per_group = hw // M_HEAD_DIM

    for g in range(n_groups):
        cgb = cm_ref[:, g * gw:(g + 1) * gw].astype(BF16)
        bgb = bm_ref[:, g * gw:(g + 1) * gw].astype(BF16)
        cb = lax.dot_general(cgb, bgb, (((1,), (1,)), ((), ())), preferred_element_type=F32)
        prev = state_ref[:, g * hw:(g + 1) * hw]
        y_off = jnp.dot(cgb, prev.astype(BF16), preferred_element_type=F32)
        xd_parts = []
        for pr in range(heads_per_group // 2):
            h0 = g * heads_per_group + 2 * pr
            col0 = g * hw + pr * LANES
            xp = xs_ref[:, col0:col0 + LANES]
            ms = []
            for h in (h0, h0 + 1):
                seg = acum[:, h:h + 1] - src_t[h:h + 1, :]
                ms.append((cb * jnp.exp(jnp.where(causal, seg, -jnp.inf))).astype(BF16))
            lhs = jnp.concatenate(ms, axis=1)
            xpb = xp.astype(BF16)
            rhs = jnp.concatenate([jnp.where(lower, xpb, jnp.zeros_like(xpb)),
                                   jnp.where(lower, jnp.zeros_like(xpb), xpb)], axis=0)
            y_diag = jnp.dot(lhs, rhs, preferred_element_type=F32)
            al = jnp.where(lower, acum[:, h0:h0 + 1], acum[:, h0 + 1:h0 + 2])
            srcl = jnp.where(lower, src[:, h0:h0 + 1], src[:, h0 + 1:h0 + 2])
            alast = jnp.where(lower[0:1], a_last[:, h0:h0 + 1], a_last[:, h0 + 1:h0 + 2])
            y_ref[:, col0:col0 + LANES] = (y_diag + jnp.exp(al) * y_off[:, pr * LANES:(pr + 1) * LANES]
                                           + dsk_ref[:, col0:col0 + LANES] * xp)
            xd_parts.append((xp * jnp.exp(alast - srcl)).astype(BF16))
            state_ref[:, col0:col0 + LANES] = state_ref[:, col0:col0 + LANES] * jnp.exp(alast)
        xd = jnp.concatenate(xd_parts, axis=1)
        state_ref[:, g * hw:(g + 1) * hw] += lax.dot_general(
            bgb, xd, (((0,), (0,)), ((), ())), preferred_element_type=F32)

    for g in range(n_groups):
        sl = slice(g * hw, (g + 1) * hw)
        y = y_ref[:, sl] * jax.nn.silu(z_ref[:, sl])
        y = y * lax.rsqrt(jnp.mean(y * y, axis=-1, keepdims=True) + NORM_EPS)
        o_ref[:, sl] = (y * ng_ref[:, sl]).astype(BF16)


def _ssd(zxbc, dt_raw, dt_bias, a_log, d_row, norm_g, *, batch, seq, inner, n_groups):
    t = zxbc.shape[0]
    nchunks = seq // M_CHUNK
    gn = n_groups * M_STATE
    row = lambda b, c: (b * nchunks + c, 0)
    const = lambda b, c: (0, 0)
    return pl.pallas_call(
        functools.partial(_ssd_body, n_groups),
        grid=(batch, nchunks),
        in_specs=[
            pl.BlockSpec((M_CHUNK, inner), row),
            pl.BlockSpec((M_CHUNK, inner), lambda b, c: (b * nchunks + c, 1)),
            pl.BlockSpec((M_CHUNK, gn), lambda b, c: (b * nchunks + c, 2 * inner // gn)),
            pl.BlockSpec((M_CHUNK, gn), lambda b, c: (b * nchunks + c, 2 * inner // gn + 1)),
            pl.BlockSpec((M_CHUNK, LANES), row),
            pl.BlockSpec((1, LANES), const),
            pl.BlockSpec((1, LANES), const),
            pl.BlockSpec((1, inner), const),
            pl.BlockSpec((1, inner), const),
        ],
        out_specs=pl.BlockSpec((M_CHUNK, inner), row),
        out_shape=jax.ShapeDtypeStruct((t, inner), BF16),
        scratch_shapes=[pltpu.VMEM((M_STATE, inner), F32),
                        pltpu.VMEM((M_CHUNK, inner), F32)],
        compiler_params=_params("parallel", "arbitrary"),
        name="ssd_chunk",
    )(zxbc, zxbc, zxbc, zxbc, dt_raw, dt_bias, a_log, d_row, norm_g)


def _dup_heads(w, n_heads):
    d = w.shape[0]
    w = w.reshape(d, n_heads, 1, HEAD_DIM)
    return jnp.broadcast_to(w, (d, n_heads, 2, HEAD_DIM)).reshape(d, n_heads * 2 * HEAD_DIM)


def _block_diag_tiles(blocks, n_tiles):
    g, r, c = blocks.shape
    gpt = g // n_tiles
    eye = jnp.eye(gpt, dtype=blocks.dtype)
    b = blocks.reshape(n_tiles, gpt, r, c)
    out = jnp.einsum('tgrc,gh->tgrhc', b, eye)
    return out.reshape(n_tiles, gpt * r, gpt * c)


def _rope_inv_freq_row():
    half = ROPE_DIM // 2
    inv_freq = jnp.exp(-math.log(ROPE_THETA) * jnp.arange(half, dtype=F32) * (2.0 / ROPE_DIM))
    lane = jnp.arange(LANES) % HEAD_DIM
    return jnp.where(lane < ROPE_DIM, inv_freq[lane % half], 0.0).reshape(1, LANES).astype(F32)


def _even_mixer(x, pos_b, norm_g, w_in, sinks, a_re, a_im, log_dt, b_re, b_im, c_re, c_im, d_skip, w_glu, b_glu,
                w_out, *, batch, seq):
    d = x.shape[-1]
    n_q = sinks.shape[0]
    a_width = n_q * HEAD_DIM
    groups, state = a_re.shape
    s5_width = groups * S5_GROUP
    kv_width = (w_in.shape[-1] - a_width - s5_width) // 2
    n_kv = kv_width // HEAD_DIM
    grp = n_q // n_kv
    assert grp % 2 == 0 and a_width % LANES == 0

    wq = w_in[:, :a_width]
    wk = _dup_heads(w_in[:, a_width:a_width + kv_width], n_kv)
    wv = _dup_heads(w_in[:, a_width + kv_width:a_width + 2 * kv_width], n_kv)
    wu = w_in[:, a_width + 2 * kv_width:]
    w_all = jnp.concatenate([wq, wk, wv, wu], axis=1).astype(BF16)
    q, kd, vd, u_cat = _even_inproj(x, norm_g, w_all, pos_b, _rope_inv_freq_row(), batch=batch, seq=seq,
                                    qw=a_width, kw=2 * kv_width, vw=2 * kv_width, uw=s5_width)

    sink_rows = jnp.repeat(sinks.astype(F32).reshape(n_q // 2, 2), ATTN_BLOCK, axis=1).reshape(n_q // 2, 1, 2 * ATTN_BLOCK)
    attn = _attention(q, kd, vd, sink_rows, batch=batch, seq=seq, pairs_per_kv=grp // 2)

    abr, abi, bbr, bbi = _s5_discretise(a_re.astype(F32), a_im.astype(F32), log_dt.astype(F32),
                                        b_re.astype(F32), b_im.astype(F32))
    n_tiles = groups // S5_GROUPS_PER_TILE
    to_in = lambda bb: _block_diag_tiles(jnp.swapaxes(bb, 1, 2), n_tiles)
    wb = jnp.concatenate([to_in(bbr), to_in(bbi)], axis=-1).astype(BF16)
    to_out = lambda cc: _block_diag_tiles(jnp.swapaxes(cc.astype(F32), 1, 2), n_tiles)
    wc = jnp.concatenate([to_out(c_re), -to_out(c_im)], axis=1).astype(BF16)
    a_re_t = abr.reshape(n_tiles, 1, -1)
    a_im_t = abi.reshape(n_tiles, 1, -1)

    seg_len = seq // S5_SEGMENTS
    ssm_cat = _s5(u_cat, wb, wc, a_re_t, a_im_t, d_skip.astype(F32).reshape(1, s5_width),
                  w_glu.astype(BF16), b_glu.astype(F32).reshape(1, s5_width))

    tm = seg_len
    nseg = S5_SEGMENTS
    w_o = w_out.astype(BF16)
    terms = [
        (attn, pl.BlockSpec((tm, a_width), lambda i: (i, 0)), w_o[:a_width]),
        (ssm_cat, pl.BlockSpec((None, tm, s5_width), lambda i: (i // nseg, 0, i % nseg)), w_o[a_width:]),
    ]
    return _outproj(x, terms, tm=tm)


def _odd_mixer(x, norm_g, w_in, conv_w, conv_b, dt_bias, a_log, d_skip, norm_out, w_out, *, batch, seq):
    inner = norm_out.shape[0]
    heads = a_log.shape[0]
    conv_dim = conv_w.shape[-1]
    n_groups = (conv_dim - inner) // (2 * M_STATE)
    assert heads <= LANES and inner // heads == M_HEAD_DIM
    pad = LANES - heads
    w_main = w_in[:, :inner + conv_dim].astype(BF16)
    w_dt = jnp.pad(w_in[:, inner + conv_dim:], ((0, 0), (0, pad))).astype(BF16)
    zxbc, dt_raw = _odd_inproj(x, norm_g, w_main, w_dt, conv_w.astype(F32), conv_b.astype(F32).reshape(1, conv_dim),
                               seq=seq, n_plain_cols=inner)
    row = lambda v: jnp.pad(v.astype(F32), (0, pad)).reshape(1, LANES)
    y = _ssd(zxbc, dt_raw, row(dt_bias), row(a_log), jnp.repeat(d_skip.astype(F32), M_HEAD_DIM).reshape(1, inner),
             norm_out.astype(F32).reshape(1, inner), batch=batch, seq=seq, inner=inner, n_groups=n_groups)
    tm = 512
    return _outproj(x, [(y, pl.BlockSpec((tm, inner), lambda i: (i, 0)), w_out.astype(BF16))], tm=tm)


def kernel(x, positions, norm_ffn1, ffn1_gate, ffn1_up, ffn1_down, norm_mix, norm_ffn2, ffn2_gate, ffn2_up,
           ffn2_down, ev_w_in, ev_sinks, s5_a_re, s5_a_im, s5_log_dt, s5_b_re, s5_b_im, s5_c_re, s5_c_im, s5_d,
           s5_w_glu, s5_b_glu, ev_w_out, m_w_in, m_conv_w, m_conv_b, m_dt_bias, m_a_log, m_d, m_norm, m_w_out,
           final_norm):
    batch, seq, d = x.shape
    depth = norm_ffn1.shape[0]
    t = batch * seq
    h = x.reshape(t, d).astype(F32)
    pos_b = jnp.broadcast_to(positions.reshape(t, 1).astype(F32), (t, LANES))
    g1, g2, gm = (v.astype(F32).reshape(depth, 1, d) for v in (norm_ffn1, norm_ffn2, norm_mix))
    ffn_f32 = [(ffn1_gate, ffn1_up, ffn1_down), (ffn2_gate, ffn2_up, ffn2_down)]
    w_cur = tuple(w[0].astype(BF16) for w in ffn_f32[0])
    for layer in range(depth):
        h, w_cur = _ffn(h, g1, *w_cur, layer, next_f32=ffn_f32[1] + (layer,))
        if layer % 2 == 0:
            e = layer // 2
            h = _even_mixer(h, pos_b, gm[layer], ev_w_in[e], ev_sinks[e], s5_a_re[e], s5_a_im[e], s5_log_dt[e],
                            s5_b_re[e], s5_b_im[e], s5_c_re[e], s5_c_im[e], s5_d[e].reshape(-1), s5_w_glu[e],
                            s5_b_glu[e], ev_w_out[e], batch=batch, seq=seq)
        else:
            o = layer // 2
            h = _odd_mixer(h, gm[layer], m_w_in[o], m_conv_w[o], m_conv_b[o], m_dt_bias[o], m_a_log[o], m_d[o],
                           m_norm[o], m_w_out[o], batch=batch, seq=seq)
        if layer == depth - 1:
            h, _ = _ffn(h, g2, *w_cur, layer, final_g=final_norm.astype(F32).reshape(1, d))
        else:
            h, w_cur = _ffn(h, g2, *w_cur, layer, next_f32=ffn_f32[0] + (layer + 1,))
    return h.reshape(batch, seq, d).astype(x.dtype)
```

```python
import functools
import math

import jax
import jax.numpy as jnp
from jax import lax
from jax.experimental import pallas as pl
from jax.experimental.pallas import tpu as pltpu

F32 = jnp.float32
BF16 = jnp.bfloat16

NORM_EPS = 1e-5
NEG_INF = -1e30
LANES = 128
SUBLANES = 8
VMEM_LIMIT_BYTES = 60 * 1024 * 1024

HEAD_DIM = 64
ATTN_BLOCK = 128
ROPE_THETA = 500000.0
ROPE_DIM = HEAD_DIM // 4
ATTN_BLOCKS_PER_STEP = 4
S5_GROUP = 16
S5_STATE = 64
S5_GROUPS_PER_TILE = 16
M_HEAD_DIM = 64
M_STATE = 128
M_CHUNK = 128
M_CONV = 4


def _params(*sem):
    return pltpu.CompilerParams(dimension_semantics=sem, vmem_limit_bytes=VMEM_LIMIT_BYTES)


def _rms(x, g):
    return x * lax.rsqrt(jnp.mean(x * x, axis=-1, keepdims=True) + NORM_EPS) * g


def _ffn_body(final, cast_next, x_ref, g_ref, wg_ref, wu_ref, wd_ref, *rest):
    xn_ref = rest[-1]
    o_ref = rest[-5] if cast_next else rest[-2]
    j = pl.program_id(1)
    if cast_next:
        n_in = 1 if final else 0
        for src, dst in zip(rest[n_in:n_in + 3], rest[-4:-1]):
            dst[...] = src[...].astype(BF16)

    @pl.when(j == 0)
    def _():
        x = x_ref[...]
        xn_ref[...] = _rms(x, g_ref[...]).astype(BF16)
        o_ref[...] = x

    xn = xn_ref[...]
    gate = jnp.dot(xn, wg_ref[...], preferred_element_type=F32)
    up = jnp.dot(xn, wu_ref[...], preferred_element_type=F32)
    h = (jax.nn.silu(gate) * up * 0.5).astype(BF16)
    o_ref[...] += jnp.dot(h, wd_ref[...], preferred_element_type=F32)

    if final:
        @pl.when(j == pl.num_programs(1) - 1)
        def _():
            o_ref[...] = _rms(o_ref[...], rest[0][...])


def _ffn(x, g, wg, wu, wd, layer, final_g=None, next_f32=None, *, tm=1024, tf=512):
    t, d = x.shape
    dff = wg.shape[-1]
    ni, nj = t // tm, dff // tf
    in_specs = [
        pl.BlockSpec((tm, d), lambda i, j: (i, 0)),
        pl.BlockSpec((None, 1, d), lambda i, j: (layer, 0, 0)),
        pl.BlockSpec((d, tf), lambda i, j: (0, j)),
        pl.BlockSpec((d, tf), lambda i, j: (0, j)),
        pl.BlockSpec((tf, d), lambda i, j: (j, 0)),
    ]
    args = [x, g, wg, wu, wd]
    out_specs = [pl.BlockSpec((tm, d), lambda i, j: (i, 0))]
    out_shape = [jax.ShapeDtypeStruct((t, d), F32)]
    if final_g is not None:
        in_specs.append(pl.BlockSpec((1, d), lambda i, j: (0, 0)))
        args.append(final_g)
    if next_f32 is not None:
        ng, nu, nd, nl = next_f32
        assert d % ni == 0 and dff % nj == 0
        up_blk, dn_blk = (d // ni, dff // nj), (dff // nj, d // ni)
        in_specs += [pl.BlockSpec((None,) + up_blk, lambda i, j: (nl, i, j)),
                     pl.BlockSpec((None,) + up_blk, lambda i, j: (nl, i, j)),
                     pl.BlockSpec((None,) + dn_blk, lambda i, j: (nl, j, i))]
        args += [ng, nu, nd]
        out_specs += [pl.BlockSpec(up_blk, lambda i, j: (i, j)), pl.BlockSpec(up_blk, lambda i, j: (i, j)),
                      pl.BlockSpec(dn_blk, lambda i, j: (j, i))]
        out_shape += [jax.ShapeDtypeStruct((d, dff), BF16)] * 2 + [jax.ShapeDtypeStruct((dff, d), BF16)]
    outs = pl.pallas_call(
        functools.partial(_ffn_body, final_g is not None, next_f32 is not None),
        grid=(ni, nj),
        in_specs=in_specs,
        out_specs=out_specs,
        out_shape=out_shape,
        scratch_shapes=[pltpu.VMEM((tm, d), BF16)],
        compiler_params=_params("parallel", "arbitrary"),
        name="swiglu_ffn",
    )(*args)
    return outs[0], tuple(outs[1:])


def _outproj_body(n_terms, x_ref, *refs):
    o_ref = refs[-1]
    acc = x_ref[...]
    for i in range(n_terms):
        acc = acc + jnp.dot(refs[2 * i][...], refs[2 * i + 1][...], preferred_element_type=F32)
    o_ref[...] = acc


def _outproj(x, terms, *, tm):
    t, d = x.shape
    in_specs = [pl.BlockSpec((tm, d), lambda i: (i, 0))]
    args = [x]
    for a, a_spec, w, w_spec in terms:
        in_specs += [a_spec, w_spec]
        args += [a, w]
    return pl.pallas_call(
        functools.partial(_outproj_body, len(terms)),
        grid=(t // tm,),
        in_specs=in_specs,
        out_specs=pl.BlockSpec((tm, d), lambda i: (i, 0)),
        out_shape=jax.ShapeDtypeStruct((t, d), F32),
        compiler_params=_params("parallel"),
        name="residual_outproj",
    )(*args)


def _even_inproj_body(n_rot, x_ref, g_ref, w_ref, pos_ref, invf_ref, q_ref, k_ref, v_ref, u_ref, xn_ref):
    xn_ref[...] = _rms(x_ref[...], g_ref[...]).astype(BF16)
    ang = pos_ref[...] * invf_ref[...]
    cos, sin = jnp.cos(ang), jnp.sin(ang)
    lane = lax.broadcasted_iota(jnp.int32, ang.shape, 1) % HEAD_DIM
    half = ROPE_DIM // 2
    sin_from_lower = jnp.where((lane >= half) & (lane < ROPE_DIM), sin, 0.0)
    sin_from_upper = jnp.where(lane < half, sin, 0.0)

    cw = 512
    qw, kw, vw = q_ref.shape[-1], k_ref.shape[-1], v_ref.shape[-1]
    for c0 in range(0, w_ref.shape[-1], cw):
        p = jnp.dot(xn_ref[...], w_ref[:, c0:c0 + cw], preferred_element_type=F32)
        for l0 in range(0, cw, LANES):
            col = c0 + l0
            t = p[:, l0:l0 + LANES]
            if col < n_rot:
                t = (t * cos + pltpu.roll(t, half, axis=1) * sin_from_lower
                     - pltpu.roll(t, LANES - half, axis=1) * sin_from_upper)
            if col < qw:
                q_ref[:, col:col + LANES] = t.astype(BF16)
            elif col < qw + kw:
                k_ref[:, col - qw:col - qw + LANES] = t.astype(BF16)
            elif col < qw + kw + vw:
                v_ref[:, col - qw - kw:col - qw - kw + LANES] = t.astype(BF16)
            else:
                c = col - qw - kw - vw
                u_ref[:, c:c + LANES] = t


def _even_inproj(x, g, w, pos_b, invf, *, qw, kw, vw, uw, tm=512):
    t, d = x.shape
    return pl.pallas_call(
        functools.partial(_even_inproj_body, qw + kw),
        grid=(t // tm,),
        in_specs=[
            pl.BlockSpec((tm, d), lambda i: (i, 0)),
            pl.BlockSpec((1, d), lambda i: (0, 0)),
            pl.BlockSpec(w.shape, lambda i: (0, 0)),
            pl.BlockSpec((tm, LANES), lambda i: (i, 0)),
            pl.BlockSpec((1, LANES), lambda i: (0, 0)),
        ],
        out_specs=[
            pl.BlockSpec((tm, qw), lambda i: (i, 0)),
            pl.BlockSpec((tm, kw), lambda i: (i, 0)),
            pl.BlockSpec((tm, vw), lambda i: (i, 0)),
            pl.BlockSpec((tm, uw), lambda i: (i, 0)),
        ],
        out_shape=[
            jax.ShapeDtypeStruct((t, qw), BF16),
            jax.ShapeDtypeStruct((t, kw), BF16),
            jax.ShapeDtypeStruct((t, vw), BF16),
            jax.ShapeDtypeStruct((t, uw), F32),
        ],
        scratch_shapes=[pltpu.VMEM((tm, d), BF16)],
        compiler_params=_params("parallel"),
        name="even_inproj_rotary",
    )(x, g, w, pos_b, invf)


def _attn_body(pairs_per_kv, q_ref, kc_ref, kp_ref, vc_ref, vp_ref, sink_ref, o_ref):
    m_step = pl.program_id(1)
    blk = ATTN_BLOCK
    kk = jnp.concatenate([kp_ref[...], kc_ref[...]], axis=0)
    vv = jnp.concatenate([vp_ref[...], vc_ref[...]], axis=0)
    lower = lax.broadcasted_iota(jnp.int32, (blk, LANES), 1) < HEAD_DIM
    key = lax.broadcasted_iota(jnp.int32, (2 * blk, 2 * blk), 0)
    qrow = lax.broadcasted_iota(jnp.int32, (2 * blk, 2 * blk), 1) % blk
    band = (key > qrow) & (key <= qrow + blk)
    upper_rows = lax.broadcasted_iota(jnp.int32, (LANES, blk), 0) >= HEAD_DIM
    n_pairs = q_ref.shape[-1] // LANES
    scale = 1.0 / math.sqrt(HEAD_DIM)

    def scores(t, c):
        hk = c // pairs_per_kv
        qc = q_ref[t * blk:(t + 1) * blk, c * LANES:(c + 1) * LANES] * scale
        q2 = jnp.concatenate([jnp.where(lower, qc, jnp.zeros_like(qc)),
                              jnp.where(lower, jnp.zeros_like(qc), qc)], axis=0)
        kh = kk[t * blk:t * blk + 2 * blk, hk * LANES:(hk + 1) * LANES]
        return lax.dot_general(kh, q2, (((1,), (1,)), ((), ())), preferred_element_type=F32)

    def finish(t, c, s):
        hk = c // pairs_per_kv
        first = (m_step == 0) if t == 0 else False
        valid = band & ((key >= blk) | jnp.logical_not(first))
        s = jnp.where(valid, s, NEG_INF)
        sink = sink_ref[c]
        mx = jnp.maximum(jnp.max(s, axis=0, keepdims=True), sink)
        p = jnp.exp(s - mx)
        den = jnp.sum(p, axis=0, keepdims=True) + jnp.exp(sink - mx)
        p = (p * (1.0 / den)).astype(BF16)
        vh = vv[t * blk:t * blk + 2 * blk, hk * LANES:(hk + 1) * LANES]
        r = lax.dot_general(vh, p, (((0,), (0,)), ((), ())), preferred_element_type=F32)
        o_t = jnp.where(upper_rows, r[:, blk:], r[:, :blk])
        o_ref[t * blk:(t + 1) * blk, c * LANES:(c + 1) * LANES] = o_t.T.astype(BF16)

    units = [(t, c) for t in range(ATTN_BLOCKS_PER_STEP) for c in range(n_pairs)]
    ahead = 4
    pending = [scores(*u) for u in units[:ahead]]
    for n, (t, c) in enumerate(units):
        if n + ahead < len(units):
            pending.append(scores(*units[n + ahead]))
        finish(t, c, pending.pop(0))


def _attention(q, kd, vd, sink_rows, *, batch, seq, pairs_per_kv):
    t, qw = q.shape
    kw = kd.shape[-1]
    rows = ATTN_BLOCK * ATTN_BLOCKS_PER_STEP
    steps = seq // rows
    bps = ATTN_BLOCKS_PER_STEP
    nblk = seq // ATTN_BLOCK

    def cur(b, m):
        return (b * steps + m, 0)

    def prev(b, m):
        return (b * nblk + jnp.maximum(m * bps - 1, 0), 0)

    return pl.pallas_call(
        functools.partial(_attn_body, pairs_per_kv),
        grid=(batch, steps),
        in_specs=[
            pl.BlockSpec((rows, qw), cur),
            pl.BlockSpec((rows, kw), cur),
            pl.BlockSpec((ATTN_BLOCK, kw), prev),
            pl.BlockSpec((rows, kw), cur),
            pl.BlockSpec((ATTN_BLOCK, kw), prev),
            pl.BlockSpec(sink_rows.shape, lambda b, m: (0, 0, 0)),
        ],
        out_specs=pl.BlockSpec((rows, qw), cur),
        out_shape=jax.ShapeDtypeStruct((t, qw), BF16),
        compiler_params=_params("parallel", "arbitrary"),
        name="swa_attention",
    )(q, kd, kd, vd, vd, sink_rows)


def _s5_discretise_body(are_ref, aim_ref, logdt_ref, bre_ref, bim_ref, abr_ref, abi_ref, bbr_ref, bbi_ref):
    are, aim = are_ref[...], aim_ref[...]
    dt = jnp.exp(logdt_ref[...])
    mag = jnp.exp(are * dt)
    abar_re, abar_im = mag * jnp.cos(aim * dt), mag * jnp.sin(aim * dt)
    nr, ni = abar_re - 1.0, abar_im
    den = are * are + aim * aim
    coef_re = (nr * are + ni * aim) / den
    coef_im = (ni * are - nr * aim) / den
    bre, bim = bre_ref[...], bim_ref[...]
    abr_ref[...] = abar_re
    abi_ref[...] = abar_im
    bbr_ref[...] = coef_re * bre - coef_im * bim
    bbi_ref[...] = coef_re * bim + coef_im * bre


def _s5_discretise(a_re, a_im, log_dt, b_re, b_im):
    g, p = a_re.shape
    c = b_re.shape[-1]
    col = lambda v: v.reshape(g * p, 1)
    logdt_col = jnp.broadcast_to(log_dt[:, None], (g, p)).reshape(g * p, 1)
    outs = pl.pallas_call(
        _s5_discretise_body,
        out_shape=[jax.ShapeDtypeStruct((g * p, 1), F32)] * 2 + [jax.ShapeDtypeStruct((g * p, c), F32)] * 2,
        name="s5_discretise",
    )(col(a_re), col(a_im), logdt_col, b_re.reshape(g * p, c), b_im.reshape(g * p, c))
    abr, abi, bbr, bbi = outs
    return abr.reshape(g, p), abi.reshape(g, p), bbr.reshape(g, p, c), bbi.reshape(g, p, c)


def _cmul(ar, ai, br, bi):
    return ar * br - ai * bi, ar * bi + ai * br


def _s5_body(n_tiles, steps, u_ref, wb_ref, wc_ref, are_ref, aim_ref, d_ref, wglu_ref, bglu_ref,
             o_ref, us_ref, ys_ref, carry_ref, pw_ref, p8_ref):
    blk = pl.program_id(1)
    sw = are_ref.shape[-1]
    cw = wc_ref.shape[-1]
    nslab = us_ref.shape[0]
    nsub = SUBLANES

    @pl.when(blk == 0)
    def _():
        carry_ref[...] = jnp.zeros_like(carry_ref)
        for kt in range(n_tiles):
            ar, ai = are_ref[kt], aim_ref[kt]
            pr, pi = ar, ai
            for j in range(steps):
                pw_ref[kt, j:j + 1, :sw] = pr
                pw_ref[kt, j:j + 1, sw:] = pi
                if j + 1 < steps:
                    pr, pi = _cmul(pr, pi, ar, ai)
            qr, qi = pr, pi
            for s in range(nsub):
                p8_ref[kt, s:s + 1, :sw] = qr
                p8_ref[kt, s:s + 1, sw:] = qi
                if s + 1 < nsub:
                    qr, qi = _cmul(qr, qi, pr, pi)

    for s in range(nsub):
        for c in range(nslab):
            us_ref[c, pl.ds(s, steps, stride=nsub), :] = u_ref[s * steps:(s + 1) * steps, c * LANES:(c + 1) * LANES]

    def u_cols(c0, n):
        return jnp.concatenate([us_ref[c0 + i] for i in range(n)], axis=1)

    def input_drive(kt):
        ub = u_cols(kt * (cw // LANES), cw // LANES).astype(BF16)
        return jnp.dot(ub, wb_ref[kt], preferred_element_type=F32)

    srow = lax.broadcasted_iota(jnp.int32, (nsub, sw), 0)

    def scan_tile(kt, bu):
        ar = jnp.broadcast_to(are_ref[kt], (nsub, sw))
        ai = jnp.broadcast_to(aim_ref[kt], (nsub, sw))
        hr, hi = jnp.zeros((nsub, sw), F32), jnp.zeros((nsub, sw), F32)
        zero_start = []
        for j in range(steps):
            r0 = j * nsub
            pr, pi = _cmul(ar, ai, hr, hi)
            hr, hi = pr + bu[r0:r0 + nsub, :sw], pi + bu[r0:r0 + nsub, sw:]
            zero_start.append((hr, hi))
        xr, xi = hr, hi
        for d in (1, 2, 4):
            pr, pi = p8_ref[kt, d - 1:d, :sw], p8_ref[kt, d - 1:d, sw:]
            sr = jnp.where(srow >= d, pltpu.roll(xr, d, axis=0), 0.0)
            si = jnp.where(srow >= d, pltpu.roll(xi, d, axis=0), 0.0)
            qr, qi = _cmul(pr, pi, sr, si)
            xr, xi = xr + qr, xi + qi
        cr, ci = carry_ref[kt, :, :sw], carry_ref[kt, :, sw:]
        qr, qi = _cmul(p8_ref[kt, :, :sw], p8_ref[kt, :, sw:], cr, ci)
        xr, xi = xr + qr, xi + qi
        init_r = jnp.where(srow == 0, cr, pltpu.roll(xr, 1, axis=0))
        init_i = jnp.where(srow == 0, ci, pltpu.roll(xi, 1, axis=0))
        carry_ref[kt, :, :sw] = jnp.broadcast_to(xr[nsub - 1:nsub], (nsub, sw))
        carry_ref[kt, :, sw:] = jnp.broadcast_to(xi[nsub - 1:nsub], (nsub, sw))
        rows = []
        for j in range(steps):
            pr, pi = _cmul(pw_ref[kt, j:j + 1, :sw], pw_ref[kt, j:j + 1, sw:], init_r, init_i)
            rows.append(jnp.concatenate([zero_start[j][0] + pr, zero_start[j][1] + pi], axis=1))
        return jnp.concatenate(rows, axis=0)

    bus = [input_drive(kt) for kt in range(n_tiles)]
    ys = [jnp.dot(scan_tile(kt, bus[kt]).astype(BF16), wc_ref[kt], preferred_element_type=F32)
          for kt in range(n_tiles)]
    y = jnp.concatenate(ys, axis=1) + d_ref[...] * u_cols(0, nslab)
    gl = jax.nn.gelu(y)
    gate = jnp.dot(gl.astype(BF16), wglu_ref[...], preferred_element_type=F32) + bglu_ref[...]
    out = gl * jax.nn.sigmoid(gate)
    for c in range(nslab):
        ys_ref[c] = out[:, c * LANES:(c + 1) * LANES]
    for s in range(nsub):
        for c in range(nslab):
            o_ref[s * steps:(s + 1) * steps, c * LANES:(c + 1) * LANES] = (
                ys_ref[c, pl.ds(s, steps, stride=nsub), :].astype(BF16))


def _s5(u, wb, wc, a_re_t, a_im_t, d_row, w_glu_all, e, b_glu, *, batch, seq, steps=32):
    t, width = u.shape
    n_tiles, cw, sw2 = wb.shape
    rows_per_block = steps * SUBLANES
    nblk = seq // rows_per_block
    const3 = lambda b, k: (0, 0, 0)
    const2 = lambda b, k: (0, 0)
    row = lambda b, k: (b * nblk + k, 0)
    return pl.pallas_call(
        functools.partial(_s5_body, n_tiles, steps),
        grid=(batch, nblk),
        in_specs=[
            pl.BlockSpec((rows_per_block, width), row),
            pl.BlockSpec(wb.shape, const3),
            pl.BlockSpec(wc.shape, const3),
            pl.BlockSpec(a_re_t.shape, const3),
            pl.BlockSpec(a_im_t.shape, const3),
            pl.BlockSpec(d_row.shape, const2),
            pl.BlockSpec((None,) + w_glu_all.shape[1:], lambda b, k: (e, 0, 0)),
            pl.BlockSpec(b_glu.shape, const2),
        ],
        out_specs=pl.BlockSpec((rows_per_block, width), row),
        out_shape=jax.ShapeDtypeStruct((t, width), BF16),
        scratch_shapes=[
            pltpu.VMEM((width // LANES, rows_per_block, LANES), F32),
            pltpu.VMEM((width // LANES, rows_per_block, LANES), F32),
            pltpu.VMEM((n_tiles, SUBLANES, sw2), F32),
            pltpu.VMEM((n_tiles, steps, sw2), F32),
            pltpu.VMEM((n_tiles, SUBLANES, sw2), F32),
        ],
        compiler_params=_params("parallel", "arbitrary"),
        name="s5_scan_glu",
    )(u, wb, wc, a_re_t, a_im_t, d_row, w_glu_all, b_glu)


HALO_ROWS = 16


def _odd_inproj_body(n_plain, tiles_per_seq, x_ref, xh_ref, g_ref, w_ref, wdt_ref, cw_ref, cb_ref,
                     o_ref, dt_ref, xn_ref):
    i, j = pl.program_id(0), pl.program_id(1)

    @pl.when(j == 0)
    def _():
        g = g_ref[...]
        xn_ref[0:HALO_ROWS, :] = _rms(xh_ref[...], g).astype(BF16)
        xn_ref[HALO_ROWS:, :] = _rms(x_ref[...], g).astype(BF16)
        dt_ref[...] = jnp.dot(xn_ref[HALO_ROWS:, :], wdt_ref[...], preferred_element_type=F32)

    @pl.when(j < n_plain)
    def _():
        o_ref[...] = jnp.dot(xn_ref[HALO_ROWS:, :], w_ref[...], preferred_element_type=F32)

    @pl.when(j >= n_plain)
    def _():
        p = jnp.dot(xn_ref[...], w_ref[...], preferred_element_type=F32)
        seq_start = (i % tiles_per_seq) == 0
        halo = jnp.where(seq_start, 0.0, p[:HALO_ROWS])
        ext = jnp.concatenate([halo, p[HALO_ROWS:]], axis=0)
        acc = cb_ref[...] + cw_ref[M_CONV - 1:M_CONV, :] * p[HALO_ROWS:]
        for k in range(1, M_CONV):
            acc = acc + cw_ref[M_CONV - 1 - k:M_CONV - k, :] * ext[HALO_ROWS - k:ext.shape[0] - k]
        o_ref[...] = jax.nn.silu(acc)


def _odd_inproj(x, g, w, layer, n, wdt, conv_w, conv_b, *, seq, n_plain_cols, tm=1024, tn=1024):
    t, d = x.shape
    assert seq % tm == 0 and n_plain_cols % tn == 0 and n % tn == 0
    n_plain = n_plain_cols // tn
    halo_blocks = tm // HALO_ROWS
    conv_col = lambda i, j: (0, jnp.maximum(j - n_plain, 0))
    return pl.pallas_call(
        functools.partial(_odd_inproj_body, n_plain, seq // tm),
        grid=(t // tm, n // tn),
        in_specs=[
            pl.BlockSpec((tm, d), lambda i, j: (i, 0)),
            pl.BlockSpec((HALO_ROWS, d), lambda i, j: (jnp.maximum(i * halo_blocks - 1, 0), 0)),
            pl.BlockSpec((1, d), lambda i, j: (0, 0)),
            pl.BlockSpec((None, d, tn), lambda i, j: (layer, 0, j)),
            pl.BlockSpec(wdt.shape, lambda i, j: (0, 0)),
            pl.BlockSpec((M_CONV, tn), conv_col),
            pl.BlockSpec((1, tn), conv_col),
        ],
        out_specs=[
            pl.BlockSpec((tm, tn), lambda i, j: (i, j)),
            pl.BlockSpec((tm, wdt.shape[-1]), lambda i, j: (i, 0)),
        ],
        out_shape=[jax.ShapeDtypeStruct((t, n), F32), jax.ShapeDtypeStruct((t, wdt.shape[-1]), F32)],
        scratch_shapes=[pltpu.VMEM((HALO_ROWS + tm, d), BF16)],
        compiler_params=_params("parallel", "arbitrary"),
        name="odd_inproj_conv",
    )(x, x, g, w, wdt, conv_w, conv_b)


def _ssd_body(n_groups, z_ref, xs_ref, bm_ref, cm_ref, dt_ref, dtb_ref, alog_ref, dsk_ref, ng_ref,
              o_ref, state_ref, y_ref):
    c = pl.program_id(1)
    L = M_CHUNK
    inner = xs_ref.shape[-1]
    gw = bm_ref.shape[-1] // n_groups
    hw = inner // n_groups

    @pl.when(c == 0)
    def _():
        state_ref[...] = jnp.zeros_like(state_ref)

    dt = jax.nn.softplus(dt_ref[...] + dtb_ref[...])
    a = -jnp.exp(alog_ref[...])
    acum = a * dt
    rowi = lax.broadcasted_iota(jnp.int32, acum.shape, 0)
    k = 1
    while k < L:
        acum = acum + jnp.where(rowi >= k, pltpu.roll(acum, k, axis=0), 0.0)
        k *= 2
    src = acum - jnp.log(dt)
    src_t = src.T
    a_last = acum[L - 1:L, :]

    li = lax.broadcasted_iota(jnp.int32, (L, L), 0)
    si = lax.broadcasted_iota(jnp.int32, (L, L), 1)
    causal = li >= si
    lane = lax.broadcasted_iota(jnp.int32, (L, LANES), 1)
    lower = lane < M_HEAD_DIM
    heads_per_group = hw // M_HEAD_DIM

    for g in range(n_groups):
        cgb = cm_ref[:, g * gw:(g + 1) * gw].astype(BF16)
        bgb = bm_ref[:, g * gw:(g + 1) * gw].astype(BF16)
        cb = lax.dot_general(cgb, bgb, (((1,), (1,)), ((), ())), preferred_element_type=F32)
        prev = state_ref[:, g * hw:(g + 1) * hw]
        y_off = jnp.dot(cgb, prev.astype(BF16), preferred_element_type=F32)
        xd_parts = []
        for pr in range(heads_per_group // 2):
            h0 = g * heads_per_group + 2 * pr
            col0 = g * hw + pr * LANES
            xp = xs_ref[:, col0:col0 + LANES]
            ms = []
            for h in (h0, h0 + 1):
                seg = acum[:, h:h + 1] - src_t[h:h + 1, :]
                ms.append((cb * jnp.exp(jnp.where(causal, seg, -jnp.inf))).astype(BF16))
            lhs = jnp.concatenate(ms, axis=1)
            xpb = xp.astype(BF16)
            rhs = jnp.concatenate([jnp.where(lower, xpb, jnp.zeros_like(xpb)),
                                   jnp.where(lower, jnp.zeros_like(xpb), xpb)], axis=0)
            y_diag = jnp.dot(lhs, rhs, preferred_element_type=F32)
            al = jnp.where(lower, acum[:, h0:h0 + 1], acum[:, h0 + 1:h0 + 2])
            srcl = jnp.where(lower, src[:, h0:h0 + 1], src[:, h0 + 1:h0 + 2])
            alast = jnp.where(lower[0:1], a_last[:, h0:h0 + 1], a_last[:, h0 + 1:h0 + 2])
            y_ref[:, col0:col0 + LANES] = (y_diag + jnp.exp(al) * y_off[:, pr * LANES:(pr + 1) * LANES]
                                           + dsk_ref[:, col0:col0 + LANES] * xp)
            xd_parts.append((xp * jnp.exp(alast - srcl)).astype(BF16))
            state_ref[:, col0:col0 + LANES] = state_ref[:, col0:col0 + LANES] * jnp.exp(alast)
        xd = jnp.concatenate(xd_parts, axis=1)
        state_ref[:, g * hw:(g + 1) * hw] += lax.dot_general(
            bgb, xd, (((0,), (0,)), ((), ())), preferred_element_type=F32)

    for g in range(n_groups):
        sl = slice(g * hw, (g + 1) * hw)
        y = y_ref[:, sl] * jax.nn.silu(z_ref[:, sl])
        y = y * lax.rsqrt(jnp.mean(y * y, axis=-1, keepdims=True) + NORM_EPS)
        o_ref[:, sl] = (y * ng_ref[:, sl]).astype(BF16)


def _ssd(zxbc, dt_raw, dt_bias, a_log, d_row, norm_g, *, batch, seq, inner, n_groups):
    t = zxbc.shape[0]
    nchunks = seq // M_CHUNK
    gn = n_groups * M_STATE
    row = lambda b, c: (b * nchunks + c, 0)
    const = lambda b, c: (0, 0)
    return pl.pallas_call(
        functools.partial(_ssd_body, n_groups),
        grid=(batch, nchunks),
        in_specs=[
            pl.BlockSpec((M_CHUNK, inner), row),
            pl.BlockSpec((M_CHUNK, inner), lambda b, c: (b * nchunks + c, 1)),
            pl.BlockSpec((M_CHUNK, gn), lambda b, c: (b * nchunks + c, 2 * inner // gn)),
            pl.BlockSpec((M_CHUNK, gn), lambda b, c: (b * nchunks + c, 2 * inner // gn + 1)),
            pl.BlockSpec((M_CHUNK, LANES), row),
            pl.BlockSpec((1, LANES), const),
            pl.BlockSpec((1, LANES), const),
            pl.BlockSpec((1, inner), const),
            pl.BlockSpec((1, inner), const),
        ],
        out_specs=pl.BlockSpec((M_CHUNK, inner), row),
        out_shape=jax.ShapeDtypeStruct((t, inner), BF16),
        scratch_shapes=[pltpu.VMEM((M_STATE, inner), F32),
                        pltpu.VMEM((M_CHUNK, inner), F32)],
        compiler_params=_params("parallel", "arbitrary"),
        name="ssd_chunk",
    )(zxbc, zxbc, zxbc, zxbc, dt_raw, dt_bias, a_log, d_row, norm_g)


def _dup_heads(w, n_heads):
    d = w.shape[0]
    w = w.reshape(d, n_heads, 1, HEAD_DIM)
    return jnp.broadcast_to(w, (d, n_heads, 2, HEAD_DIM)).reshape(d, n_heads * 2 * HEAD_DIM)


def _block_diag_tiles(blocks, n_tiles):
    g, r, c = blocks.shape
    gpt = g // n_tiles
    eye = jnp.eye(gpt, dtype=blocks.dtype)
    b = blocks.reshape(n_tiles, gpt, r, c)
    out = jnp.einsum('tgrc,gh->tgrhc', b, eye)
    return out.reshape(n_tiles, gpt * r, gpt * c)


def _rope_inv_freq_row():
    half = ROPE_DIM // 2
    inv_freq = jnp.exp(-math.log(ROPE_THETA) * jnp.arange(half, dtype=F32) * (2.0 / ROPE_DIM))
    lane = jnp.arange(LANES) % HEAD_DIM
    return jnp.where(lane < ROPE_DIM, inv_freq[lane % half], 0.0).reshape(1, LANES).astype(F32)


def _even_mixer(x, pos_b, norm_g, w_in, sinks, a_re, a_im, log_dt, b_re, b_im, c_re, c_im, d_skip, w_glu_all,
                b_glu, w_out_all, e, *, batch, seq):
    d = x.shape[-1]
    n_q = sinks.shape[0]
    a_width = n_q * HEAD_DIM
    groups, state = a_re.shape
    s5_width = groups * S5_GROUP
    kv_width = (w_in.shape[-1] - a_width - s5_width) // 2
    n_kv = kv_width // HEAD_DIM
    grp = n_q // n_kv
    assert grp % 2 == 0 and a_width % LANES == 0 and a_width == s5_width

    w_in = w_in.astype(BF16)
    wq = w_in[:, :a_width]
    wk = _dup_heads(w_in[:, a_width:a_width + kv_width], n_kv)
    wv = _dup_heads(w_in[:, a_width + kv_width:a_width + 2 * kv_width], n_kv)
    wu = w_in[:, a_width + 2 * kv_width:]
    w_all = jnp.concatenate([wq, wk, wv, wu], axis=1)
    q, kd, vd, u = _even_inproj(x, norm_g, w_all, pos_b, _rope_inv_freq_row(),
                                qw=a_width, kw=2 * kv_width, vw=2 * kv_width, uw=s5_width)

    sink_rows = jnp.repeat(sinks.astype(F32).reshape(n_q // 2, 2), ATTN_BLOCK, axis=1).reshape(n_q // 2, 1, 2 * ATTN_BLOCK)
    attn = _attention(q, kd, vd, sink_rows, batch=batch, seq=seq, pairs_per_kv=grp // 2)

    abr, abi, bbr, bbi = _s5_discretise(a_re.astype(F32), a_im.astype(F32), log_dt.astype(F32),
                                        b_re.astype(F32), b_im.astype(F32))
    n_tiles = groups // S5_GROUPS_PER_TILE
    to_in = lambda bb: _block_diag_tiles(jnp.swapaxes(bb, 1, 2), n_tiles)
    wb = jnp.concatenate([to_in(bbr), to_in(bbi)], axis=-1).astype(BF16)
    to_out = lambda cc: _block_diag_tiles(jnp.swapaxes(cc.astype(F32), 1, 2), n_tiles)
    wc = jnp.concatenate([to_out(c_re), -to_out(c_im)], axis=1).astype(BF16)
    a_re_t = abr.reshape(n_tiles, 1, -1)
    a_im_t = abi.reshape(n_tiles, 1, -1)

    ssm = _s5(u, wb, wc, a_re_t, a_im_t, d_skip.astype(F32).reshape(1, s5_width),
              w_glu_all, e, b_glu.astype(F32).reshape(1, s5_width), batch=batch, seq=seq)

    tm = 512
    terms = [
        (attn, pl.BlockSpec((tm, a_width), lambda i: (i, 0)),
         w_out_all, pl.BlockSpec((None, a_width, d), lambda i: (e, 0, 0))),
        (ssm, pl.BlockSpec((tm, s5_width), lambda i: (i, 0)),
         w_out_all, pl.BlockSpec((None, s5_width, d), lambda i: (e, 1, 0))),
    ]
    return _outproj(x, terms, tm=tm)


def _odd_mixer(x, norm_g, w_in_all, w_in_f32, conv_w, conv_b, dt_bias, a_log, d_skip, norm_out, w_out_all, o, *,
               batch, seq):
    d = x.shape[-1]
    inner = norm_out.shape[0]
    heads = a_log.shape[0]
    conv_dim = conv_w.shape[-1]
    n_groups = (conv_dim - inner) // (2 * M_STATE)
    assert heads <= LANES and inner // heads == M_HEAD_DIM
    pad = LANES - heads
    w_dt = jnp.pad(w_in_f32[:, inner + conv_dim:], ((0, 0), (0, pad))).astype(BF16)
    zxbc, dt_raw = _odd_inproj(x, norm_g, w_in_all, o, inner + conv_dim, w_dt, conv_w.astype(F32),
                               conv_b.astype(F32).reshape(1, conv_dim), seq=seq, n_plain_cols=inner)
    row = lambda v: jnp.pad(v.astype(F32), (0, pad)).reshape(1, LANES)
    y = _ssd(zxbc, dt_raw, row(dt_bias), row(a_log), jnp.repeat(d_skip.astype(F32), M_HEAD_DIM).reshape(1, inner),
             norm_out.astype(F32).reshape(1, inner), batch=batch, seq=seq, inner=inner, n_groups=n_groups)
    tm = 512
    return _outproj(x, [(y, pl.BlockSpec((tm, inner), lambda i: (i, 0)),
                         w_out_all, pl.BlockSpec((None, inner, d), lambda i: (o, 0, 0)))], tm=tm)


def kernel(x, positions, norm_ffn1, ffn1_gate, ffn1_up, ffn1_down, norm_mix, norm_ffn2, ffn2_gate, ffn2_up,
           ffn2_down, ev_w_in, ev_sinks, s5_a_re, s5_a_im, s5_log_dt, s5_b_re, s5_b_im, s5_c_re, s5_c_im, s5_d,
           s5_w_glu, s5_b_glu, ev_w_out, m_w_in, m_conv_w, m_conv_b, m_dt_bias, m_a_log, m_d, m_norm, m_w_out,
           final_norm):
    batch, seq, d = x.shape
    depth = norm_ffn1.shape[0]
    t = batch * seq
    h = x.reshape(t, d).astype(F32)
    pos_b = jnp.broadcast_to(positions.reshape(t, 1).astype(F32), (t, LANES))
    g1, g2, gm = (v.astype(F32).reshape(depth, 1, d) for v in (norm_ffn1, norm_ffn2, norm_mix))
    w_glu_all, ev_w_out_all, m_w_in_all, m_w_out_all = (w.astype(BF16) for w in (s5_w_glu, ev_w_out, m_w_in, m_w_out))
    ffn_f32 = [(ffn1_gate, ffn1_up, ffn1_down), (ffn2_gate, ffn2_up, ffn2_down)]
    w_cur = tuple(w[0].astype(BF16) for w in ffn_f32[0])
    for layer in range(depth):
        h, w_cur = _ffn(h, g1, *w_cur, layer, next_f32=ffn_f32[1] + (layer,))
        if layer % 2 == 0:
            e = layer // 2
            h = _even_mixer(h, pos_b, gm[layer], ev_w_in[e], ev_sinks[e], s5_a_re[e], s5_a_im[e], s5_log_dt[e],
                            s5_b_re[e], s5_b_im[e], s5_c_re[e], s5_c_im[e], s5_d[e].reshape(-1), w_glu_all,
                            s5_b_glu[e], ev_w_out_all, e, batch=batch, seq=seq)
        else:
            o = layer // 2
            h = _odd_mixer(h, gm[layer], m_w_in_all, m_w_in[o], m_conv_w[o], m_conv_b[o], m_dt_bias[o], m_a_log[o],
                           m_d[o], m_norm[o], m_w_out_all, o, batch=batch, seq=seq)
        if layer == depth - 1:
            h, _ = _ffn(h, g2, *w_cur, layer, final_g=final_norm.astype(F32).reshape(1, d))
        else:
            h, w_cur = _ffn(h, g2, *w_cur, layer, next_f32=ffn_f32[0] + (layer + 1,))
    return h.reshape(batch, seq, d).astype(x.dtype)
```

```python
import functools
import math

import jax
import jax.numpy as jnp
from jax import lax
from jax.experimental import pallas as pl
from jax.experimental.pallas import tpu as pltpu

F32 = jnp.float32
BF16 = jnp.bfloat16

NORM_EPS = 1e-5
NEG_INF = -1e30
LANES = 128
SUBLANES = 8
VMEM_LIMIT_BYTES = 60 * 1024 * 1024

HEAD_DIM = 64
ATTN_BLOCK = 128
ROPE_THETA = 500000.0
ROPE_DIM = HEAD_DIM // 4
ATTN_BLOCKS_PER_STEP = 4
S5_GROUP = 16
S5_STATE = 64
S5_GROUPS_PER_TILE = 16
M_HEAD_DIM = 64
M_STATE = 128
M_CHUNK = 128
M_CONV = 4


def _params(*sem):
    return pltpu.CompilerParams(dimension_semantics=sem, vmem_limit_bytes=VMEM_LIMIT_BYTES)


def _rms(x, g):
    return x * lax.rsqrt(jnp.mean(x * x, axis=-1, keepdims=True) + NORM_EPS) * g


def _ffn_body(final, cast_next, x_ref, g_ref, wg_ref, wu_ref, wd_ref, *rest):
    xn_ref = rest[-1]
    o_ref = rest[-5] if cast_next else rest[-2]
    j = pl.program_id(1)
    if cast_next:
        n_in = 1 if final else 0
        for src, dst in zip(rest[n_in:n_in + 3], rest[-4:-1]):
            dst[...] = src[...].astype(BF16)

    @pl.when(j == 0)
    def _():
        x = x_ref[...]
        xn_ref[...] = _rms(x, g_ref[...]).astype(BF16)
        o_ref[...] = x

    xn = xn_ref[...]
    gate = jnp.dot(xn, wg_ref[...], preferred_element_type=F32)
    up = jnp.dot(xn, wu_ref[...], preferred_element_type=F32)
    h = (jax.nn.silu(gate) * up * 0.5).astype(BF16)
    o_ref[...] += jnp.dot(h, wd_ref[...], preferred_element_type=F32)

    if final:
        @pl.when(j == pl.num_programs(1) - 1)
        def _():
            o_ref[...] = _rms(o_ref[...], rest[0][...])


def _ffn(x, g, wg, wu, wd, layer, final_g=None, next_f32=None, *, tm=1024, tf=512):
    t, d = x.shape
    dff = wg.shape[-1]
    ni, nj = t // tm, dff // tf
    in_specs = [
        pl.BlockSpec((tm, d), lambda i, j: (i, 0)),
        pl.BlockSpec((None, 1, d), lambda i, j: (layer, 0, 0)),
        pl.BlockSpec((d, tf), lambda i, j: (0, j)),
        pl.BlockSpec((d, tf), lambda i, j: (0, j)),
        pl.BlockSpec((tf, d), lambda i, j: (j, 0)),
    ]
    args = [x, g, wg, wu, wd]
    out_specs = [pl.BlockSpec((tm, d), lambda i, j: (i, 0))]
    out_shape = [jax.ShapeDtypeStruct((t, d), F32)]
    if final_g is not None:
        in_specs.append(pl.BlockSpec((1, d), lambda i, j: (0, 0)))
        args.append(final_g)
    if next_f32 is not None:
        ng, nu, nd, nl = next_f32
        assert d % ni == 0 and dff % nj == 0
        up_blk, dn_blk = (d // ni, dff // nj), (dff // nj, d // ni)
        in_specs += [pl.BlockSpec((None,) + up_blk, lambda i, j: (nl, i, j)),
                     pl.BlockSpec((None,) + up_blk, lambda i, j: (nl, i, j)),
                     pl.BlockSpec((None,) + dn_blk, lambda i, j: (nl, j, i))]
        args += [ng, nu, nd]
        out_specs += [pl.BlockSpec(up_blk, lambda i, j: (i, j)), pl.BlockSpec(up_blk, lambda i, j: (i, j)),
                      pl.BlockSpec(dn_blk, lambda i, j: (j, i))]
        out_shape += [jax.ShapeDtypeStruct((d, dff), BF16)] * 2 + [jax.ShapeDtypeStruct((dff, d), BF16)]
    outs = pl.pallas_call(
        functools.partial(_ffn_body, final_g is not None, next_f32 is not None),
        grid=(ni, nj),
        in_specs=in_specs,
        out_specs=out_specs,
        out_shape=out_shape,
        scratch_shapes=[pltpu.VMEM((tm, d), BF16)],
        compiler_params=_params("parallel", "arbitrary"),
        name="swiglu_ffn",
    )(*args)
    return outs[0], tuple(outs[1:])


def _outproj_body(n_terms, x_ref, *refs):
    o_ref = refs[-1]
    acc = x_ref[...]
    for i in range(n_terms):
        acc = acc + jnp.dot(refs[2 * i][...], refs[2 * i + 1][...], preferred_element_type=F32)
    o_ref[...] = acc


def _outproj(x, terms, *, tm):
    t, d = x.shape
    in_specs = [pl.BlockSpec((tm, d), lambda i: (i, 0))]
    args = [x]
    for a, a_spec, w, w_spec in terms:
        in_specs += [a_spec, w_spec]
        args += [a, w]
    return pl.pallas_call(
        functools.partial(_outproj_body, len(terms)),
        grid=(t // tm,),
        in_specs=in_specs,
        out_specs=pl.BlockSpec((tm, d), lambda i: (i, 0)),
        out_shape=jax.ShapeDtypeStruct((t, d), F32),
        compiler_params=_params("parallel"),
        name="residual_outproj",
    )(*args)


def _even_inproj_body(n_rot, x_ref, g_ref, w_ref, pos_ref, invf_ref, q_ref, k_ref, v_ref, u_ref, xn_ref):
    xn_ref[...] = _rms(x_ref[...], g_ref[...]).astype(BF16)
    ang = pos_ref[...] * invf_ref[...]
    cos, sin = jnp.cos(ang), jnp.sin(ang)
    lane = lax.broadcasted_iota(jnp.int32, ang.shape, 1) % HEAD_DIM
    half = ROPE_DIM // 2
    sin_from_lower = jnp.where((lane >= half) & (lane < ROPE_DIM), sin, 0.0)
    sin_from_upper = jnp.where(lane < half, sin, 0.0)

    cw = 512
    qw, kw, vw = q_ref.shape[-1], k_ref.shape[-1], v_ref.shape[-1]
    for c0 in range(0, w_ref.shape[-1], cw):
        p = jnp.dot(xn_ref[...], w_ref[:, c0:c0 + cw], preferred_element_type=F32)
        for l0 in range(0, cw, LANES):
            col = c0 + l0
            t = p[:, l0:l0 + LANES]
            if col < n_rot:
                t = (t * cos + pltpu.roll(t, half, axis=1) * sin_from_lower
                     - pltpu.roll(t, LANES - half, axis=1) * sin_from_upper)
            if col < qw:
                q_ref[:, col:col + LANES] = t.astype(BF16)
            elif col < qw + kw:
                k_ref[:, col - qw:col - qw + LANES] = t.astype(BF16)
            elif col < qw + kw + vw:
                v_ref[:, col - qw - kw:col - qw - kw + LANES] = t.astype(BF16)
            else:
                c = col - qw - kw - vw
                u_ref[:, c:c + LANES] = t


def _even_inproj(x, g, w, pos_b, invf, *, qw, kw, vw, uw, tm=512):
    t, d = x.shape
    return pl.pallas_call(
        functools.partial(_even_inproj_body, qw + kw),
        grid=(t // tm,),
        in_specs=[
            pl.BlockSpec((tm, d), lambda i: (i, 0)),
            pl.BlockSpec((1, d), lambda i: (0, 0)),
            pl.BlockSpec(w.shape, lambda i: (0, 0)),
            pl.BlockSpec((tm, LANES), lambda i: (i, 0)),
            pl.BlockSpec((1, LANES), lambda i: (0, 0)),
        ],
        out_specs=[
            pl.BlockSpec((tm, qw), lambda i: (i, 0)),
            pl.BlockSpec((tm, kw), lambda i: (i, 0)),
            pl.BlockSpec((tm, vw), lambda i: (i, 0)),
            pl.BlockSpec((tm, uw), lambda i: (i, 0)),
        ],
        out_shape=[
            jax.ShapeDtypeStruct((t, qw), BF16),
            jax.ShapeDtypeStruct((t, kw), BF16),
            jax.ShapeDtypeStruct((t, vw), BF16),
            jax.ShapeDtypeStruct((t, uw), F32),
        ],
        scratch_shapes=[pltpu.VMEM((tm, d), BF16)],
        compiler_params=_params("parallel"),
        name="even_inproj_rotary",
    )(x, g, w, pos_b, invf)


def _attn_body(pairs_per_kv, q_ref, kc_ref, kp_ref, vc_ref, vp_ref, sink_ref, o_ref):
    m_step = pl.program_id(1)
    blk = ATTN_BLOCK
    kk = jnp.concatenate([kp_ref[...], kc_ref[...]], axis=0)
    vv = jnp.concatenate([vp_ref[...], vc_ref[...]], axis=0)
    lower = lax.broadcasted_iota(jnp.int32, (blk, LANES), 1) < HEAD_DIM
    key = lax.broadcasted_iota(jnp.int32, (2 * blk, 2 * blk), 0)
    qrow = lax.broadcasted_iota(jnp.int32, (2 * blk, 2 * blk), 1) % blk
    band = (key > qrow) & (key <= qrow + blk)
    upper_rows = lax.broadcasted_iota(jnp.int32, (LANES, blk), 0) >= HEAD_DIM
    n_pairs = q_ref.shape[-1] // LANES
    scale = 1.0 / math.sqrt(HEAD_DIM)

    def scores(t, c):
        hk = c // pairs_per_kv
        qc = q_ref[t * blk:(t + 1) * blk, c * LANES:(c + 1) * LANES] * scale
        q2 = jnp.concatenate([jnp.where(lower, qc, jnp.zeros_like(qc)),
                              jnp.where(lower, jnp.zeros_like(qc), qc)], axis=0)
        kh = kk[t * blk:t * blk + 2 * blk, hk * LANES:(hk + 1) * LANES]
        return lax.dot_general(kh, q2, (((1,), (1,)), ((), ())), preferred_element_type=F32)

    def finish(t, c, s):
        hk = c // pairs_per_kv
        first = (m_step == 0) if t == 0 else False
        valid = band & ((key >= blk) | jnp.logical_not(first))
        s = jnp.where(valid, s, NEG_INF)
        sink = sink_ref[c]
        mx = jnp.maximum(jnp.max(s, axis=0, keepdims=True), sink)
        p = jnp.exp(s - mx)
        den = jnp.sum(p, axis=0, keepdims=True) + jnp.exp(sink - mx)
        p = (p * (1.0 / den)).astype(BF16)
        vh = vv[t * blk:t * blk + 2 * blk, hk * LANES:(hk + 1) * LANES]
        r = lax.dot_general(vh, p, (((0,), (0,)), ((), ())), preferred_element_type=F32)
        o_t = jnp.where(upper_rows, r[:, blk:], r[:, :blk])
        o_ref[t * blk:(t + 1) * blk, c * LANES:(c + 1) * LANES] = o_t.T.astype(BF16)

    units = [(t, c) for t in range(ATTN_BLOCKS_PER_STEP) for c in range(n_pairs)]
    ahead = 4
    pending = [scores(*u) for u in units[:ahead]]
    for n, (t, c) in enumerate(units):
        if n + ahead < len(units):
            pending.append(scores(*units[n + ahead]))
        finish(t, c, pending.pop(0))


def _attention(q, kd, vd, sink_rows, *, batch, seq, pairs_per_kv):
    t, qw = q.shape
    kw = kd.shape[-1]
    rows = ATTN_BLOCK * ATTN_BLOCKS_PER_STEP
    steps = seq // rows
    bps = ATTN_BLOCKS_PER_STEP
    nblk = seq // ATTN_BLOCK

    def cur(b, m):
        return (b * steps + m, 0)

    def prev(b, m):
        return (b * nblk + jnp.maximum(m * bps - 1, 0), 0)

    return pl.pallas_call(
        functools.partial(_attn_body, pairs_per_kv),
        grid=(batch, steps),
        in_specs=[
            pl.BlockSpec((rows, qw), cur),
            pl.BlockSpec((rows, kw), cur),
            pl.BlockSpec((ATTN_BLOCK, kw), prev),
            pl.BlockSpec((rows, kw), cur),
            pl.BlockSpec((ATTN_BLOCK, kw), prev),
            pl.BlockSpec(sink_rows.shape, lambda b, m: (0, 0, 0)),
        ],
        out_specs=pl.BlockSpec((rows, qw), cur),
        out_shape=jax.ShapeDtypeStruct((t, qw), BF16),
        compiler_params=_params("parallel", "arbitrary"),
        name="swa_attention",
    )(q, kd, kd, vd, vd, sink_rows)


def _s5_discretise_body(are_ref, aim_ref, logdt_ref, bre_ref, bim_ref, abr_ref, abi_ref, bbr_ref, bbi_ref):
    are, aim = are_ref[...], aim_ref[...]
    dt = jnp.exp(logdt_ref[...])
    mag = jnp.exp(are * dt)
    abar_re, abar_im = mag * jnp.cos(aim * dt), mag * jnp.sin(aim * dt)
    nr, ni = abar_re - 1.0, abar_im
    den = are * are + aim * aim
    coef_re = (nr * are + ni * aim) / den
    coef_im = (ni * are - nr * aim) / den
    bre, bim = bre_ref[...], bim_ref[...]
    abr_ref[...] = abar_re
    abi_ref[...] = abar_im
    bbr_ref[...] = coef_re * bre - coef_im * bim
    bbi_ref[...] = coef_re * bim + coef_im * bre


def _s5_discretise(a_re, a_im, log_dt, b_re, b_im):
    g, p = a_re.shape
    c = b_re.shape[-1]
    col = lambda v: v.reshape(g * p, 1)
    logdt_col = jnp.broadcast_to(log_dt[:, None], (g, p)).reshape(g * p, 1)
    outs = pl.pallas_call(
        _s5_discretise_body,
        out_shape=[jax.ShapeDtypeStruct((g * p, 1), F32)] * 2 + [jax.ShapeDtypeStruct((g * p, c), F32)] * 2,
        name="s5_discretise",
    )(col(a_re), col(a_im), logdt_col, b_re.reshape(g * p, c), b_im.reshape(g * p, c))
    abr, abi, bbr, bbi = outs
    return abr.reshape(g, p), abi.reshape(g, p), bbr.reshape(g, p, c), bbi.reshape(g, p, c)


def _cmul(ar, ai, br, bi):
    return ar * br - ai * bi, ar * bi + ai * br


def _s5_body(n_tiles, steps, u_ref, wb_ref, wc_ref, are_ref, aim_ref, d_ref, wglu_ref, bglu_ref,
             o_ref, us_ref, ys_ref, carry_ref, pw_ref, p8_ref):
    blk = pl.program_id(1)
    sw = are_ref.shape[-1]
    cw = wc_ref.shape[-1]
    nslab = us_ref.shape[0]
    nsub = SUBLANES

    @pl.when(blk == 0)
    def _():
        carry_ref[...] = jnp.zeros_like(carry_ref)
        for kt in range(n_tiles):
            ar, ai = are_ref[kt], aim_ref[kt]
            pr, pi = ar, ai
            for j in range(steps):
                pw_ref[kt, j:j + 1, :sw] = pr
                pw_ref[kt, j:j + 1, sw:] = pi
                if j + 1 < steps:
                    pr, pi = _cmul(pr, pi, ar, ai)
            qr, qi = pr, pi
            for s in range(nsub):
                p8_ref[kt, s:s + 1, :sw] = qr
                p8_ref[kt, s:s + 1, sw:] = qi
                if s + 1 < nsub:
                    qr, qi = _cmul(qr, qi, pr, pi)

    for s in range(nsub):
        for c in range(nslab):
            us_ref[c, pl.ds(s, steps, stride=nsub), :] = u_ref[s * steps:(s + 1) * steps, c * LANES:(c + 1) * LANES]

    def u_cols(c0, n):
        return jnp.concatenate([us_ref[c0 + i] for i in range(n)], axis=1)

    def input_drive(kt):
        ub = u_cols(kt * (cw // LANES), cw // LANES).astype(BF16)
        return jnp.dot(ub, wb_ref[kt], preferred_element_type=F32)

    srow = lax.broadcasted_iota(jnp.int32, (nsub, sw), 0)

    def scan_tile(kt, bu):
        ar = jnp.broadcast_to(are_ref[kt], (nsub, sw))
        ai = jnp.broadcast_to(aim_ref[kt], (nsub, sw))
        hr, hi = jnp.zeros((nsub, sw), F32), jnp.zeros((nsub, sw), F32)
        zero_start = []
        for j in range(steps):
            r0 = j * nsub
            pr, pi = _cmul(ar, ai, hr, hi)
            hr, hi = pr + bu[r0:r0 + nsub, :sw], pi + bu[r0:r0 + nsub, sw:]
            zero_start.append((hr, hi))
        xr, xi = hr, hi
        for d in (1, 2, 4):
            pr, pi = p8_ref[kt, d - 1:d, :sw], p8_ref[kt, d - 1:d, sw:]
            sr = jnp.where(srow >= d, pltpu.roll(xr, d, axis=0), 0.0)
            si = jnp.where(srow >= d, pltpu.roll(xi, d, axis=0), 0.0)
            qr, qi = _cmul(pr, pi, sr, si)
            xr, xi = xr + qr, xi + qi
        cr, ci = carry_ref[kt, :, :sw], carry_ref[kt, :, sw:]
        qr, qi = _cmul(p8_ref[kt, :, :sw], p8_ref[kt, :, sw:], cr, ci)
        xr, xi = xr + qr, xi + qi
        init_r = jnp.where(srow == 0, cr, pltpu.roll(xr, 1, axis=0))
        init_i = jnp.where(srow == 0, ci, pltpu.roll(xi, 1, axis=0))
        carry_ref[kt, :, :sw] = jnp.broadcast_to(xr[nsub - 1:nsub], (nsub, sw))
        carry_ref[kt, :, sw:] = jnp.broadcast_to(xi[nsub - 1:nsub], (nsub, sw))
        rows = []
        for j in range(steps):
            pr, pi = _cmul(pw_ref[kt, j:j + 1, :sw], pw_ref[kt, j:j + 1, sw:], init_r, init_i)
            rows.append(jnp.concatenate([zero_start[j][0] + pr, zero_start[j][1] + pi], axis=1))
        return jnp.concatenate(rows, axis=0)

    bus = [input_drive(kt) for kt in range(n_tiles)]
    ys = [jnp.dot(scan_tile(kt, bus[kt]).astype(BF16), wc_ref[kt], preferred_element_type=F32)
          for kt in range(n_tiles)]
    y = jnp.concatenate(ys, axis=1) + d_ref[...] * u_cols(0, nslab)
    gl = jax.nn.gelu(y)
    gate = jnp.dot(gl.astype(BF16), wglu_ref[...], preferred_element_type=F32) + bglu_ref[...]
    out = gl * jax.nn.sigmoid(gate)
    for c in range(nslab):
        ys_ref[c] = out[:, c * LANES:(c + 1) * LANES]
    for s in range(nsub):
        for c in range(nslab):
            o_ref[s * steps:(s + 1) * steps, c * LANES:(c + 1) * LANES] = (
                ys_ref[c, pl.ds(s, steps, stride=nsub), :].astype(BF16))


def _s5(u, wb, wc, a_re_t, a_im_t, d_row, w_glu_all, e, b_glu, *, batch, seq, steps=32):
    t, width = u.shape
    n_tiles, cw, sw2 = wb.shape
    rows_per_block = steps * SUBLANES
    nblk = seq // rows_per_block
    const3 = lambda b, k: (0, 0, 0)
    const2 = lambda b, k: (0, 0)
    row = lambda b, k: (b * nblk + k, 0)
    return pl.pallas_call(
        functools.partial(_s5_body, n_tiles, steps),
        grid=(batch, nblk),
        in_specs=[
            pl.BlockSpec((rows_per_block, width), row),
            pl.BlockSpec(wb.shape, const3),
            pl.BlockSpec(wc.shape, const3),
            pl.BlockSpec(a_re_t.shape, const3),
            pl.BlockSpec(a_im_t.shape, const3),
            pl.BlockSpec(d_row.shape, const2),
            pl.BlockSpec((None,) + w_glu_all.shape[1:], lambda b, k: (e, 0, 0)),
            pl.BlockSpec(b_glu.shape, const2),
        ],
        out_specs=pl.BlockSpec((rows_per_block, width), row),
        out_shape=jax.ShapeDtypeStruct((t, width), BF16),
        scratch_shapes=[
            pltpu.VMEM((width // LANES, rows_per_block, LANES), F32),
            pltpu.VMEM((width // LANES, rows_per_block, LANES), F32),
            pltpu.VMEM((n_tiles, SUBLANES, sw2), F32),
            pltpu.VMEM((n_tiles, steps, sw2), F32),
            pltpu.VMEM((n_tiles, SUBLANES, sw2), F32),
        ],
        compiler_params=_params("parallel", "arbitrary"),
        name="s5_scan_glu",
    )(u, wb, wc, a_re_t, a_im_t, d_row, w_glu_all, b_glu)


HALO_ROWS = 16


def _odd_inproj_body(n_plain, tiles_per_seq, x_ref, xh_ref, g_ref, w_ref, wdt_ref, cw_ref, cb_ref,
                     o_ref, dt_ref, xn_ref):
    i, j = pl.program_id(0), pl.program_id(1)

    @pl.when(j == 0)
    def _():
        g = g_ref[...]
        xn_ref[0:HALO_ROWS, :] = _rms(xh_ref[...], g).astype(BF16)
        xn_ref[HALO_ROWS:, :] = _rms(x_ref[...], g).astype(BF16)
        dt_ref[...] = jnp.dot(xn_ref[HALO_ROWS:, :], wdt_ref[...], preferred_element_type=F32)

    @pl.when(j < n_plain)
    def _():
        o_ref[...] = jnp.dot(xn_ref[HALO_ROWS:, :], w_ref[...], preferred_element_type=F32)

    @pl.when(j >= n_plain)
    def _():
        p = jnp.dot(xn_ref[...], w_ref[...], preferred_element_type=F32)
        seq_start = (i % tiles_per_seq) == 0
        halo = jnp.where(seq_start, 0.0, p[:HALO_ROWS])
        ext = jnp.concatenate([halo, p[HALO_ROWS:]], axis=0)
        acc = cb_ref[...] + cw_ref[M_CONV - 1:M_CONV, :] * p[HALO_ROWS:]
        for k in range(1, M_CONV):
            acc = acc + cw_ref[M_CONV - 1 - k:M_CONV - k, :] * ext[HALO_ROWS - k:ext.shape[0] - k]
        o_ref[...] = jax.nn.silu(acc)


def _odd_inproj(x, g, w, layer, n, wdt, conv_w, conv_b, *, seq, n_plain_cols, tm=1024, tn=1024):
    t, d = x.shape
    assert seq % tm == 0 and n_plain_cols % tn == 0 and n % tn == 0
    n_plain = n_plain_cols // tn
    halo_blocks = tm // HALO_ROWS
    conv_col = lambda i, j: (0, jnp.maximum(j - n_plain, 0))
    return pl.pallas_call(
        functools.partial(_odd_inproj_body, n_plain, seq // tm),
        grid=(t // tm, n // tn),
        in_specs=[
            pl.BlockSpec((tm, d), lambda i, j: (i, 0)),
            pl.BlockSpec((HALO_ROWS, d), lambda i, j: (jnp.maximum(i * halo_blocks - 1, 0), 0)),
            pl.BlockSpec((1, d), lambda i, j: (0, 0)),
            pl.BlockSpec((None, d, tn), lambda i, j: (layer, 0, j)),
            pl.BlockSpec((None,) + wdt.shape[1:], lambda i, j: (layer, 0, 0)),
            pl.BlockSpec((M_CONV, tn), conv_col),
            pl.BlockSpec((1, tn), conv_col),
        ],
        out_specs=[
            pl.BlockSpec((tm, tn), lambda i, j: (i, j)),
            pl.BlockSpec((tm, wdt.shape[-1]), lambda i, j: (i, 0)),
        ],
        out_shape=[jax.ShapeDtypeStruct((t, n), F32), jax.ShapeDtypeStruct((t, wdt.shape[-1]), F32)],
        scratch_shapes=[pltpu.VMEM((HALO_ROWS + tm, d), BF16)],
        compiler_params=_params("parallel", "arbitrary"),
        name="odd_inproj_conv",
    )(x, x, g, w, wdt, conv_w, conv_b)


def _ssd_body(n_groups, z_ref, xs_ref, bm_ref, cm_ref, dt_ref, dtb_ref, alog_ref, dsk_ref, ng_ref,
              o_ref, state_ref, y_ref):
    c = pl.program_id(1)
    L = M_CHUNK
    inner = xs_ref.shape[-1]
    gw = bm_ref.shape[-1] // n_groups
    hw = inner // n_groups

    @pl.when(c == 0)
    def _():
        state_ref[...] = jnp.zeros_like(state_ref)

    dt = jax.nn.softplus(dt_ref[...] + dtb_ref[...])
    a = -jnp.exp(alog_ref[...])
    acum = a * dt
    rowi = lax.broadcasted_iota(jnp.int32, acum.shape, 0)
    k = 1
    while k < L:
        acum = acum + jnp.where(rowi >= k, pltpu.roll(acum, k, axis=0), 0.0)
        k *= 2
    src = acum - jnp.log(dt)
    src_t = src.T
    a_last = acum[L - 1:L, :]

    li = lax.broadcasted_iota(jnp.int32, (L, L), 0)
    si = lax.broadcasted_iota(jnp.int32, (L, L), 1)
    causal = li >= si
    lane = lax.broadcasted_iota(jnp.int32, (L, LANES), 1)
    lower = lane < M_HEAD_DIM
    heads_per_group = hw // M_HEAD_DIM

    for g in range(n_groups):
        cgb = cm_ref[:, g * gw:(g + 1) * gw].astype(BF16)
        bgb = bm_ref[:, g * gw:(g + 1) * gw].astype(BF16)
        cb = lax.dot_general(cgb, bgb, (((1,), (1,)), ((), ())), preferred_element_type=F32)
        prev = state_ref[:, g * hw:(g + 1) * hw]
        y_off = jnp.dot(cgb, prev.astype(BF16), preferred_element_type=F32)
        xd_parts = []
        for pr in range(heads_per_group // 2):
            h0 = g * heads_per_group + 2 * pr
            col0 = g * hw + pr * LANES
            xp = xs_ref[:, col0:col0 + LANES]
            ms = []
            for h in (h0, h0 + 1):
                seg = acum[:, h:h + 1] - src_t[h:h + 1, :]
                ms.append((cb * jnp.exp(jnp.where(causal, seg, -jnp.inf))).astype(BF16))
            lhs = jnp.concatenate(ms, axis=1)
            xpb = xp.astype(BF16)
            rhs = jnp.concatenate([jnp.where(lower, xpb, jnp.zeros_like(xpb)),
                                   jnp.where(lower, jnp.zeros_like(xpb), xpb)], axis=0)
            y_diag = jnp.dot(lhs, rhs, preferred_element_type=F32)
            al = jnp.where(lower, acum[:, h0:h0 + 1], acum[:, h0 + 1:h0 + 2])
            srcl = jnp.where(lower, src[:, h0:h0 + 1], src[:, h0 + 1:h0 + 2])
            alast = jnp.where(lower[0:1], a_last[:, h0:h0 + 1], a_last[:, h0 + 1:h0 + 2])
            y_ref[:, col0:col0 + LANES] = (y_diag + jnp.exp(al) * y_off[:, pr * LANES:(pr + 1) * LANES]
                                           + dsk_ref[:, col0:col0 + LANES] * xp)
            xd_parts.append((xp * jnp.exp(alast - srcl)).astype(BF16))
            state_ref[:, col0:col0 + LANES] = state_ref[:, col0:col0 + LANES] * jnp.exp(alast)
        xd = jnp.concatenate(xd_parts, axis=1)
        state_ref[:, g * hw:(g + 1) * hw] += lax.dot_general(
            bgb, xd, (((0,), (0,)), ((), ())), preferred_element_type=F32)

    for g in range(n_groups):
        sl = slice(g * hw, (g + 1) * hw)
        y = y_ref[:, sl] * jax.nn.silu(z_ref[:, sl])
        y = y * lax.rsqrt(jnp.mean(y * y, axis=-1, keepdims=True) + NORM_EPS)
        o_ref[:, sl] = (y * ng_ref[:, sl]).astype(BF16)


def _ssd(zxbc, dt_raw, dt_bias, a_log, d_row, norm_g, *, batch, seq, inner, n_groups):
    t = zxbc.shape[0]
    nchunks = seq // M_CHUNK
    gn = n_groups * M_STATE
    row = lambda b, c: (b * nchunks + c, 0)
    const = lambda b, c: (0, 0)
    return pl.pallas_call(
        functools.partial(_ssd_body, n_groups),
        grid=(batch, nchunks),
        in_specs=[
            pl.BlockSpec((M_CHUNK, inner), row),
            pl.BlockSpec((M_CHUNK, inner), lambda b, c: (b * nchunks + c, 1)),
            pl.BlockSpec((M_CHUNK, gn), lambda b, c: (b * nchunks + c, 2 * inner // gn)),
            pl.BlockSpec((M_CHUNK, gn), lambda b, c: (b * nchunks + c, 2 * inner // gn + 1)),
            pl.BlockSpec((M_CHUNK, LANES), row),
            pl.BlockSpec((1, LANES), const),
            pl.BlockSpec((1, LANES), const),
            pl.BlockSpec((1, inner), const),
            pl.BlockSpec((1, inner), const),
        ],
        out_specs=pl.BlockSpec((M_CHUNK, inner), row),
        out_shape=jax.ShapeDtypeStruct((t, inner), BF16),
        scratch_shapes=[pltpu.VMEM((M_STATE, inner), F32),
                        pltpu.VMEM((M_CHUNK, inner), F32)],
        compiler_params=_params("parallel", "arbitrary"),
        name="ssd_chunk",
    )(zxbc, zxbc, zxbc, zxbc, dt_raw, dt_bias, a_log, d_row, norm_g)


def _dup_heads(w, n_heads):
    d = w.shape[0]
    w = w.reshape(d, n_heads, 1, HEAD_DIM)
    return jnp.broadcast_to(w, (d, n_heads, 2, HEAD_DIM)).reshape(d, n_heads * 2 * HEAD_DIM)


def _block_diag_tiles(blocks, n_tiles):
    g, r, c = blocks.shape
    gpt = g // n_tiles
    eye = jnp.eye(gpt, dtype=blocks.dtype)
    b = blocks.reshape(n_tiles, gpt, r, c)
    out = jnp.einsum('tgrc,gh->tgrhc', b, eye)
    return out.reshape(n_tiles, gpt * r, gpt * c)


def _rope_inv_freq_row():
    half = ROPE_DIM // 2
    inv_freq = jnp.exp(-math.log(ROPE_THETA) * jnp.arange(half, dtype=F32) * (2.0 / ROPE_DIM))
    lane = jnp.arange(LANES) % HEAD_DIM
    return jnp.where(lane < ROPE_DIM, inv_freq[lane % half], 0.0).reshape(1, LANES).astype(F32)


def _even_mixer(x, pos_b, norm_g, w_in, sinks, a_re, a_im, log_dt, b_re, b_im, c_re, c_im, d_skip, w_glu_all,
                b_glu, w_out_all, e, *, batch, seq):
    d = x.shape[-1]
    n_q = sinks.shape[0]
    a_width = n_q * HEAD_DIM
    groups, state = a_re.shape
    s5_width = groups * S5_GROUP
    kv_width = (w_in.shape[-1] - a_width - s5_width) // 2
    n_kv = kv_width // HEAD_DIM
    grp = n_q // n_kv
    assert grp % 2 == 0 and a_width % LANES == 0 and a_width == s5_width

    w_in = w_in.astype(BF16)
    wq = w_in[:, :a_width]
    wk = _dup_heads(w_in[:, a_width:a_width + kv_width], n_kv)
    wv = _dup_heads(w_in[:, a_width + kv_width:a_width + 2 * kv_width], n_kv)
    wu = w_in[:, a_width + 2 * kv_width:]
    w_all = jnp.concatenate([wq, wk, wv, wu], axis=1)
    q, kd, vd, u = _even_inproj(x, norm_g, w_all, pos_b, _rope_inv_freq_row(),
                                qw=a_width, kw=2 * kv_width, vw=2 * kv_width, uw=s5_width)

    sink_rows = jnp.repeat(sinks.astype(F32).reshape(n_q // 2, 2), ATTN_BLOCK, axis=1).reshape(n_q // 2, 1, 2 * ATTN_BLOCK)
    attn = _attention(q, kd, vd, sink_rows, batch=batch, seq=seq, pairs_per_kv=grp // 2)

    abr, abi, bbr, bbi = _s5_discretise(a_re.astype(F32), a_im.astype(F32), log_dt.astype(F32),
                                        b_re.astype(F32), b_im.astype(F32))
    n_tiles = groups // S5_GROUPS_PER_TILE
    to_in = lambda bb: _block_diag_tiles(jnp.swapaxes(bb, 1, 2), n_tiles)
    wb = jnp.concatenate([to_in(bbr), to_in(bbi)], axis=-1).astype(BF16)
    to_out = lambda cc: _block_diag_tiles(jnp.swapaxes(cc.astype(F32), 1, 2), n_tiles)
    wc = jnp.concatenate([to_out(c_re), -to_out(c_im)], axis=1).astype(BF16)
    a_re_t = abr.reshape(n_tiles, 1, -1)
    a_im_t = abi.reshape(n_tiles, 1, -1)

    ssm = _s5(u, wb, wc, a_re_t, a_im_t, d_skip.astype(F32).reshape(1, s5_width),
              w_glu_all, e, b_glu.astype(F32).reshape(1, s5_width), batch=batch, seq=seq)

    tm = 512
    terms = [
        (attn, pl.BlockSpec((tm, a_width), lambda i: (i, 0)),
         w_out_all, pl.BlockSpec((None, a_width, d), lambda i: (e, 0, 0))),
        (ssm, pl.BlockSpec((tm, s5_width), lambda i: (i, 0)),
         w_out_all, pl.BlockSpec((None, s5_width, d), lambda i: (e, 1, 0))),
    ]
    return _outproj(x, terms, tm=tm)


def _odd_mixer(x, norm_g, w_main_all, w_dt_all, conv_w, conv_b, dt_bias, a_log, d_skip, norm_out, w_out_all, o, *,
               batch, seq):
    d = x.shape[-1]
    inner = norm_out.shape[0]
    heads = a_log.shape[0]
    conv_dim = conv_w.shape[-1]
    n_groups = (conv_dim - inner) // (2 * M_STATE)
    assert heads <= LANES and inner // heads == M_HEAD_DIM
    pad = LANES - heads
    zxbc, dt_raw = _odd_inproj(x, norm_g, w_main_all, o, inner + conv_dim, w_dt_all, conv_w.astype(F32),
                               conv_b.astype(F32).reshape(1, conv_dim), seq=seq, n_plain_cols=inner)
    row = lambda v: jnp.pad(v.astype(F32), (0, pad)).reshape(1, LANES)
    y = _ssd(zxbc, dt_raw, row(dt_bias), row(a_log), jnp.repeat(d_skip.astype(F32), M_HEAD_DIM).reshape(1, inner),
             norm_out.astype(F32).reshape(1, inner), batch=batch, seq=seq, inner=inner, n_groups=n_groups)
    tm = 512
    return _outproj(x, [(y, pl.BlockSpec((tm, inner), lambda i: (i, 0)),
                         w_out_all, pl.BlockSpec((None, inner, d), lambda i: (o, 0, 0)))], tm=tm)


def kernel(x, positions, norm_ffn1, ffn1_gate, ffn1_up, ffn1_down, norm_mix, norm_ffn2, ffn2_gate, ffn2_up,
           ffn2_down, ev_w_in, ev_sinks, s5_a_re, s5_a_im, s5_log_dt, s5_b_re, s5_b_im, s5_c_re, s5_c_im, s5_d,
           s5_w_glu, s5_b_glu, ev_w_out, m_w_in, m_conv_w, m_conv_b, m_dt_bias, m_a_log, m_d, m_norm, m_w_out,
           final_norm):
    batch, seq, d = x.shape
    depth = norm_ffn1.shape[0]
    t = batch * seq
    h = x.reshape(t, d).astype(F32)
    pos_b = jnp.broadcast_to(positions.reshape(t, 1).astype(F32), (t, LANES))
    g1, g2, gm = (v.astype(F32).reshape(depth, 1, d) for v in (norm_ffn1, norm_ffn2, norm_mix))
    w_glu_all, ev_w_out_all, m_w_out_all = (w.astype(BF16) for w in (s5_w_glu, ev_w_out, m_w_out))
    m_main = m_norm.shape[-1] + m_conv_w.shape[-1]
    m_w_main_all = m_w_in[:, :, :m_main].astype(BF16)
    m_w_dt_all = jnp.pad(m_w_in[:, :, m_main:], ((0, 0), (0, 0), (0, LANES - m_a_log.shape[-1]))).astype(BF16)
    ffn_f32 = [(ffn1_gate, ffn1_up, ffn1_down), (ffn2_gate, ffn2_up, ffn2_down)]
    w_cur = tuple(w[0].astype(BF16) for w in ffn_f32[0])
    for layer in range(depth):
        h, w_cur = _ffn(h, g1, *w_cur, layer, next_f32=ffn_f32[1] + (layer,))
        if layer % 2 == 0:
            e = layer // 2
            h = _even_mixer(h, pos_b, gm[layer], ev_w_in[e], ev_sinks[e], s5_a_re[e], s5_a_im[e], s5_log_dt[e],
                            s5_b_re[e], s5_b_im[e], s5_c_re[e], s5_c_im[e], s5_d[e].reshape(-1), w_glu_all,
                            s5_b_glu[e], ev_w_out_all, e, batch=batch, seq=seq)
        else:
            o = layer // 2
            h = _odd_mixer(h, gm[layer], m_w_main_all, m_w_dt_all, m_conv_w[o], m_conv_b[o], m_dt_bias[o], m_a_log[o],
                           m_d[o], m_norm[o], m_w_out_all, o, batch=batch, seq=seq)
        if layer == depth - 1:
            h, _ = _ffn(h, g2, *w_cur, layer, final_g=final_norm.astype(F32).reshape(1, d))
        else:
            h, w_cur = _ffn(h, g2, *w_cur, layer, next_f32=ffn_f32[0] + (layer + 1,))
    return h.reshape(batch, seq, d).astype(x.dtype)
```

```python
import functools
import math

import jax
import jax.numpy as jnp
from jax import lax
from jax.experimental import pallas as pl
from jax.experimental.pallas import tpu as pltpu

F32 = jnp.float32
BF16 = jnp.bfloat16

NORM_EPS = 1e-5
NEG_INF = -1e30
LANES = 128
SUBLANES = 8
VMEM_LIMIT_BYTES = 60 * 1024 * 1024

HEAD_DIM = 64
ATTN_BLOCK = 128
ROPE_THETA = 500000.0
ROPE_DIM = HEAD_DIM // 4
ATTN_BLOCKS_PER_STEP = 4
S5_GROUP = 16
S5_STATE = 64
S5_GROUPS_PER_TILE = 16
M_HEAD_DIM = 64
M_STATE = 128
M_CHUNK = 128
M_CONV = 4


def _params(*sem):
    return pltpu.CompilerParams(dimension_semantics=sem, vmem_limit_bytes=VMEM_LIMIT_BYTES)


def _rms(x, g):
    return x * lax.rsqrt(jnp.mean(x * x, axis=-1, keepdims=True) + NORM_EPS) * g


def _ffn_body(final, cast_next, x_ref, g_ref, wg_ref, wu_ref, wd_ref, *rest):
    xn_ref = rest[-1]
    o_ref = rest[-5] if cast_next else rest[-2]
    j = pl.program_id(1)
    if cast_next:
        n_in = 1 if final else 0
        for src, dst in zip(rest[n_in:n_in + 3], rest[-4:-1]):
            dst[...] = src[...].astype(BF16)

    @pl.when(j == 0)
    def _():
        x = x_ref[...]
        xn_ref[...] = _rms(x, g_ref[...]).astype(BF16)
        o_ref[...] = x

    xn = xn_ref[...]
    gate = jnp.dot(xn, wg_ref[...], preferred_element_type=F32)
    up = jnp.dot(xn, wu_ref[...], preferred_element_type=F32)
    h = (jax.nn.silu(gate) * up * 0.5).astype(BF16)
    o_ref[...] += jnp.dot(h, wd_ref[...], preferred_element_type=F32)

    if final:
        @pl.when(j == pl.num_programs(1) - 1)
        def _():
            o_ref[...] = _rms(o_ref[...], rest[0][...])


def _ffn(x, g, wg, wu, wd, layer, final_g=None, next_f32=None, *, tm=1024, tf=512):
    t, d = x.shape
    dff = wg.shape[-1]
    ni, nj = t // tm, dff // tf
    in_specs = [
        pl.BlockSpec((tm, d), lambda i, j: (i, 0)),
        pl.BlockSpec((None, 1, d), lambda i, j: (layer, 0, 0)),
        pl.BlockSpec((d, tf), lambda i, j: (0, j)),
        pl.BlockSpec((d, tf), lambda i, j: (0, j)),
        pl.BlockSpec((tf, d), lambda i, j: (j, 0)),
    ]
    args = [x, g, wg, wu, wd]
    out_specs = [pl.BlockSpec((tm, d), lambda i, j: (i, 0))]
    out_shape = [jax.ShapeDtypeStruct((t, d), F32)]
    if final_g is not None:
        in_specs.append(pl.BlockSpec((1, d), lambda i, j: (0, 0)))
        args.append(final_g)
    if next_f32 is not None:
        ng, nu, nd, nl = next_f32
        assert d % ni == 0 and dff % nj == 0
        up_blk, dn_blk = (d // ni, dff // nj), (dff // nj, d // ni)
        in_specs += [pl.BlockSpec((None,) + up_blk, lambda i, j: (nl, i, j)),
                     pl.BlockSpec((None,) + up_blk, lambda i, j: (nl, i, j)),
                     pl.BlockSpec((None,) + dn_blk, lambda i, j: (nl, j, i))]
        args += [ng, nu, nd]
        out_specs += [pl.BlockSpec(up_blk, lambda i, j: (i, j)), pl.BlockSpec(up_blk, lambda i, j: (i, j)),
                      pl.BlockSpec(dn_blk, lambda i, j: (j, i))]
        out_shape += [jax.ShapeDtypeStruct((d, dff), BF16)] * 2 + [jax.ShapeDtypeStruct((dff, d), BF16)]
    outs = pl.pallas_call(
        functools.partial(_ffn_body, final_g is not None, next_f32 is not None),
        grid=(ni, nj),
        in_specs=in_specs,
        out_specs=out_specs,
        out_shape=out_shape,
        scratch_shapes=[pltpu.VMEM((tm, d), BF16)],
        compiler_params=_params("parallel", "arbitrary"),
        name="swiglu_ffn",
    )(*args)
    return outs[0], tuple(outs[1:])


def _outproj_body(n_terms, x_ref, *refs):
    o_ref = refs[-1]
    acc = x_ref[...]
    for i in range(n_terms):
        acc = acc + jnp.dot(refs[2 * i][...], refs[2 * i + 1][...], preferred_element_type=F32)
    o_ref[...] = acc


def _outproj(x, terms, *, tm):
    t, d = x.shape
    in_specs = [pl.BlockSpec((tm, d), lambda i: (i, 0))]
    args = [x]
    for a, a_spec, w, w_spec in terms:
        in_specs += [a_spec, w_spec]
        args += [a, w]
    return pl.pallas_call(
        functools.partial(_outproj_body, len(terms)),
        grid=(t // tm,),
        in_specs=in_specs,
        out_specs=pl.BlockSpec((tm, d), lambda i: (i, 0)),
        out_shape=jax.ShapeDtypeStruct((t, d), F32),
        compiler_params=_params("parallel"),
        name="residual_outproj",
    )(*args)


def _even_inproj_body(n_rot, x_ref, g_ref, w_ref, pos_ref, invf_ref, q_ref, k_ref, v_ref, u_ref, xn_ref):
    xn_ref[...] = _rms(x_ref[...], g_ref[...]).astype(BF16)
    ang = pos_ref[...] * invf_ref[...]
    cos, sin = jnp.cos(ang), jnp.sin(ang)
    lane = lax.broadcasted_iota(jnp.int32, ang.shape, 1) % HEAD_DIM
    half = ROPE_DIM // 2
    sin_from_lower = jnp.where((lane >= half) & (lane < ROPE_DIM), sin, 0.0)
    sin_from_upper = jnp.where(lane < half, sin, 0.0)

    cw = 512
    qw, kw, vw = q_ref.shape[-1], k_ref.shape[-1], v_ref.shape[-1]
    for c0 in range(0, w_ref.shape[-1], cw):
        p = jnp.dot(xn_ref[...], w_ref[:, c0:c0 + cw], preferred_element_type=F32)
        for l0 in range(0, cw, LANES):
            col = c0 + l0
            t = p[:, l0:l0 + LANES]
            if col < n_rot:
                t = (t * cos + pltpu.roll(t, half, axis=1) * sin_from_lower
                     - pltpu.roll(t, LANES - half, axis=1) * sin_from_upper)
            if col < qw:
                q_ref[:, col:col + LANES] = t.astype(BF16)
            elif col < qw + kw:
                k_ref[:, col - qw:col - qw + LANES] = t.astype(BF16)
            elif col < qw + kw + vw:
                v_ref[:, col - qw - kw:col - qw - kw + LANES] = t.astype(BF16)
            else:
                c = col - qw - kw - vw
                u_ref[:, c:c + LANES] = t


def _even_inproj(x, g, w_all, e, pos_b, invf, *, qw, kw, vw, uw, tm=512):
    t, d = x.shape
    return pl.pallas_call(
        functools.partial(_even_inproj_body, qw + kw),
        grid=(t // tm,),
        in_specs=[
            pl.BlockSpec((tm, d), lambda i: (i, 0)),
            pl.BlockSpec((1, d), lambda i: (0, 0)),
            pl.BlockSpec((None,) + w_all.shape[1:], lambda i: (e, 0, 0)),
            pl.BlockSpec((tm, LANES), lambda i: (i, 0)),
            pl.BlockSpec((1, LANES), lambda i: (0, 0)),
        ],
        out_specs=[
            pl.BlockSpec((tm, qw), lambda i: (i, 0)),
            pl.BlockSpec((tm, kw), lambda i: (i, 0)),
            pl.BlockSpec((tm, vw), lambda i: (i, 0)),
            pl.BlockSpec((tm, uw), lambda i: (i, 0)),
        ],
        out_shape=[
            jax.ShapeDtypeStruct((t, qw), BF16),
            jax.ShapeDtypeStruct((t, kw), BF16),
            jax.ShapeDtypeStruct((t, vw), BF16),
            jax.ShapeDtypeStruct((t, uw), F32),
        ],
        scratch_shapes=[pltpu.VMEM((tm, d), BF16)],
        compiler_params=_params("parallel"),
        name="even_inproj_rotary",
    )(x, g, w_all, pos_b, invf)


def _attn_body(pairs_per_kv, q_ref, kc_ref, kp_ref, vc_ref, vp_ref, sink_ref, o_ref):
    m_step = pl.program_id(1)
    blk = ATTN_BLOCK
    kk = jnp.concatenate([kp_ref[...], kc_ref[...]], axis=0)
    vv = jnp.concatenate([vp_ref[...], vc_ref[...]], axis=0)
    lower = lax.broadcasted_iota(jnp.int32, (blk, LANES), 1) < HEAD_DIM
    key = lax.broadcasted_iota(jnp.int32, (2 * blk, 2 * blk), 0)
    qrow = lax.broadcasted_iota(jnp.int32, (2 * blk, 2 * blk), 1) % blk
    band = (key > qrow) & (key <= qrow + blk)
    upper_rows = lax.broadcasted_iota(jnp.int32, (LANES, blk), 0) >= HEAD_DIM
    n_pairs = q_ref.shape[-1] // LANES
    scale = 1.0 / math.sqrt(HEAD_DIM)

    def scores(t, c):
        hk = c // pairs_per_kv
        qc = q_ref[t * blk:(t + 1) * blk, c * LANES:(c + 1) * LANES] * scale
        q2 = jnp.concatenate([jnp.where(lower, qc, jnp.zeros_like(qc)),
                              jnp.where(lower, jnp.zeros_like(qc), qc)], axis=0)
        kh = kk[t * blk:t * blk + 2 * blk, hk * LANES:(hk + 1) * LANES]
        return lax.dot_general(kh, q2, (((1,), (1,)), ((), ())), preferred_element_type=F32)

    def finish(t, c, s):
        hk = c // pairs_per_kv
        first = (m_step == 0) if t == 0 else False
        valid = band & ((key >= blk) | jnp.logical_not(first))
        s = jnp.where(valid, s, NEG_INF)
        sink = sink_ref[c]
        mx = jnp.maximum(jnp.max(s, axis=0, keepdims=True), sink)
        p = jnp.exp(s - mx)
        den = jnp.sum(p, axis=0, keepdims=True) + jnp.exp(sink - mx)
        p = (p * (1.0 / den)).astype(BF16)
        vh = vv[t * blk:t * blk + 2 * blk, hk * LANES:(hk + 1) * LANES]
        r = lax.dot_general(vh, p, (((0,), (0,)), ((), ())), preferred_element_type=F32)
        o_t = jnp.where(upper_rows, r[:, blk:], r[:, :blk])
        o_ref[t * blk:(t + 1) * blk, c * LANES:(c + 1) * LANES] = o_t.T.astype(BF16)

    units = [(t, c) for t in range(ATTN_BLOCKS_PER_STEP) for c in range(n_pairs)]
    ahead = 4
    pending = [scores(*u) for u in units[:ahead]]
    for n, (t, c) in enumerate(units):
        if n + ahead < len(units):
            pending.append(scores(*units[n + ahead]))
        finish(t, c, pending.pop(0))


def _attention(q, kd, vd, sink_rows_all, e, *, batch, seq, pairs_per_kv):
    t, qw = q.shape
    kw = kd.shape[-1]
    rows = ATTN_BLOCK * ATTN_BLOCKS_PER_STEP
    steps = seq // rows
    bps = ATTN_BLOCKS_PER_STEP
    nblk = seq // ATTN_BLOCK

    def cur(b, m):
        return (b * steps + m, 0)

    def prev(b, m):
        return (b * nblk + jnp.maximum(m * bps - 1, 0), 0)

    return pl.pallas_call(
        functools.partial(_attn_body, pairs_per_kv),
        grid=(batch, steps),
        in_specs=[
            pl.BlockSpec((rows, qw), cur),
            pl.BlockSpec((rows, kw), cur),
            pl.BlockSpec((ATTN_BLOCK, kw), prev),
            pl.BlockSpec((rows, kw), cur),
            pl.BlockSpec((ATTN_BLOCK, kw), prev),
            pl.BlockSpec((qw // LANES,) + sink_rows_all.shape[1:], lambda b, m: (e, 0, 0)),
        ],
        out_specs=pl.BlockSpec((rows, qw), cur),
        out_shape=jax.ShapeDtypeStruct((t, qw), BF16),
        compiler_params=_params("parallel", "arbitrary"),
        name="swa_attention",
    )(q, kd, kd, vd, vd, sink_rows_all)


def _s5_discretise_body(are_ref, aim_ref, logdt_ref, bre_ref, bim_ref, abr_ref, abi_ref, bbr_ref, bbi_ref):
    are, aim = are_ref[...], aim_ref[...]
    dt = jnp.exp(logdt_ref[...])
    mag = jnp.exp(are * dt)
    abar_re, abar_im = mag * jnp.cos(aim * dt), mag * jnp.sin(aim * dt)
    nr, ni = abar_re - 1.0, abar_im
    den = are * are + aim * aim
    coef_re = (nr * are + ni * aim) / den
    coef_im = (ni * are - nr * aim) / den
    bre, bim = bre_ref[...], bim_ref[...]
    abr_ref[...] = abar_re
    abi_ref[...] = abar_im
    bbr_ref[...] = coef_re * bre - coef_im * bim
    bbi_ref[...] = coef_re * bim + coef_im * bre


def _s5_discretise(a_re, a_im, log_dt, b_re, b_im):
    g, p = a_re.shape
    c = b_re.shape[-1]
    col = lambda v: v.reshape(g * p, 1)
    logdt_col = jnp.broadcast_to(log_dt[:, None], (g, p)).reshape(g * p, 1)
    outs = pl.pallas_call(
        _s5_discretise_body,
        out_shape=[jax.ShapeDtypeStruct((g * p, 1), F32)] * 2 + [jax.ShapeDtypeStruct((g * p, c), F32)] * 2,
        name="s5_discretise",
    )(col(a_re), col(a_im), logdt_col, b_re.reshape(g * p, c), b_im.reshape(g * p, c))
    abr, abi, bbr, bbi = outs
    return abr.reshape(g, p), abi.reshape(g, p), bbr.reshape(g, p, c), bbi.reshape(g, p, c)


def _cmul(ar, ai, br, bi):
    return ar * br - ai * bi, ar * bi + ai * br


def _s5_body(n_tiles, steps, u_ref, wb_ref, wc_ref, are_ref, aim_ref, d_ref, wglu_ref, bglu_ref,
             o_ref, us_ref, ys_ref, carry_ref, pw_ref, p8_ref):
    blk = pl.program_id(1)
    sw = are_ref.shape[-1]
    cw = wc_ref.shape[-1]
    nslab = us_ref.shape[0]
    nsub = SUBLANES

    @pl.when(blk == 0)
    def _():
        carry_ref[...] = jnp.zeros_like(carry_ref)
        for kt in range(n_tiles):
            ar, ai = are_ref[kt], aim_ref[kt]
            pr, pi = ar, ai
            for j in range(steps):
                pw_ref[kt, j:j + 1, :sw] = pr
                pw_ref[kt, j:j + 1, sw:] = pi
                if j + 1 < steps:
                    pr, pi = _cmul(pr, pi, ar, ai)
            qr, qi = pr, pi
            for s in range(nsub):
                p8_ref[kt, s:s + 1, :sw] = qr
                p8_ref[kt, s:s + 1, sw:] = qi
                if s + 1 < nsub:
                    qr, qi = _cmul(qr, qi, pr, pi)

    for s in range(nsub):
        for c in range(nslab):
            us_ref[c, pl.ds(s, steps, stride=nsub), :] = u_ref[s * steps:(s + 1) * steps, c * LANES:(c + 1) * LANES]

    def u_cols(c0, n):
        return jnp.concatenate([us_ref[c0 + i] for i in range(n)], axis=1)

    def input_drive(kt):
        ub = u_cols(kt * (cw // LANES), cw // LANES).astype(BF16)
        return jnp.dot(ub, wb_ref[kt], preferred_element_type=F32)

    srow = lax.broadcasted_iota(jnp.int32, (nsub, sw), 0)

    def scan_tile(kt, bu):
        ar = jnp.broadcast_to(are_ref[kt], (nsub, sw))
        ai = jnp.broadcast_to(aim_ref[kt], (nsub, sw))
        hr, hi = jnp.zeros((nsub, sw), F32), jnp.zeros((nsub, sw), F32)
        zero_start = []
        for j in range(steps):
            r0 = j * nsub
            pr, pi = _cmul(ar, ai, hr, hi)
            hr, hi = pr + bu[r0:r0 + nsub, :sw], pi + bu[r0:r0 + nsub, sw:]
            zero_start.append((hr, hi))
        xr, xi = hr, hi
        for d in (1, 2, 4):
            pr, pi = p8_ref[kt, d - 1:d, :sw], p8_ref[kt, d - 1:d, sw:]
            sr = jnp.where(srow >= d, pltpu.roll(xr, d, axis=0), 0.0)
            si = jnp.where(srow >= d, pltpu.roll(xi, d, axis=0), 0.0)
            qr, qi = _cmul(pr, pi, sr, si)
            xr, xi = xr + qr, xi + qi
        cr, ci = carry_ref[kt, :, :sw], carry_ref[kt, :, sw:]
        qr, qi = _cmul(p8_ref[kt, :, :sw], p8_ref[kt, :, sw:], cr, ci)
        xr, xi = xr + qr, xi + qi
        init_r = jnp.where(srow == 0, cr, pltpu.roll(xr, 1, axis=0))
        init_i = jnp.where(srow == 0, ci, pltpu.roll(xi, 1, axis=0))
        carry_ref[kt, :, :sw] = jnp.broadcast_to(xr[nsub - 1:nsub], (nsub, sw))
        carry_ref[kt, :, sw:] = jnp.broadcast_to(xi[nsub - 1:nsub], (nsub, sw))
        rows = []
        for j in range(steps):
            pr, pi = _cmul(pw_ref[kt, j:j + 1, :sw], pw_ref[kt, j:j + 1, sw:], init_r, init_i)
            rows.append(jnp.concatenate([zero_start[j][0] + pr, zero_start[j][1] + pi], axis=1))
        return jnp.concatenate(rows, axis=0)

    bus = [input_drive(kt) for kt in range(n_tiles)]
    ys = [jnp.dot(scan_tile(kt, bus[kt]).astype(BF16), wc_ref[kt], preferred_element_type=F32)
          for kt in range(n_tiles)]
    y = jnp.concatenate(ys, axis=1) + d_ref[...] * u_cols(0, nslab)
    gl = jax.nn.gelu(y)
    gate = jnp.dot(gl.astype(BF16), wglu_ref[...], preferred_element_type=F32) + bglu_ref[...]
    out = gl * jax.nn.sigmoid(gate)
    for c in range(nslab):
        ys_ref[c] = out[:, c * LANES:(c + 1) * LANES]
    for s in range(nsub):
        for c in range(nslab):
            o_ref[s * steps:(s + 1) * steps, c * LANES:(c + 1) * LANES] = (
                ys_ref[c, pl.ds(s, steps, stride=nsub), :].astype(BF16))


def _s5(u, wb, wc, a_re_t, a_im_t, d_rows, w_glu_all, b_glu_rows, e, n_tiles, *, batch, seq, steps=32):
    t, width = u.shape
    _, cw, sw2 = wb.shape
    rows_per_block = steps * SUBLANES
    nblk = seq // rows_per_block
    layer3 = lambda b, k: (e, 0, 0)
    row = lambda b, k: (b * nblk + k, 0)
    return pl.pallas_call(
        functools.partial(_s5_body, n_tiles, steps),
        grid=(batch, nblk),
        in_specs=[
            pl.BlockSpec((rows_per_block, width), row),
            pl.BlockSpec((n_tiles,) + wb.shape[1:], layer3),
            pl.BlockSpec((n_tiles,) + wc.shape[1:], layer3),
            pl.BlockSpec((n_tiles,) + a_re_t.shape[1:], layer3),
            pl.BlockSpec((n_tiles,) + a_im_t.shape[1:], layer3),
            pl.BlockSpec((None,) + d_rows.shape[1:], layer3),
            pl.BlockSpec((None,) + w_glu_all.shape[1:], layer3),
            pl.BlockSpec((None,) + b_glu_rows.shape[1:], layer3),
        ],
        out_specs=pl.BlockSpec((rows_per_block, width), row),
        out_shape=jax.ShapeDtypeStruct((t, width), BF16),
        scratch_shapes=[
            pltpu.VMEM((width // LANES, rows_per_block, LANES), F32),
            pltpu.VMEM((width // LANES, rows_per_block, LANES), F32),
            pltpu.VMEM((n_tiles, SUBLANES, sw2), F32),
            pltpu.VMEM((n_tiles, steps, sw2), F32),
            pltpu.VMEM((n_tiles, SUBLANES, sw2), F32),
        ],
        compiler_params=_params("parallel", "arbitrary"),
        name="s5_scan_glu",
    )(u, wb, wc, a_re_t, a_im_t, d_rows, w_glu_all, b_glu_rows)


HALO_ROWS = 16


def _odd_inproj_body(n_plain, tiles_per_seq, x_ref, xh_ref, g_ref, w_ref, wdt_ref, cw_ref, cb_ref,
                     o_ref, dt_ref, xn_ref):
    i, j = pl.program_id(0), pl.program_id(1)

    @pl.when(j == 0)
    def _():
        g = g_ref[...]
        xn_ref[0:HALO_ROWS, :] = _rms(xh_ref[...], g).astype(BF16)
        xn_ref[HALO_ROWS:, :] = _rms(x_ref[...], g).astype(BF16)
        dt_ref[...] = jnp.dot(xn_ref[HALO_ROWS:, :], wdt_ref[...], preferred_element_type=F32)

    @pl.when(j < n_plain)
    def _():
        o_ref[...] = jnp.dot(xn_ref[HALO_ROWS:, :], w_ref[...], preferred_element_type=F32)

    @pl.when(j >= n_plain)
    def _():
        p = jnp.dot(xn_ref[...], w_ref[...], preferred_element_type=F32)
        seq_start = (i % tiles_per_seq) == 0
        halo = jnp.where(seq_start, 0.0, p[:HALO_ROWS])
        ext = jnp.concatenate([halo, p[HALO_ROWS:]], axis=0)
        acc = cb_ref[...] + cw_ref[M_CONV - 1:M_CONV, :] * p[HALO_ROWS:]
        for k in range(1, M_CONV):
            acc = acc + cw_ref[M_CONV - 1 - k:M_CONV - k, :] * ext[HALO_ROWS - k:ext.shape[0] - k]
        o_ref[...] = jax.nn.silu(acc)


def _odd_inproj(x, g, w, layer, n, wdt, conv_w, conv_b, *, seq, n_plain_cols, tm=1024, tn=1024):
    t, d = x.shape
    assert seq % tm == 0 and n_plain_cols % tn == 0 and n % tn == 0
    n_plain = n_plain_cols // tn
    halo_blocks = tm // HALO_ROWS
    conv_col = lambda i, j: (0, jnp.maximum(j - n_plain, 0))
    return pl.pallas_call(
        functools.partial(_odd_inproj_body, n_plain, seq // tm),
        grid=(t // tm, n // tn),
        in_specs=[
            pl.BlockSpec((tm, d), lambda i, j: (i, 0)),
            pl.BlockSpec((HALO_ROWS, d), lambda i, j: (jnp.maximum(i * halo_blocks - 1, 0), 0)),
            pl.BlockSpec((1, d), lambda i, j: (0, 0)),
            pl.BlockSpec((None, d, tn), lambda i, j: (layer, 0, j)),
            pl.BlockSpec((None,) + wdt.shape[1:], lambda i, j: (layer, 0, 0)),
            pl.BlockSpec((M_CONV, tn), conv_col),
            pl.BlockSpec((1, tn), conv_col),
        ],
        out_specs=[
            pl.BlockSpec((tm, tn), lambda i, j: (i, j)),
            pl.BlockSpec((tm, wdt.shape[-1]), lambda i, j: (i, 0)),
        ],
        out_shape=[jax.ShapeDtypeStruct((t, n), F32), jax.ShapeDtypeStruct((t, wdt.shape[-1]), F32)],
        scratch_shapes=[pltpu.VMEM((HALO_ROWS + tm, d), BF16)],
        compiler_params=_params("parallel", "arbitrary"),
        name="odd_inproj_conv",
    )(x, x, g, w, wdt, conv_w, conv_b)


def _ssd_body(n_groups, z_ref, xs_ref, bm_ref, cm_ref, dt_ref, dtb_ref, alog_ref, dsk_ref, ng_ref,
              o_ref, state_ref, y_ref):
    c = pl.program_id(1)
    L = M_CHUNK
    inner = xs_ref.shape[-1]
    gw = bm_ref.shape[-1] // n_groups
    hw = inner // n_groups

    @pl.when(c == 0)
    def _():
        state_ref[...] = jnp.zeros_like(state_ref)

    dt = jax.nn.softplus(dt_ref[...] + dtb_ref[...])
    a = -jnp.exp(alog_ref[...])
    acum = a * dt
    rowi = lax.broadcasted_iota(jnp.int32, acum.shape, 0)
    k = 1
    while k < L:
        acum = acum + jnp.where(rowi >= k, pltpu.roll(acum, k, axis=0), 0.0)
        k *= 2
    src = acum - jnp.log(dt)
    src_t = src.T
    a_last = acum[L - 1:L, :]

    li = lax.broadcasted_iota(jnp.int32, (L, L), 0)
    si = lax.broadcasted_iota(jnp.int32, (L, L), 1)
    causal = li >= si
    lane = lax.broadcasted_iota(jnp.int32, (L, LANES), 1)
    lower = lane < M_HEAD_DIM
    heads_per_group = hw // M_HEAD_DIM

    for g in range(n_groups):
        cgb = cm_ref[:, g * gw:(g + 1) * gw].astype(BF16)
        bgb = bm_ref[:, g * gw:(g + 1) * gw].astype(BF16)
        cb = lax.dot_general(cgb, bgb, (((1,), (1,)), ((), ())), preferred_element_type=F32)
        prev = state_ref[:, g * hw:(g + 1) * hw]
        y_off = jnp.dot(cgb, prev.astype(BF16), preferred_element_type=F32)
        xd_parts = []
        for pr in range(heads_per_group // 2):
            h0 = g * heads_per_group + 2 * pr
            col0 = g * hw + pr * LANES
            xp = xs_ref[:, col0:col0 + LANES]
            ms = []
            for h in (h0, h0 + 1):
                seg = acum[:, h:h + 1] - src_t[h:h + 1, :]
                ms.append((cb * jnp.exp(jnp.where(causal, seg, -jnp.inf))).astype(BF16))
            lhs = jnp.concatenate(ms, axis=1)
            xpb = xp.astype(BF16)
            rhs = jnp.concatenate([jnp.where(lower, xpb, jnp.zeros_like(xpb)),
                                   jnp.where(lower, jnp.zeros_like(xpb), xpb)], axis=0)
            y_diag = jnp.dot(lhs, rhs, preferred_element_type=F32)
            al = jnp.where(lower, acum[:, h0:h0 + 1], acum[:, h0 + 1:h0 + 2])
            srcl = jnp.where(lower, src[:, h0:h0 + 1], src[:, h0 + 1:h0 + 2])
            alast = jnp.where(lower[0:1], a_last[:, h0:h0 + 1], a_last[:, h0 + 1:h0 + 2])
            y_ref[:, col0:col0 + LANES] = (y_diag + jnp.exp(al) * y_off[:, pr * LANES:(pr + 1) * LANES]
                                           + dsk_ref[:, col0:col0 + LANES] * xp)
            xd_parts.append((xp * jnp.exp(alast - srcl)).astype(BF16))
            state_ref[:, col0:col0 + LANES] = state_ref[:, col0:col0 + LANES] * jnp.exp(alast)
        xd = jnp.concatenate(xd_parts, axis=1)
        state_ref[:, g * hw:(g + 1) * hw] += lax.dot_general(
            bgb, xd, (((0,), (0,)), ((), ())), preferred_element_type=F32)

    for g in range(n_groups):
        sl = slice(g * hw, (g + 1) * hw)
        y = y_ref[:, sl] * jax.nn.silu(z_ref[:, sl])
        y = y * lax.rsqrt(jnp.mean(y * y, axis=-1, keepdims=True) + NORM_EPS)
        o_ref[:, sl] = (y * ng_ref[:, sl]).astype(BF16)


def _ssd(zxbc, dt_raw, dt_bias, a_log, d_row, norm_g, *, batch, seq, inner, n_groups):
    t = zxbc.shape[0]
    nchunks = seq // M_CHUNK
    gn = n_groups * M_STATE
    row = lambda b, c: (b * nchunks + c, 0)
    const = lambda b, c: (0, 0)
    return pl.pallas_call(
        functools.partial(_ssd_body, n_groups),
        grid=(batch, nchunks),
        in_specs=[
            pl.BlockSpec((M_CHUNK, inner), row),
            pl.BlockSpec((M_CHUNK, inner), lambda b, c: (b * nchunks + c, 1)),
            pl.BlockSpec((M_CHUNK, gn), lambda b, c: (b * nchunks + c, 2 * inner // gn)),
            pl.BlockSpec((M_CHUNK, gn), lambda b, c: (b * nchunks + c, 2 * inner // gn + 1)),
            pl.BlockSpec((M_CHUNK, LANES), row),
            pl.BlockSpec((1, LANES), const),
            pl.BlockSpec((1, LANES), const),
            pl.BlockSpec((1, inner), const),
            pl.BlockSpec((1, inner), const),
        ],
        out_specs=pl.BlockSpec((M_CHUNK, inner), row),
        out_shape=jax.ShapeDtypeStruct((t, inner), BF16),
        scratch_shapes=[pltpu.VMEM((M_STATE, inner), F32),
                        pltpu.VMEM((M_CHUNK, inner), F32)],
        compiler_params=_params("parallel", "arbitrary"),
        name="ssd_chunk",
    )(zxbc, zxbc, zxbc, zxbc, dt_raw, dt_bias, a_log, d_row, norm_g)


def _cast_body(w_ref, o_ref):
    o_ref[...] = w_ref[...].astype(BF16)


def _cast_leading_cols(w, n_cols, *, tn=1024):
    layers, rows, _ = w.shape
    assert n_cols % tn == 0
    return pl.pallas_call(
        _cast_body,
        grid=(layers, n_cols // tn),
        in_specs=[pl.BlockSpec((None, rows, tn), lambda l, j: (l, 0, j))],
        out_specs=pl.BlockSpec((None, rows, tn), lambda l, j: (l, 0, j)),
        out_shape=jax.ShapeDtypeStruct((layers, rows, n_cols), BF16),
        compiler_params=_params("parallel", "parallel"),
        name="cast_weight_columns",
    )(w)


def _dup_heads(w, n_heads):
    lead = w.shape[:-1]
    w = w.reshape(lead + (n_heads, 1, HEAD_DIM))
    return jnp.broadcast_to(w, lead + (n_heads, 2, HEAD_DIM)).reshape(lead + (n_heads * 2 * HEAD_DIM,))


def _block_diag_tiles(blocks, n_tiles):
    g, r, c = blocks.shape
    gpt = g // n_tiles
    eye = jnp.eye(gpt, dtype=blocks.dtype)
    b = blocks.reshape(n_tiles, gpt, r, c)
    out = jnp.einsum('tgrc,gh->tgrhc', b, eye)
    return out.reshape(n_tiles, gpt * r, gpt * c)


def _rope_inv_freq_row():
    half = ROPE_DIM // 2
    inv_freq = jnp.exp(-math.log(ROPE_THETA) * jnp.arange(half, dtype=F32) * (2.0 / ROPE_DIM))
    lane = jnp.arange(LANES) % HEAD_DIM
    return jnp.where(lane < ROPE_DIM, inv_freq[lane % half], 0.0).reshape(1, LANES).astype(F32)


def _even_prep(w_in, sinks, a_re, a_im, log_dt, b_re, b_im, c_re, c_im, d_skip, w_glu, b_glu, w_out):
    n_layers, d, _ = w_in.shape
    n_q = sinks.shape[-1]
    a_width = n_q * HEAD_DIM
    groups, state = a_re.shape[1:]
    s5_width = groups * S5_GROUP
    kv_width = (w_in.shape[-1] - a_width - s5_width) // 2
    n_kv = kv_width // HEAD_DIM
    grp = n_q // n_kv
    assert grp % 2 == 0 and a_width % LANES == 0 and a_width == s5_width

    w_in = w_in.astype(BF16)
    wk = _dup_heads(w_in[..., a_width:a_width + kv_width], n_kv)
    wv = _dup_heads(w_in[..., a_width + kv_width:a_width + 2 * kv_width], n_kv)
    w_all = jnp.concatenate([w_in[..., :a_width], wk, wv, w_in[..., a_width + 2 * kv_width:]], axis=-1)

    sink_rows = jnp.repeat(sinks.astype(F32).reshape(n_layers * n_q // 2, 2), ATTN_BLOCK, axis=1)
    sink_rows = sink_rows.reshape(n_layers * n_q // 2, 1, 2 * ATTN_BLOCK)

    fold = lambda v: v.astype(F32).reshape((n_layers * groups,) + v.shape[2:])
    abr, abi, bbr, bbi = _s5_discretise(fold(a_re), fold(a_im), fold(log_dt), fold(b_re), fold(b_im))
    n_tiles = groups // S5_GROUPS_PER_TILE
    all_tiles = n_layers * n_tiles
    to_in = lambda bb: _block_diag_tiles(jnp.swapaxes(bb, 1, 2), all_tiles)
    wb = jnp.concatenate([to_in(bbr), to_in(bbi)], axis=-1).astype(BF16)
    to_out = lambda cc: _block_diag_tiles(jnp.swapaxes(fold(cc), 1, 2), all_tiles)
    wc = jnp.concatenate([to_out(c_re), -to_out(c_im)], axis=1).astype(BF16)
    return dict(
        w_all=w_all, sink_rows=sink_rows, wb=wb, wc=wc, n_tiles=n_tiles,
        a_re_t=abr.reshape(all_tiles, 1, -1), a_im_t=abi.reshape(all_tiles, 1, -1),
        d_rows=d_skip.astype(F32).reshape(n_layers, 1, s5_width), b_glu_rows=b_glu.astype(F32).reshape(n_layers, 1, s5_width),
        w_glu=w_glu.astype(BF16), w_out=w_out.astype(BF16),
        a_width=a_width, kv_width=kv_width, s5_width=s5_width, pairs_per_kv=grp // 2)


def _even_mixer(x, pos_b, norm_g, p, e, *, batch, seq):
    d = x.shape[-1]
    a_width, kv_width, s5_width = p["a_width"], p["kv_width"], p["s5_width"]
    q, kd, vd, u = _even_inproj(x, norm_g, p["w_all"], e, pos_b, _rope_inv_freq_row(),
                                qw=a_width, kw=2 * kv_width, vw=2 * kv_width, uw=s5_width)
    attn = _attention(q, kd, vd, p["sink_rows"], e, batch=batch, seq=seq, pairs_per_kv=p["pairs_per_kv"])
    ssm = _s5(u, p["wb"], p["wc"], p["a_re_t"], p["a_im_t"], p["d_rows"], p["w_glu"], p["b_glu_rows"], e,
              p["n_tiles"], batch=batch, seq=seq)
    tm = 512
    terms = [
        (attn, pl.BlockSpec((tm, a_width), lambda i: (i, 0)),
         p["w_out"], pl.BlockSpec((None, a_width, d), lambda i: (e, 0, 0))),
        (ssm, pl.BlockSpec((tm, s5_width), lambda i: (i, 0)),
         p["w_out"], pl.BlockSpec((None, s5_width, d), lambda i: (e, 1, 0))),
    ]
    return _outproj(x, terms, tm=tm)


def _odd_mixer(x, norm_g, w_main_all, w_dt_all, conv_w, conv_b, dt_bias, a_log, d_skip, norm_out, w_out_all, o, *,
               batch, seq):
    d = x.shape[-1]
    inner = norm_out.shape[0]
    heads = a_log.shape[0]
    conv_dim = conv_w.shape[-1]
    n_groups = (conv_dim - inner) // (2 * M_STATE)
    assert heads <= LANES and inner // heads == M_HEAD_DIM
    pad = LANES - heads
    zxbc, dt_raw = _odd_inproj(x, norm_g, w_main_all, o, inner + conv_dim, w_dt_all, conv_w.astype(F32),
                               conv_b.astype(F32).reshape(1, conv_dim), seq=seq, n_plain_cols=inner)
    row = lambda v: jnp.pad(v.astype(F32), (0, pad)).reshape(1, LANES)
    y = _ssd(zxbc, dt_raw, row(dt_bias), row(a_log), jnp.repeat(d_skip.astype(F32), M_HEAD_DIM).reshape(1, inner),
             norm_out.astype(F32).reshape(1, inner), batch=batch, seq=seq, inner=inner, n_groups=n_groups)
    tm = 512
    return _outproj(x, [(y, pl.BlockSpec((tm, inner), lambda i: (i, 0)),
                         w_out_all, pl.BlockSpec((None, inner, d), lambda i: (o, 0, 0)))], tm=tm)


def kernel(x, positions, norm_ffn1, ffn1_gate, ffn1_up, ffn1_down, norm_mix, norm_ffn2, ffn2_gate, ffn2_up,
           ffn2_down, ev_w_in, ev_sinks, s5_a_re, s5_a_im, s5_log_dt, s5_b_re, s5_b_im, s5_c_re, s5_c_im, s5_d,
           s5_w_glu, s5_b_glu, ev_w_out, m_w_in, m_conv_w, m_conv_b, m_dt_bias, m_a_log, m_d, m_norm, m_w_out,
           final_norm):
    batch, seq, d = x.shape
    depth = norm_ffn1.shape[0]
    t = batch * seq
    h = x.reshape(t, d).astype(F32)
    pos_b = jnp.broadcast_to(positions.reshape(t, 1).astype(F32), (t, LANES))
    g1, g2, gm = (v.astype(F32).reshape(depth, 1, d) for v in (norm_ffn1, norm_ffn2, norm_mix))
    even = _even_prep(ev_w_in, ev_sinks, s5_a_re, s5_a_im, s5_log_dt, s5_b_re, s5_b_im, s5_c_re, s5_c_im, s5_d,
                      s5_w_glu, s5_b_glu, ev_w_out)
    m_w_out_all = m_w_out.astype(BF16)
    m_main = m_norm.shape[-1] + m_conv_w.shape[-1]
    m_w_main_all = _cast_leading_cols(m_w_in, m_main)
    m_w_dt_all = jnp.pad(m_w_in[:, :, m_main:], ((0, 0), (0, 0), (0, LANES - m_a_log.shape[-1]))).astype(BF16)
    ffn_f32 = [(ffn1_gate, ffn1_up, ffn1_down), (ffn2_gate, ffn2_up, ffn2_down)]
    w_cur = tuple(w[0].astype(BF16) for w in ffn_f32[0])
    for layer in range(depth):
        h, w_cur = _ffn(h, g1, *w_cur, layer, next_f32=ffn_f32[1] + (layer,))
        if layer % 2 == 0:
            e = layer // 2
            h = _even_mixer(h, pos_b, gm[layer], even, e, batch=batch, seq=seq)
        else:
            o = layer // 2
            h = _odd_mixer(h, gm[layer], m_w_main_all, m_w_dt_all, m_conv_w[o], m_conv_b[o], m_dt_bias[o], m_a_log[o],
                           m_d[o], m_norm[o], m_w_out_all, o, batch=batch, seq=seq)
        if layer == depth - 1:
            h, _ = _ffn(h, g2, *w_cur, layer, final_g=final_norm.astype(F32).reshape(1, d))
        else:
            h, w_cur = _ffn(h, g2, *w_cur, layer, next_f32=ffn_f32[0] + (layer + 1,))
    return h.reshape(batch, seq, d).astype(x.dtype)
```

```python
import functools
import math

import jax
import jax.numpy as jnp
from jax import lax
from jax.experimental import pallas as pl
from jax.experimental.pallas import tpu as pltpu

F32 = jnp.float32
BF16 = jnp.bfloat16

NORM_EPS = 1e-5
NEG_INF = -1e30
LANES = 128
SUBLANES = 8
VMEM_LIMIT_BYTES = 60 * 1024 * 1024

HEAD_DIM = 64
ATTN_BLOCK = 128
ROPE_THETA = 500000.0
ROPE_DIM = HEAD_DIM // 4
ATTN_BLOCKS_PER_STEP = 4
S5_GROUP = 16
S5_STATE = 64
S5_GROUPS_PER_TILE = 16
M_HEAD_DIM = 64
M_STATE = 128
M_CHUNK = 128
M_CONV = 4


def _params(*sem):
    return pltpu.CompilerParams(dimension_semantics=sem, vmem_limit_bytes=VMEM_LIMIT_BYTES)


def _rms(x, g):
    return x * lax.rsqrt(jnp.mean(x * x, axis=-1, keepdims=True) + NORM_EPS) * g


def _ffn_body(final, cast_next, x_ref, g_ref, wg_ref, wu_ref, wd_ref, *rest):
    xn_ref = rest[-1]
    o_ref = rest[-5] if cast_next else rest[-2]
    j = pl.program_id(1)
    if cast_next:
        n_in = 1 if final else 0
        for src, dst in zip(rest[n_in:n_in + 3], rest[-4:-1]):
            dst[...] = src[...].astype(BF16)

    @pl.when(j == 0)
    def _():
        x = x_ref[...]
        xn_ref[...] = _rms(x, g_ref[...]).astype(BF16)
        o_ref[...] = x

    xn = xn_ref[...]
    gate = jnp.dot(xn, wg_ref[...], preferred_element_type=F32)
    up = jnp.dot(xn, wu_ref[...], preferred_element_type=F32)
    h = (jax.nn.silu(gate) * up * 0.5).astype(BF16)
    o_ref[...] += jnp.dot(h, wd_ref[...], preferred_element_type=F32)

    if final:
        @pl.when(j == pl.num_programs(1) - 1)
        def _():
            o_ref[...] = _rms(o_ref[...], rest[0][...])


def _ffn(x, g, wg, wu, wd, layer, final_g=None, next_f32=None, *, tm=1024, tf=512):
    t, d = x.shape
    dff = wg.shape[-1]
    ni, nj = t // tm, dff // tf
    in_specs = [
        pl.BlockSpec((tm, d), lambda i, j: (i, 0)),
        pl.BlockSpec((None, 1, d), lambda i, j: (layer, 0, 0)),
        pl.BlockSpec((d, tf), lambda i, j: (0, j)),
        pl.BlockSpec((d, tf), lambda i, j: (0, j)),
        pl.BlockSpec((tf, d), lambda i, j: (j, 0)),
    ]
    args = [x, g, wg, wu, wd]
    out_specs = [pl.BlockSpec((tm, d), lambda i, j: (i, 0))]
    out_shape = [jax.ShapeDtypeStruct((t, d), F32)]
    if final_g is not None:
        in_specs.append(pl.BlockSpec((1, d), lambda i, j: (0, 0)))
        args.append(final_g)
    if next_f32 is not None:
        ng, nu, nd, nl = next_f32
        assert d % ni == 0 and dff % nj == 0
        up_blk, dn_blk = (d // ni, dff // nj), (dff // nj, d // ni)
        in_specs += [pl.BlockSpec((None,) + up_blk, lambda i, j: (nl, i, j)),
                     pl.BlockSpec((None,) + up_blk, lambda i, j: (nl, i, j)),
                     pl.BlockSpec((None,) + dn_blk, lambda i, j: (nl, j, i))]
        args += [ng, nu, nd]
        out_specs += [pl.BlockSpec(up_blk, lambda i, j: (i, j)), pl.BlockSpec(up_blk, lambda i, j: (i, j)),
                      pl.BlockSpec(dn_blk, lambda i, j: (j, i))]
        out_shape += [jax.ShapeDtypeStruct((d, dff), BF16)] * 2 + [jax.ShapeDtypeStruct((dff, d), BF16)]
    outs = pl.pallas_call(
        functools.partial(_ffn_body, final_g is not None, next_f32 is not None),
        grid=(ni, nj),
        in_specs=in_specs,
        out_specs=out_specs,
        out_shape=out_shape,
        scratch_shapes=[pltpu.VMEM((tm, d), BF16)],
        compiler_params=_params("parallel", "arbitrary"),
        name="swiglu_ffn",
    )(*args)
    return outs[0], tuple(outs[1:])


def _outproj_body(n_terms, x_ref, *refs):
    o_ref = refs[-1]
    acc = x_ref[...]
    for i in range(n_terms):
        acc = acc + jnp.dot(refs[2 * i][...], refs[2 * i + 1][...], preferred_element_type=F32)
    o_ref[...] = acc


def _outproj(x, terms, *, tm):
    t, d = x.shape
    in_specs = [pl.BlockSpec((tm, d), lambda i: (i, 0))]
    args = [x]
    for a, a_spec, w, w_spec in terms:
        in_specs += [a_spec, w_spec]
        args += [a, w]
    return pl.pallas_call(
        functools.partial(_outproj_body, len(terms)),
        grid=(t // tm,),
        in_specs=in_specs,
        out_specs=pl.BlockSpec((tm, d), lambda i: (i, 0)),
        out_shape=jax.ShapeDtypeStruct((t, d), F32),
        compiler_params=_params("parallel"),
        name="residual_outproj",
    )(*args)


def _even_inproj_body(n_rot, x_ref, g_ref, w_ref, pos_ref, invf_ref, q_ref, k_ref, v_ref, u_ref, xn_ref):
    xn_ref[...] = _rms(x_ref[...], g_ref[...]).astype(BF16)
    ang = pos_ref[...] * invf_ref[...]
    cos, sin = jnp.cos(ang), jnp.sin(ang)
    lane = lax.broadcasted_iota(jnp.int32, ang.shape, 1) % HEAD_DIM
    half = ROPE_DIM // 2
    sin_from_lower = jnp.where((lane >= half) & (lane < ROPE_DIM), sin, 0.0)
    sin_from_upper = jnp.where(lane < half, sin, 0.0)

    first_head = lax.broadcasted_iota(jnp.int32, ang.shape, 1) < HEAD_DIM

    def store_doubled(ref, c, t):
        swapped = pltpu.roll(t, HEAD_DIM, axis=1)
        ref[:, 2 * c:2 * c + LANES] = jnp.where(first_head, t, swapped).astype(BF16)
        ref[:, 2 * c + LANES:2 * c + 2 * LANES] = jnp.where(first_head, swapped, t).astype(BF16)

    cw = 512
    qw, kw, vw = q_ref.shape[-1], k_ref.shape[-1] // 2, v_ref.shape[-1] // 2
    for c0 in range(0, w_ref.shape[-1], cw):
        p = jnp.dot(xn_ref[...], w_ref[:, c0:c0 + cw], preferred_element_type=F32)
        for l0 in range(0, cw, LANES):
            col = c0 + l0
            t = p[:, l0:l0 + LANES]
            if col < n_rot:
                t = (t * cos + pltpu.roll(t, half, axis=1) * sin_from_lower
                     - pltpu.roll(t, LANES - half, axis=1) * sin_from_upper)
            if col < qw:
                q_ref[:, col:col + LANES] = t.astype(BF16)
            elif col < qw + kw:
                store_doubled(k_ref, col - qw, t)
            elif col < qw + kw + vw:
                store_doubled(v_ref, col - qw - kw, t)
            else:
                c = col - qw - kw - vw
                u_ref[:, c:c + LANES] = t


def _even_inproj(x, g, w_all, e, pos_b, invf, *, qw, kw, vw, uw, tm=512):
    t, d = x.shape
    return pl.pallas_call(
        functools.partial(_even_inproj_body, qw + kw // 2),
        grid=(t // tm,),
        in_specs=[
            pl.BlockSpec((tm, d), lambda i: (i, 0)),
            pl.BlockSpec((1, d), lambda i: (0, 0)),
            pl.BlockSpec((None,) + w_all.shape[1:], lambda i: (e, 0, 0)),
            pl.BlockSpec((tm, LANES), lambda i: (i, 0)),
            pl.BlockSpec((1, LANES), lambda i: (0, 0)),
        ],
        out_specs=[
            pl.BlockSpec((tm, qw), lambda i: (i, 0)),
            pl.BlockSpec((tm, kw), lambda i: (i, 0)),
            pl.BlockSpec((tm, vw), lambda i: (i, 0)),
            pl.BlockSpec((tm, uw), lambda i: (i, 0)),
        ],
        out_shape=[
            jax.ShapeDtypeStruct((t, qw), BF16),
            jax.ShapeDtypeStruct((t, kw), BF16),
            jax.ShapeDtypeStruct((t, vw), BF16),
            jax.ShapeDtypeStruct((t, uw), F32),
        ],
        scratch_shapes=[pltpu.VMEM((tm, d), BF16)],
        compiler_params=_params("parallel"),
        name="even_inproj_rotary",
    )(x, g, w_all, pos_b, invf)


def _attn_body(pairs_per_kv, q_ref, kc_ref, kp_ref, vc_ref, vp_ref, sink_ref, o_ref):
    m_step = pl.program_id(1)
    blk = ATTN_BLOCK
    kk = jnp.concatenate([kp_ref[...], kc_ref[...]], axis=0)
    vv = jnp.concatenate([vp_ref[...], vc_ref[...]], axis=0)
    lower = lax.broadcasted_iota(jnp.int32, (blk, LANES), 1) < HEAD_DIM
    key = lax.broadcasted_iota(jnp.int32, (2 * blk, 2 * blk), 0)
    qrow = lax.broadcasted_iota(jnp.int32, (2 * blk, 2 * blk), 1) % blk
    band = (key > qrow) & (key <= qrow + blk)
    upper_rows = lax.broadcasted_iota(jnp.int32, (LANES, blk), 0) >= HEAD_DIM
    n_pairs = q_ref.shape[-1] // LANES
    scale = 1.0 / math.sqrt(HEAD_DIM)

    def scores(t, c):
        hk = c // pairs_per_kv
        qc = q_ref[t * blk:(t + 1) * blk, c * LANES:(c + 1) * LANES] * scale
        q2 = jnp.concatenate([jnp.where(lower, qc, jnp.zeros_like(qc)),
                              jnp.where(lower, jnp.zeros_like(qc), qc)], axis=0)
        kh = kk[t * blk:t * blk + 2 * blk, hk * LANES:(hk + 1) * LANES]
        return lax.dot_general(kh, q2, (((1,), (1,)), ((), ())), preferred_element_type=F32)

    def finish(t, c, s):
        hk = c // pairs_per_kv
        first = (m_step == 0) if t == 0 else False
        valid = band & ((key >= blk) | jnp.logical_not(first))
        s = jnp.where(valid, s, NEG_INF)
        sink = sink_ref[c]
        mx = jnp.maximum(jnp.max(s, axis=0, keepdims=True), sink)
        p = jnp.exp(s - mx)
        den = jnp.sum(p, axis=0, keepdims=True) + jnp.exp(sink - mx)
        p = (p * (1.0 / den)).astype(BF16)
        vh = vv[t * blk:t * blk + 2 * blk, hk * LANES:(hk + 1) * LANES]
        r = lax.dot_general(vh, p, (((0,), (0,)), ((), ())), preferred_element_type=F32)
        o_t = jnp.where(upper_rows, r[:, blk:], r[:, :blk])
        o_ref[t * blk:(t + 1) * blk, c * LANES:(c + 1) * LANES] = o_t.T.astype(BF16)

    units = [(t, c) for t in range(ATTN_BLOCKS_PER_STEP) for c in range(n_pairs)]
    ahead = 4
    pending = [scores(*u) for u in units[:ahead]]
    for n, (t, c) in enumerate(units):
        if n + ahead < len(units):
            pending.append(scores(*units[n + ahead]))
        finish(t, c, pending.pop(0))


def _attention(q, kd, vd, sink_rows_all, e, *, batch, seq, pairs_per_kv):
    t, qw = q.shape
    kw = kd.shape[-1]
    rows = ATTN_BLOCK * ATTN_BLOCKS_PER_STEP
    steps = seq // rows
    bps = ATTN_BLOCKS_PER_STEP
    nblk = seq // ATTN_BLOCK

    def cur(b, m):
        return (b * steps + m, 0)

    def prev(b, m):
        return (b * nblk + jnp.maximum(m * bps - 1, 0), 0)

    return pl.pallas_call(
        functools.partial(_attn_body, pairs_per_kv),
        grid=(batch, steps),
        in_specs=[
            pl.BlockSpec((rows, qw), cur),
            pl.BlockSpec((rows, kw), cur),
            pl.BlockSpec((ATTN_BLOCK, kw), prev),
            pl.BlockSpec((rows, kw), cur),
            pl.BlockSpec((ATTN_BLOCK, kw), prev),
            pl.BlockSpec((qw // LANES,) + sink_rows_all.shape[1:], lambda b, m: (e, 0, 0)),
        ],
        out_specs=pl.BlockSpec((rows, qw), cur),
        out_shape=jax.ShapeDtypeStruct((t, qw), BF16),
        compiler_params=_params("parallel", "arbitrary"),
        name="swa_attention",
    )(q, kd, kd, vd, vd, sink_rows_all)


def _s5_discretise_body(are_ref, aim_ref, logdt_ref, bre_ref, bim_ref, abr_ref, abi_ref, bbr_ref, bbi_ref):
    are, aim = are_ref[...], aim_ref[...]
    dt = jnp.exp(logdt_ref[...])
    mag = jnp.exp(are * dt)
    abar_re, abar_im = mag * jnp.cos(aim * dt), mag * jnp.sin(aim * dt)
    nr, ni = abar_re - 1.0, abar_im
    den = are * are + aim * aim
    coef_re = (nr * are + ni * aim) / den
    coef_im = (ni * are - nr * aim) / den
    bre, bim = bre_ref[...], bim_ref[...]
    abr_ref[...] = abar_re
    abi_ref[...] = abar_im
    bbr_ref[...] = coef_re * bre - coef_im * bim
    bbi_ref[...] = coef_re * bim + coef_im * bre


def _s5_discretise(a_re, a_im, log_dt, b_re, b_im):
    g, p = a_re.shape
    c = b_re.shape[-1]
    col = lambda v: v.reshape(g * p, 1)
    logdt_col = jnp.broadcast_to(log_dt[:, None], (g, p)).reshape(g * p, 1)
    outs = pl.pallas_call(
        _s5_discretise_body,
        out_shape=[jax.ShapeDtypeStruct((g * p, 1), F32)] * 2 + [jax.ShapeDtypeStruct((g * p, c), F32)] * 2,
        name="s5_discretise",
    )(col(a_re), col(a_im), logdt_col, b_re.reshape(g * p, c), b_im.reshape(g * p, c))
    abr, abi, bbr, bbi = outs
    return abr.reshape(g, p), abi.reshape(g, p), bbr.reshape(g, p, c), bbi.reshape(g, p, c)


def _cmul(ar, ai, br, bi):
    return ar * br - ai * bi, ar * bi + ai * br


def _s5_body(n_tiles, steps, u_ref, wb_ref, wc_ref, are_ref, aim_ref, d_ref, wglu_ref, bglu_ref,
             o_ref, us_ref, ys_ref, carry_ref, pw_ref, p8_ref):
    blk = pl.program_id(1)
    sw = are_ref.shape[-1]
    cw = wc_ref.shape[-1]
    nslab = us_ref.shape[0]
    nsub = SUBLANES

    @pl.when(blk == 0)
    def _():
        carry_ref[...] = jnp.zeros_like(carry_ref)
        for kt in range(n_tiles):
            ar, ai = are_ref[kt], aim_ref[kt]
            pr, pi = ar, ai
            for j in range(steps):
                pw_ref[kt, j:j + 1, :sw] = pr
                pw_ref[kt, j:j + 1, sw:] = pi
                if j + 1 < steps:
                    pr, pi = _cmul(pr, pi, ar, ai)
            qr, qi = pr, pi
            for s in range(nsub):
                p8_ref[kt, s:s + 1, :sw] = qr
                p8_ref[kt, s:s + 1, sw:] = qi
                if s + 1 < nsub:
                    qr, qi = _cmul(qr, qi, pr, pi)

    for s in range(nsub):
        for c in range(nslab):
            us_ref[c, pl.ds(s, steps, stride=nsub), :] = u_ref[s * steps:(s + 1) * steps, c * LANES:(c + 1) * LANES]

    def u_cols(c0, n):
        return jnp.concatenate([us_ref[c0 + i] for i in range(n)], axis=1)

    def input_drive(kt):
        ub = u_cols(kt * (cw // LANES), cw // LANES).astype(BF16)
        return jnp.dot(ub, wb_ref[kt], preferred_element_type=F32)

    srow = lax.broadcasted_iota(jnp.int32, (nsub, sw), 0)

    def scan_tile(kt, bu):
        ar = jnp.broadcast_to(are_ref[kt], (nsub, sw))
        ai = jnp.broadcast_to(aim_ref[kt], (nsub, sw))
        hr, hi = jnp.zeros((nsub, sw), F32), jnp.zeros((nsub, sw), F32)
        zero_start = []
        for j in range(steps):
            r0 = j * nsub
            pr, pi = _cmul(ar, ai, hr, hi)
            hr, hi = pr + bu[r0:r0 + nsub, :sw], pi + bu[r0:r0 + nsub, sw:]
            zero_start.append((hr, hi))
        xr, xi = hr, hi
        for d in (1, 2, 4):
            pr, pi = p8_ref[kt, d - 1:d, :sw], p8_ref[kt, d - 1:d, sw:]
            sr = jnp.where(srow >= d, pltpu.roll(xr, d, axis=0), 0.0)
            si = jnp.where(srow >= d, pltpu.roll(xi, d, axis=0), 0.0)
            qr, qi = _cmul(pr, pi, sr, si)
            xr, xi = xr + qr, xi + qi
        cr, ci = carry_ref[kt, :, :sw], carry_ref[kt, :, sw:]
        qr, qi = _cmul(p8_ref[kt, :, :sw], p8_ref[kt, :, sw:], cr, ci)
        xr, xi = xr + qr, xi + qi
        init_r = jnp.where(srow == 0, cr, pltpu.roll(xr, 1, axis=0))
        init_i = jnp.where(srow == 0, ci, pltpu.roll(xi, 1, axis=0))
        carry_ref[kt, :, :sw] = jnp.broadcast_to(xr[nsub - 1:nsub], (nsub, sw))
        carry_ref[kt, :, sw:] = jnp.broadcast_to(xi[nsub - 1:nsub], (nsub, sw))
        rows = []
        for j in range(steps):
            pr, pi = _cmul(pw_ref[kt, j:j + 1, :sw], pw_ref[kt, j:j + 1, sw:], init_r, init_i)
            rows.append(jnp.concatenate([zero_start[j][0] + pr, zero_start[j][1] + pi], axis=1))
        return jnp.concatenate(rows, axis=0)

    bus = [input_drive(kt) for kt in range(n_tiles)]
    ys = [jnp.dot(scan_tile(kt, bus[kt]).astype(BF16), wc_ref[kt], preferred_element_type=F32)
          for kt in range(n_tiles)]
    y = jnp.concatenate(ys, axis=1) + d_ref[...] * u_cols(0, nslab)
    gl = jax.nn.gelu(y)
    gate = jnp.dot(gl.astype(BF16), wglu_ref[...], preferred_element_type=F32) + bglu_ref[...]
    out = gl * jax.nn.sigmoid(gate)
    for c in range(nslab):
        ys_ref[c] = out[:, c * LANES:(c + 1) * LANES]
    for s in range(nsub):
        for c in range(nslab):
            o_ref[s * steps:(s + 1) * steps, c * LANES:(c + 1) * LANES] = (
                ys_ref[c, pl.ds(s, steps, stride=nsub), :].astype(BF16))


def _s5(u, wb, wc, a_re_t, a_im_t, d_rows, w_glu_all, b_glu_rows, e, n_tiles, *, batch, seq, steps=32):
    t, width = u.shape
    _, cw, sw2 = wb.shape
    rows_per_block = steps * SUBLANES
    nblk = seq // rows_per_block
    layer3 = lambda b, k: (e, 0, 0)
    row = lambda b, k: (b * nblk + k, 0)
    return pl.pallas_call(
        functools.partial(_s5_body, n_tiles, steps),
        grid=(batch, nblk),
        in_specs=[
            pl.BlockSpec((rows_per_block, width), row),
            pl.BlockSpec((n_tiles,) + wb.shape[1:], layer3),
            pl.BlockSpec((n_tiles,) + wc.shape[1:], layer3),
            pl.BlockSpec((n_tiles,) + a_re_t.shape[1:], layer3),
            pl.BlockSpec((n_tiles,) + a_im_t.shape[1:], layer3),
            pl.BlockSpec((None,) + d_rows.shape[1:], layer3),
            pl.BlockSpec((None,) + w_glu_all.shape[1:], layer3),
            pl.BlockSpec((None,) + b_glu_rows.shape[1:], layer3),
        ],
        out_specs=pl.BlockSpec((rows_per_block, width), row),
        out_shape=jax.ShapeDtypeStruct((t, width), BF16),
        scratch_shapes=[
            pltpu.VMEM((width // LANES, rows_per_block, LANES), F32),
            pltpu.VMEM((width // LANES, rows_per_block, LANES), F32),
            pltpu.VMEM((n_tiles, SUBLANES, sw2), F32),
            pltpu.VMEM((n_tiles, steps, sw2), F32),
            pltpu.VMEM((n_tiles, SUBLANES, sw2), F32),
        ],
        compiler_params=_params("parallel", "arbitrary"),
        name="s5_scan_glu",
    )(u, wb, wc, a_re_t, a_im_t, d_rows, w_glu_all, b_glu_rows)


HALO_ROWS = 16


def _odd_inproj_body(n_plain, tiles_per_seq, x_ref, xh_ref, g_ref, w_ref, wdt_ref, cw_ref, cb_ref,
                     o_ref, dt_ref, xn_ref):
    i, j = pl.program_id(0), pl.program_id(1)

    @pl.when(j == 0)
    def _():
        g = g_ref[...]
        xn_ref[0:HALO_ROWS, :] = _rms(xh_ref[...], g).astype(BF16)
        xn_ref[HALO_ROWS:, :] = _rms(x_ref[...], g).astype(BF16)
        dt_ref[...] = jnp.dot(xn_ref[HALO_ROWS:, :], wdt_ref[...], preferred_element_type=F32)

    @pl.when(j < n_plain)
    def _():
        o_ref[...] = jnp.dot(xn_ref[HALO_ROWS:, :], w_ref[...], preferred_element_type=F32)

    @pl.when(j >= n_plain)
    def _():
        p = jnp.dot(xn_ref[...], w_ref[...], preferred_element_type=F32)
        seq_start = (i % tiles_per_seq) == 0
        halo = jnp.where(seq_start, 0.0, p[:HALO_ROWS])
        ext = jnp.concatenate([halo, p[HALO_ROWS:]], axis=0)
        acc = cb_ref[...] + cw_ref[M_CONV - 1:M_CONV, :] * p[HALO_ROWS:]
        for k in range(1, M_CONV):
            acc = acc + cw_ref[M_CONV - 1 - k:M_CONV - k, :] * ext[HALO_ROWS - k:ext.shape[0] - k]
        o_ref[...] = jax.nn.silu(acc)


def _odd_inproj(x, g, w, layer, n, wdt, conv_w, conv_b, *, seq, n_plain_cols, tm=1024, tn=1024):
    t, d = x.shape
    assert seq % tm == 0 and n_plain_cols % tn == 0 and n % tn == 0
    n_plain = n_plain_cols // tn
    halo_blocks = tm // HALO_ROWS
    conv_col = lambda i, j: (0, jnp.maximum(j - n_plain, 0))
    return pl.pallas_call(
        functools.partial(_odd_inproj_body, n_plain, seq // tm),
        grid=(t // tm, n // tn),
        in_specs=[
            pl.BlockSpec((tm, d), lambda i, j: (i, 0)),
            pl.BlockSpec((HALO_ROWS, d), lambda i, j: (jnp.maximum(i * halo_blocks - 1, 0), 0)),
            pl.BlockSpec((1, d), lambda i, j: (0, 0)),
            pl.BlockSpec((None, d, tn), lambda i, j: (layer, 0, j)),
            pl.BlockSpec((None,) + wdt.shape[1:], lambda i, j: (layer, 0, 0)),
            pl.BlockSpec((M_CONV, tn), conv_col),
            pl.BlockSpec((1, tn), conv_col),
        ],
        out_specs=[
            pl.BlockSpec((tm, tn), lambda i, j: (i, j)),
            pl.BlockSpec((tm, wdt.shape[-1]), lambda i, j: (i, 0)),
        ],
        out_shape=[jax.ShapeDtypeStruct((t, n), F32), jax.ShapeDtypeStruct((t, wdt.shape[-1]), F32)],
        scratch_shapes=[pltpu.VMEM((HALO_ROWS + tm, d), BF16)],
        compiler_params=_params("parallel", "arbitrary"),
        name="odd_inproj_conv",
    )(x, x, g, w, wdt, conv_w, conv_b)


def _ssd_body(n_groups, z_ref, xs_ref, bm_ref, cm_ref, dt_ref, dtb_ref, alog_ref, dsk_ref, ng_ref,
              o_ref, state_ref, y_ref):
    c = pl.program_id(1)
    L = M_CHUNK
    inner = xs_ref.shape[-1]
    gw = bm_ref.shape[-1] // n_groups
    hw = inner // n_groups

    @pl.when(c == 0)
    def _():
        state_ref[...] = jnp.zeros_like(state_ref)

    dt = jax.nn.softplus(dt_ref[...] + dtb_ref[...])
    a = -jnp.exp(alog_ref[...])
    acum = a * dt
    rowi = lax.broadcasted_iota(jnp.int32, acum.shape, 0)
    k = 1
    while k < L:
        acum = acum + jnp.where(rowi >= k, pltpu.roll(acum, k, axis=0), 0.0)
        k *= 2
    src = acum - jnp.log(dt)
    src_t = src.T
    a_last = acum[L - 1:L, :]

    li = lax.broadcasted_iota(jnp.int32, (L, L), 0)
    si = lax.broadcasted_iota(jnp.int32, (L, L), 1)
    causal = li >= si
    lane = lax.broadcasted_iota(jnp.int32, (L, LANES), 1)
    lower = lane < M_HEAD_DIM
    heads_per_group = hw // M_HEAD_DIM

    for g in range(n_groups):
        cgb = cm_ref[:, g * gw:(g + 1) * gw].astype(BF16)
        bgb = bm_ref[:, g * gw:(g + 1) * gw].astype(BF16)
        cb = lax.dot_general(cgb, bgb, (((1,), (1,)), ((), ())), preferred_element_type=F32)
        prev = state_ref[:, g * hw:(g + 1) * hw]
        y_off = jnp.dot(cgb, prev.astype(BF16), preferred_element_type=F32)
        xd_parts = []
        for pr in range(heads_per_group // 2):
            h0 = g * heads_per_group + 2 * pr
            col0 = g * hw + pr * LANES
            xp = xs_ref[:, col0:col0 + LANES]
            ms = []
            for h in (h0, h0 + 1):
                seg = acum[:, h:h + 1] - src_t[h:h + 1, :]
                ms.append((cb * jnp.exp(jnp.where(causal, seg, -jnp.inf))).astype(BF16))
            lhs = jnp.concatenate(ms, axis=1)
            xpb = xp.astype(BF16)
            rhs = jnp.concatenate([jnp.where(lower, xpb, jnp.zeros_like(xpb)),
                                   jnp.where(lower, jnp.zeros_like(xpb), xpb)], axis=0)
            y_diag = jnp.dot(lhs, rhs, preferred_element_type=F32)
            al = jnp.where(lower, acum[:, h0:h0 + 1], acum[:, h0 + 1:h0 + 2])
            srcl = jnp.where(lower, src[:, h0:h0 + 1], src[:, h0 + 1:h0 + 2])
            alast = jnp.where(lower[0:1], a_last[:, h0:h0 + 1], a_last[:, h0 + 1:h0 + 2])
            y_ref[:, col0:col0 + LANES] = (y_diag + jnp.exp(al) * y_off[:, pr * LANES:(pr + 1) * LANES]
                                           + dsk_ref[:, col0:col0 + LANES] * xp)
            xd_parts.append((xp * jnp.exp(alast - srcl)).astype(BF16))
            state_ref[:, col0:col0 + LANES] = state_ref[:, col0:col0 + LANES] * jnp.exp(alast)
        xd = jnp.concatenate(xd_parts, axis=1)
        state_ref[:, g * hw:(g + 1) * hw] += lax.dot_general(
            bgb, xd, (((0,), (0,)), ((), ())), preferred_element_type=F32)

    for g in range(n_groups):
        sl = slice(g * hw, (g + 1) * hw)
        y = y_ref[:, sl] * jax.nn.silu(z_ref[:, sl])
        y = y * lax.rsqrt(jnp.mean(y * y, axis=-1, keepdims=True) + NORM_EPS)
        o_ref[:, sl] = (y * ng_ref[:, sl]).astype(BF16)


def _ssd(zxbc, dt_raw, dt_bias, a_log, d_row, norm_g, *, batch, seq, inner, n_groups):
    t = zxbc.shape[0]
    nchunks = seq // M_CHUNK
    gn = n_groups * M_STATE
    row = lambda b, c: (b * nchunks + c, 0)
    const = lambda b, c: (0, 0)
    return pl.pallas_call(
        functools.partial(_ssd_body, n_groups),
        grid=(batch, nchunks),
        in_specs=[
            pl.BlockSpec((M_CHUNK, inner), row),
            pl.BlockSpec((M_CHUNK, inner), lambda b, c: (b * nchunks + c, 1)),
            pl.BlockSpec((M_CHUNK, gn), lambda b, c: (b * nchunks + c, 2 * inner // gn)),
            pl.BlockSpec((M_CHUNK, gn), lambda b, c: (b * nchunks + c, 2 * inner // gn + 1)),
            pl.BlockSpec((M_CHUNK, LANES), row),
            pl.BlockSpec((1, LANES), const),
            pl.BlockSpec((1, LANES), const),
            pl.BlockSpec((1, inner), const),
            pl.BlockSpec((1, inner), const),
        ],
        out_specs=pl.BlockSpec((M_CHUNK, inner), row),
        out_shape=jax.ShapeDtypeStruct((t, inner), BF16),
        scratch_shapes=[pltpu.VMEM((M_STATE, inner), F32),
                        pltpu.VMEM((M_CHUNK, inner), F32)],
        compiler_params=_params("parallel", "arbitrary"),
        name="ssd_chunk",
    )(zxbc, zxbc, zxbc, zxbc, dt_raw, dt_bias, a_log, d_row, norm_g)


def _cast_body(wt_ref, o_ref):
    o_ref[...] = wt_ref[...].T.astype(BF16)


def _cast_leading_cols(w, n_cols, *, tn=1024):
    layers, rows, _ = w.shape
    assert n_cols % tn == 0
    return pl.pallas_call(
        _cast_body,
        grid=(layers, n_cols // tn),
        in_specs=[pl.BlockSpec((None, tn, rows), lambda l, j: (l, j, 0))],
        out_specs=pl.BlockSpec((None, rows, tn), lambda l, j: (l, 0, j)),
        out_shape=jax.ShapeDtypeStruct((layers, rows, n_cols), BF16),
        compiler_params=_params("parallel", "parallel"),
        name="cast_weight_columns",
    )(jnp.swapaxes(w, 1, 2))


def _block_diag_tiles(blocks, n_tiles):
    g, r, c = blocks.shape
    gpt = g // n_tiles
    eye = jnp.eye(gpt, dtype=blocks.dtype)
    b = blocks.reshape(n_tiles, gpt, r, c)
    out = jnp.einsum('tgrc,gh->tgrhc', b, eye)
    return out.reshape(n_tiles, gpt * r, gpt * c)


def _rope_inv_freq_row():
    half = ROPE_DIM // 2
    inv_freq = jnp.exp(-math.log(ROPE_THETA) * jnp.arange(half, dtype=F32) * (2.0 / ROPE_DIM))
    lane = jnp.arange(LANES) % HEAD_DIM
    return jnp.where(lane < ROPE_DIM, inv_freq[lane % half], 0.0).reshape(1, LANES).astype(F32)


def _even_prep(w_in, sinks, a_re, a_im, log_dt, b_re, b_im, c_re, c_im, d_skip, w_glu, b_glu, w_out):
    n_layers, d, _ = w_in.shape
    n_q = sinks.shape[-1]
    a_width = n_q * HEAD_DIM
    groups, state = a_re.shape[1:]
    s5_width = groups * S5_GROUP
    kv_width = (w_in.shape[-1] - a_width - s5_width) // 2
    n_kv = kv_width // HEAD_DIM
    grp = n_q // n_kv
    assert grp % 2 == 0 and a_width % LANES == 0 and a_width == s5_width

    assert kv_width % LANES == 0
    w_all = w_in.astype(BF16)

    sink_rows = jnp.repeat(sinks.astype(F32).reshape(n_layers * n_q // 2, 2), ATTN_BLOCK, axis=1)
    sink_rows = sink_rows.reshape(n_layers * n_q // 2, 1, 2 * ATTN_BLOCK)

    fold = lambda v: v.astype(F32).reshape((n_layers * groups,) + v.shape[2:])
    abr, abi, bbr, bbi = _s5_discretise(fold(a_re), fold(a_im), fold(log_dt), fold(b_re), fold(b_im))
    n_tiles = groups // S5_GROUPS_PER_TILE
    all_tiles = n_layers * n_tiles
    to_in = lambda bb: _block_diag_tiles(jnp.swapaxes(bb, 1, 2), all_tiles)
    wb = jnp.concatenate([to_in(bbr), to_in(bbi)], axis=-1).astype(BF16)
    to_out = lambda cc: _block_diag_tiles(jnp.swapaxes(fold(cc), 1, 2), all_tiles)
    wc = jnp.concatenate([to_out(c_re), -to_out(c_im)], axis=1).astype(BF16)
    return dict(
        w_all=w_all, sink_rows=sink_rows, wb=wb, wc=wc, n_tiles=n_tiles,
        a_re_t=abr.reshape(all_tiles, 1, -1), a_im_t=abi.reshape(all_tiles, 1, -1),
        d_rows=d_skip.astype(F32).reshape(n_layers, 1, s5_width), b_glu_rows=b_glu.astype(F32).reshape(n_layers, 1, s5_width),
        w_glu=w_glu.astype(BF16), w_out=w_out.astype(BF16),
        a_width=a_width, kv_width=kv_width, s5_width=s5_width, pairs_per_kv=grp // 2)


def _even_mixer(x, pos_b, norm_g, p, e, *, batch, seq):
    d = x.shape[-1]
    a_width, kv_width, s5_width = p["a_width"], p["kv_width"], p["s5_width"]
    q, kd, vd, u = _even_inproj(x, norm_g, p["w_all"], e, pos_b, _rope_inv_freq_row(),
                                qw=a_width, kw=2 * kv_width, vw=2 * kv_width, uw=s5_width)
    attn = _attention(q, kd, vd, p["sink_rows"], e, batch=batch, seq=seq, pairs_per_kv=p["pairs_per_kv"])
    ssm = _s5(u, p["wb"], p["wc"], p["a_re_t"], p["a_im_t"], p["d_rows"], p["w_glu"], p["b_glu_rows"], e,
              p["n_tiles"], batch=batch, seq=seq)
    tm = 512
    terms = [
        (attn, pl.BlockSpec((tm, a_width), lambda i: (i, 0)),
         p["w_out"], pl.BlockSpec((None, a_width, d), lambda i: (e, 0, 0))),
        (ssm, pl.BlockSpec((tm, s5_width), lambda i: (i, 0)),
         p["w_out"], pl.BlockSpec((None, s5_width, d), lambda i: (e, 1, 0))),
    ]
    return _outproj(x, terms, tm=tm)


def _odd_mixer(x, norm_g, w_main_all, w_dt_all, conv_w, conv_b, dt_bias, a_log, d_skip, norm_out, w_out_all, o, *,
               batch, seq):
    d = x.shape[-1]
    inner = norm_out.shape[0]
    heads = a_log.shape[0]
    conv_dim = conv_w.shape[-1]
    n_groups = (conv_dim - inner) // (2 * M_STATE)
    assert heads <= LANES and inner // heads == M_HEAD_DIM
    pad = LANES - heads
    zxbc, dt_raw = _odd_inproj(x, norm_g, w_main_all, o, inner + conv_dim, w_dt_all, conv_w.astype(F32),
                               conv_b.astype(F32).reshape(1, conv_dim), seq=seq, n_plain_cols=inner)
    row = lambda v: jnp.pad(v.astype(F32), (0, pad)).reshape(1, LANES)
    y = _ssd(zxbc, dt_raw, row(dt_bias), row(a_log), jnp.repeat(d_skip.astype(F32), M_HEAD_DIM).reshape(1, inner),
             norm_out.astype(F32).reshape(1, inner), batch=batch, seq=seq, inner=inner, n_groups=n_groups)
    tm = 512
    return _outproj(x, [(y, pl.BlockSpec((tm, inner), lambda i: (i, 0)),
                         w_out_all, pl.BlockSpec((None, inner, d), lambda i: (o, 0, 0)))], tm=tm)


def kernel(x, positions, norm_ffn1, ffn1_gate, ffn1_up, ffn1_down, norm_mix, norm_ffn2, ffn2_gate, ffn2_up,
           ffn2_down, ev_w_in, ev_sinks, s5_a_re, s5_a_im, s5_log_dt, s5_b_re, s5_b_im, s5_c_re, s5_c_im, s5_d,
           s5_w_glu, s5_b_glu, ev_w_out, m_w_in, m_conv_w, m_conv_b, m_dt_bias, m_a_log, m_d, m_norm, m_w_out,
           final_norm):
    batch, seq, d = x.shape
    depth = norm_ffn1.shape[0]
    t = batch * seq
    h = x.reshape(t, d).astype(F32)
    pos_b = jnp.broadcast_to(positions.reshape(t, 1).astype(F32), (t, LANES))
    g1, g2, gm = (v.astype(F32).reshape(depth, 1, d) for v in (norm_ffn1, norm_ffn2, norm_mix))
    even = _even_prep(ev_w_in, ev_sinks, s5_a_re, s5_a_im, s5_log_dt, s5_b_re, s5_b_im, s5_c_re, s5_c_im, s5_d,
                      s5_w_glu, s5_b_glu, ev_w_out)
    m_w_out_all = m_w_out.astype(BF16)
    m_main = m_norm.shape[-1] + m_conv_w.shape[-1]
    m_w_main_all = _cast_leading_cols(m_w_in, m_main)
    m_w_dt_all = jnp.pad(m_w_in[:, :, m_main:], ((0, 0), (0, 0), (0, LANES - m_a_log.shape[-1]))).astype(BF16)
    ffn_f32 = [(ffn1_gate, ffn1_up, ffn1_down), (ffn2_gate, ffn2_up, ffn2_down)]
    w_cur = tuple(w[0].astype(BF16) for w in ffn_f32[0])
    for layer in range(depth):
        h, w_cur = _ffn(h, g1, *w_cur, layer, next_f32=ffn_f32[1] + (layer,))
        if layer % 2 == 0:
            e = layer // 2
            h = _even_mixer(h, pos_b, gm[layer], even, e, batch=batch, seq=seq)
        else:
            o = layer // 2
            h = _odd_mixer(h, gm[layer], m_w_main_all, m_w_dt_all, m_conv_w[o], m_conv_b[o], m_dt_bias[o], m_a_log[o],
                           m_d[o], m_norm[o], m_w_out_all, o, batch=batch, seq=seq)
        if layer == depth - 1:
            h, _ = _ffn(h, g2, *w_cur, layer, final_g=final_norm.astype(F32).reshape(1, d))
        else:
            h, w_cur = _ffn(h, g2, *w_cur, layer, next_f32=ffn_f32[0] + (layer + 1,))
    return h.reshape(batch, seq, d).astype(x.dtype)
```

```python
import functools
import math

import jax
import jax.numpy as jnp
from jax import lax
from jax.experimental import pallas as pl
from jax.experimental.pallas import tpu as pltpu

F32 = jnp.float32
BF16 = jnp.bfloat16

NORM_EPS = 1e-5
NEG_INF = -1e30
LANES = 128
SUBLANES = 8
VMEM_LIMIT_BYTES = 60 * 1024 * 1024

HEAD_DIM = 64
ATTN_BLOCK = 128
ROPE_THETA = 500000.0
ROPE_DIM = HEAD_DIM // 4
ATTN_BLOCKS_PER_STEP = 4
S5_GROUP = 16
S5_STATE = 64
S5_GROUPS_PER_TILE = 16
M_HEAD_DIM = 64
M_STATE = 128
M_CHUNK = 128
M_CONV = 4


def _params(*sem):
    return pltpu.CompilerParams(dimension_semantics=sem, vmem_limit_bytes=VMEM_LIMIT_BYTES)


def _rms(x, g):
    return x * lax.rsqrt(jnp.mean(x * x, axis=-1, keepdims=True) + NORM_EPS) * g


def _ffn_body(final, cast_next, n_chunks, x_ref, g_ref, wg_ref, wu_ref, wd_ref, *rest):
    xn_ref, h_ref = rest[-2:]
    o_ref = rest[-6] if cast_next else rest[-3]
    j = pl.program_id(1)
    last = n_chunks
    if cast_next:
        n_in = 1 if final else 0
        for src, dst in zip(rest[n_in:n_in + 3], rest[-5:-2]):
            dst[...] = src[...].astype(BF16)

    def up_proj(slot):
        xn = xn_ref[...]
        gate = jnp.dot(xn, wg_ref[...], preferred_element_type=F32)
        up = jnp.dot(xn, wu_ref[...], preferred_element_type=F32)
        h_ref[slot] = (jax.nn.silu(gate) * up * 0.5).astype(BF16)

    def down_proj(slot):
        o_ref[...] += jnp.dot(h_ref[slot], wd_ref[...], preferred_element_type=F32)

    @pl.when(j == 0)
    def _():
        x = x_ref[...]
        xn_ref[...] = _rms(x, g_ref[...]).astype(BF16)
        o_ref[...] = x
        up_proj(0)

    for parity in (0, 1):
        @pl.when((j > 0) & (j < last) & (j % 2 == parity))
        def _():
            up_proj(parity)
            down_proj(1 - parity)

    @pl.when(j == last)
    def _():
        down_proj((n_chunks - 1) % 2)
        if final:
            o_ref[...] = _rms(o_ref[...], rest[0][...])


def _ffn(x, g, wg, wu, wd, layer, final_g=None, next_f32=None, *, tm=1024, tf=512):
    t, d = x.shape
    dff = wg.shape[-1]
    ni, nj = t // tm, dff // tf
    up_j = lambda j: jnp.minimum(j, nj - 1)
    down_j = lambda j: jnp.maximum(j - 1, 0)
    in_specs = [
        pl.BlockSpec((tm, d), lambda i, j: (i, 0)),
        pl.BlockSpec((None, 1, d), lambda i, j: (layer, 0, 0)),
        pl.BlockSpec((d, tf), lambda i, j: (0, up_j(j))),
        pl.BlockSpec((d, tf), lambda i, j: (0, up_j(j))),
        pl.BlockSpec((tf, d), lambda i, j: (down_j(j), 0)),
    ]
    args = [x, g, wg, wu, wd]
    out_specs = [pl.BlockSpec((tm, d), lambda i, j: (i, 0))]
    out_shape = [jax.ShapeDtypeStruct((t, d), F32)]
    if final_g is not None:
        in_specs.append(pl.BlockSpec((1, d), lambda i, j: (0, 0)))
        args.append(final_g)
    if next_f32 is not None:
        ng, nu, nd, nl = next_f32
        assert d % ni == 0 and dff % nj == 0
        up_blk, dn_blk = (d // ni, dff // nj), (dff // nj, d // ni)
        in_specs += [pl.BlockSpec((None,) + up_blk, lambda i, j: (nl, i, up_j(j))),
                     pl.BlockSpec((None,) + up_blk, lambda i, j: (nl, i, up_j(j))),
                     pl.BlockSpec((None,) + dn_blk, lambda i, j: (nl, up_j(j), i))]
        args += [ng, nu, nd]
        out_specs += [pl.BlockSpec(up_blk, lambda i, j: (i, up_j(j))), pl.BlockSpec(up_blk, lambda i, j: (i, up_j(j))),
                      pl.BlockSpec(dn_blk, lambda i, j: (up_j(j), i))]
        out_shape += [jax.ShapeDtypeStruct((d, dff), BF16)] * 2 + [jax.ShapeDtypeStruct((dff, d), BF16)]
    outs = pl.pallas_call(
        functools.partial(_ffn_body, final_g is not None, next_f32 is not None, nj),
        grid=(ni, nj + 1),
        in_specs=in_specs,
        out_specs=out_specs,
        out_shape=out_shape,
        scratch_shapes=[pltpu.VMEM((tm, d), BF16), pltpu.VMEM((2, tm, tf), BF16)],
        compiler_params=_params("parallel", "arbitrary"),
        name="swiglu_ffn",
    )(*args)
    return outs[0], tuple(outs[1:])


def _outproj_body(n_terms, x_ref, *refs):
    o_ref = refs[-1]
    acc = x_ref[...]
    for i in range(n_terms):
        acc = acc + jnp.dot(refs[2 * i][...], refs[2 * i + 1][...], preferred_element_type=F32)
    o_ref[...] = acc


def _outproj(x, terms, *, tm):
    t, d = x.shape
    in_specs = [pl.BlockSpec((tm, d), lambda i: (i, 0))]
    args = [x]
    for a, a_spec, w, w_spec in terms:
        in_specs += [a_spec, w_spec]
        args += [a, w]
    return pl.pallas_call(
        functools.partial(_outproj_body, len(terms)),
        grid=(t // tm,),
        in_specs=in_specs,
        out_specs=pl.BlockSpec((tm, d), lambda i: (i, 0)),
        out_shape=jax.ShapeDtypeStruct((t, d), F32),
        compiler_params=_params("parallel"),
        name="residual_outproj",
    )(*args)


def _even_inproj_body(n_rot, x_ref, g_ref, w_ref, pos_ref, invf_ref, q_ref, k_ref, v_ref, u_ref, xn_ref):
    xn_ref[...] = _rms(x_ref[...], g_ref[...]).astype(BF16)
    ang = pos_ref[...] * invf_ref[...]
    cos, sin = jnp.cos(ang), jnp.sin(ang)
    lane = lax.broadcasted_iota(jnp.int32, ang.shape, 1) % HEAD_DIM
    half = ROPE_DIM // 2
    sin_from_lower = jnp.where((lane >= half) & (lane < ROPE_DIM), sin, 0.0)
    sin_from_upper = jnp.where(lane < half, sin, 0.0)

    first_head = lax.broadcasted_iota(jnp.int32, ang.shape, 1) < HEAD_DIM

    def store_doubled(ref, c, t):
        swapped = pltpu.roll(t, HEAD_DIM, axis=1)
        ref[:, 2 * c:2 * c + LANES] = jnp.where(first_head, t, swapped).astype(BF16)
        ref[:, 2 * c + LANES:2 * c + 2 * LANES] = jnp.where(first_head, swapped, t).astype(BF16)

    cw = 512
    qw, kw, vw = q_ref.shape[-1], k_ref.shape[-1] // 2, v_ref.shape[-1] // 2
    for c0 in range(0, w_ref.shape[-1], cw):
        p = jnp.dot(xn_ref[...], w_ref[:, c0:c0 + cw], preferred_element_type=F32)
        for l0 in range(0, cw, LANES):
            col = c0 + l0
            t = p[:, l0:l0 + LANES]
            if col < n_rot:
                t = (t * cos + pltpu.roll(t, half, axis=1) * sin_from_lower
                     - pltpu.roll(t, LANES - half, axis=1) * sin_from_upper)
            if col < qw:
                q_ref[:, col:col + LANES] = t.astype(BF16)
            elif col < qw + kw:
                store_doubled(k_ref, col - qw, t)
            elif col < qw + kw + vw:
                store_doubled(v_ref, col - qw - kw, t)
            else:
                c = col - qw - kw - vw
                u_ref[:, c:c + LANES] = t


def _even_inproj(x, g, w_all, e, pos_b, invf, *, qw, kw, vw, uw, tm=512):
    t, d = x.shape
    return pl.pallas_call(
        functools.partial(_even_inproj_body, qw + kw // 2),
        grid=(t // tm,),
        in_specs=[
            pl.BlockSpec((tm, d), lambda i: (i, 0)),
            pl.BlockSpec((1, d), lambda i: (0, 0)),
            pl.BlockSpec((None,) + w_all.shape[1:], lambda i: (e, 0, 0)),
            pl.BlockSpec((tm, LANES), lambda i: (i, 0)),
            pl.BlockSpec((1, LANES), lambda i: (0, 0)),
        ],
        out_specs=[
            pl.BlockSpec((tm, qw), lambda i: (i, 0)),
            pl.BlockSpec((tm, kw), lambda i: (i, 0)),
            pl.BlockSpec((tm, vw), lambda i: (i, 0)),
            pl.BlockSpec((tm, uw), lambda i: (i, 0)),
        ],
        out_shape=[
            jax.ShapeDtypeStruct((t, qw), BF16),
            jax.ShapeDtypeStruct((t, kw), BF16),
            jax.ShapeDtypeStruct((t, vw), BF16),
            jax.ShapeDtypeStruct((t, uw), F32),
        ],
        scratch_shapes=[pltpu.VMEM((tm, d), BF16)],
        compiler_params=_params("parallel"),
        name="even_inproj_rotary",
    )(x, g, w_all, pos_b, invf)


def _attn_body(pairs_per_kv, q_ref, kc_ref, kp_ref, vc_ref, vp_ref, sink_ref, o_ref):
    m_step = pl.program_id(1)
    blk = ATTN_BLOCK
    kk = jnp.concatenate([kp_ref[...], kc_ref[...]], axis=0)
    vv = jnp.concatenate([vp_ref[...], vc_ref[...]], axis=0)
    lower = lax.broadcasted_iota(jnp.int32, (blk, LANES), 1) < HEAD_DIM
    key = lax.broadcasted_iota(jnp.int32, (2 * blk, 2 * blk), 0)
    qrow = lax.broadcasted_iota(jnp.int32, (2 * blk, 2 * blk), 1) % blk
    band = (key > qrow) & (key <= qrow + blk)
    upper_rows = lax.broadcasted_iota(jnp.int32, (LANES, blk), 0) >= HEAD_DIM
    n_pairs = q_ref.shape[-1] // LANES
    scale = 1.0 / math.sqrt(HEAD_DIM)

    def scores(t, c):
        hk = c // pairs_per_kv
        qc = q_ref[t * blk:(t + 1) * blk, c * LANES:(c + 1) * LANES] * scale
        q2 = jnp.concatenate([jnp.where(lower, qc, jnp.zeros_like(qc)),
                              jnp.where(lower, jnp.zeros_like(qc), qc)], axis=0)
        kh = kk[t * blk:t * blk + 2 * blk, hk * LANES:(hk + 1) * LANES]
        return lax.dot_general(kh, q2, (((1,), (1,)), ((), ())), preferred_element_type=F32)

    def finish(t, c, s):
        hk = c // pairs_per_kv
        first = (m_step == 0) if t == 0 else False
        valid = band & ((key >= blk) | jnp.logical_not(first))
        s = jnp.where(valid, s, NEG_INF)
        sink = sink_ref[c]
        mx = jnp.maximum(jnp.max(s, axis=0, keepdims=True), sink)
        p = jnp.exp(s - mx)
        den = jnp.sum(p, axis=0, keepdims=True) + jnp.exp(sink - mx)
        p = (p * (1.0 / den)).astype(BF16)
        vh = vv[t * blk:t * blk + 2 * blk, hk * LANES:(hk + 1) * LANES]
        r = lax.dot_general(vh, p, (((0,), (0,)), ((), ())), preferred_element_type=F32)
        o_t = jnp.where(upper_rows, r[:, blk:], r[:, :blk])
        o_ref[t * blk:(t + 1) * blk, c * LANES:(c + 1) * LANES] = o_t.T.astype(BF16)

    units = [(t, c) for t in range(ATTN_BLOCKS_PER_STEP) for c in range(n_pairs)]
    ahead = 4
    pending = [scores(*u) for u in units[:ahead]]
    for n, (t, c) in enumerate(units):
        if n + ahead < len(units):
            pending.append(scores(*units[n + ahead]))
        finish(t, c, pending.pop(0))


def _attention(q, kd, vd, sink_rows_all, e, *, batch, seq, pairs_per_kv):
    t, qw = q.shape
    kw = kd.shape[-1]
    rows = ATTN_BLOCK * ATTN_BLOCKS_PER_STEP
    steps = seq // rows
    bps = ATTN_BLOCKS_PER_STEP
    nblk = seq // ATTN_BLOCK

    def cur(b, m):
        return (b * steps + m, 0)

    def prev(b, m):
        return (b * nblk + jnp.maximum(m * bps - 1, 0), 0)

    return pl.pallas_call(
        functools.partial(_attn_body, pairs_per_kv),
        grid=(batch, steps),
        in_specs=[
            pl.BlockSpec((rows, qw), cur),
            pl.BlockSpec((rows, kw), cur),
            pl.BlockSpec((ATTN_BLOCK, kw), prev),
            pl.BlockSpec((rows, kw), cur),
            pl.BlockSpec((ATTN_BLOCK, kw), prev),
            pl.BlockSpec((qw // LANES,) + sink_rows_all.shape[1:], lambda b, m: (e, 0, 0)),
        ],
        out_specs=pl.BlockSpec((rows, qw), cur),
        out_shape=jax.ShapeDtypeStruct((t, qw), BF16),
        compiler_params=_params("parallel", "arbitrary"),
        name="swa_attention",
    )(q, kd, kd, vd, vd, sink_rows_all)


def _s5_discretise_body(are_ref, aim_ref, logdt_ref, bre_ref, bim_ref, abr_ref, abi_ref, bbr_ref, bbi_ref):
    are, aim = are_ref[...], aim_ref[...]
    dt = jnp.exp(logdt_ref[...])
    mag = jnp.exp(are * dt)
    abar_re, abar_im = mag * jnp.cos(aim * dt), mag * jnp.sin(aim * dt)
    nr, ni = abar_re - 1.0, abar_im
    den = are * are + aim * aim
    coef_re = (nr * are + ni * aim) / den
    coef_im = (ni * are - nr * aim) / den
    bre, bim = bre_ref[...], bim_ref[...]
    abr_ref[...] = abar_re
    abi_ref[...] = abar_im
    bbr_ref[...] = coef_re * bre - coef_im * bim
    bbi_ref[...] = coef_re * bim + coef_im * bre


def _s5_discretise(a_re, a_im, log_dt, b_re, b_im):
    g, p = a_re.shape
    c = b_re.shape[-1]
    col = lambda v: v.reshape(g * p, 1)
    logdt_col = jnp.broadcast_to(log_dt[:, None], (g, p)).reshape(g * p, 1)
    outs = pl.pallas_call(
        _s5_discretise_body,
        out_shape=[jax.ShapeDtypeStruct((g * p, 1), F32)] * 2 + [jax.ShapeDtypeStruct((g * p, c), F32)] * 2,
        name="s5_discretise",
    )(col(a_re), col(a_im), logdt_col, b_re.reshape(g * p, c), b_im.reshape(g * p, c))
    abr, abi, bbr, bbi = outs
    return abr.reshape(g, p), abi.reshape(g, p), bbr.reshape(g, p, c), bbi.reshape(g, p, c)


def _cmul(ar, ai, br, bi):
    return ar * br - ai * bi, ar * bi + ai * br


def _s5_body(n_tiles, steps, u_ref, wb_ref, wc_ref, are_ref, aim_ref, d_ref, wglu_ref, bglu_ref,
             o_ref, us_ref, ys_ref, carry_ref, pw_ref, p8_ref):
    blk = pl.program_id(1)
    sw = are_ref.shape[-1]
    cw = wc_ref.shape[-1]
    nslab = us_ref.shape[0]
    nsub = SUBLANES

    @pl.when(blk == 0)
    def _():
        carry_ref[...] = jnp.zeros_like(carry_ref)
        for kt in range(n_tiles):
            ar, ai = are_ref[kt], aim_ref[kt]
            pr, pi = ar, ai
            for j in range(steps):
                pw_ref[kt, j:j + 1, :sw] = pr
                pw_ref[kt, j:j + 1, sw:] = pi
                if j + 1 < steps:
                    pr, pi = _cmul(pr, pi, ar, ai)
            qr, qi = pr, pi
            for s in range(nsub):
                p8_ref[kt, s:s + 1, :sw] = qr
                p8_ref[kt, s:s + 1, sw:] = qi
                if s + 1 < nsub:
                    qr, qi = _cmul(qr, qi, pr, pi)

    for s in range(nsub):
        for c in range(nslab):
            us_ref[c, pl.ds(s, steps, stride=nsub), :] = u_ref[s * steps:(s + 1) * steps, c * LANES:(c + 1) * LANES]

    def u_cols(c0, n):
        return jnp.concatenate([us_ref[c0 + i] for i in range(n)], axis=1)

    def input_drive(kt):
        ub = u_cols(kt * (cw // LANES), cw // LANES).astype(BF16)
        return jnp.dot(ub, wb_ref[kt], preferred_element_type=F32)

    srow = lax.broadcasted_iota(jnp.int32, (nsub, sw), 0)

    def scan_tile(kt, bu):
        ar = jnp.broadcast_to(are_ref[kt], (nsub, sw))
        ai = jnp.broadcast_to(aim_ref[kt], (nsub, sw))
        hr, hi = jnp.zeros((nsub, sw), F32), jnp.zeros((nsub, sw), F32)
        zero_start = []
        for j in range(steps):
            r0 = j * nsub
            pr, pi = _cmul(ar, ai, hr, hi)
            hr, hi = pr + bu[r0:r0 + nsub, :sw], pi + bu[r0:r0 + nsub, sw:]
            zero_start.append((hr, hi))
        xr, xi = hr, hi
        for d in (1, 2, 4):
            pr, pi = p8_ref[kt, d - 1:d, :sw], p8_ref[kt, d - 1:d, sw:]
            sr = jnp.where(srow >= d, pltpu.roll(xr, d, axis=0), 0.0)
            si = jnp.where(srow >= d, pltpu.roll(xi, d, axis=0), 0.0)
            qr, qi = _cmul(pr, pi, sr, si)
            xr, xi = xr + qr, xi + qi
        cr, ci = carry_ref[kt, :, :sw], carry_ref[kt, :, sw:]
        qr, qi = _cmul(p8_ref[kt, :, :sw], p8_ref[kt, :, sw:], cr, ci)
        xr, xi = xr + qr, xi + qi
        init_r = jnp.where(srow == 0, cr, pltpu.roll(xr, 1, axis=0))
        init_i = jnp.where(srow == 0, ci, pltpu.roll(xi, 1, axis=0))
        carry_ref[kt, :, :sw] = jnp.broadcast_to(xr[nsub - 1:nsub], (nsub, sw))
        carry_ref[kt, :, sw:] = jnp.broadcast_to(xi[nsub - 1:nsub], (nsub, sw))
        rows = []
        for j in range(steps):
            pr, pi = _cmul(pw_ref[kt, j:j + 1, :sw], pw_ref[kt, j:j + 1, sw:], init_r, init_i)
            rows.append(jnp.concatenate([zero_start[j][0] + pr, zero_start[j][1] + pi], axis=1))
        return jnp.concatenate(rows, axis=0)

    bus = [input_drive(kt) for kt in range(n_tiles)]
    ys = [jnp.dot(scan_tile(kt, bus[kt]).astype(BF16), wc_ref[kt], preferred_element_type=F32)
          for kt in range(n_tiles)]
    y = jnp.concatenate(ys, axis=1) + d_ref[...] * u_cols(0, nslab)
    gl = jax.nn.gelu(y)
    gate = jnp.dot(gl.astype(BF16), wglu_ref[...], preferred_element_type=F32) + bglu_ref[...]
    out = gl * jax.nn.sigmoid(gate)
    for c in range(nslab):
        ys_ref[c] = out[:, c * LANES:(c + 1) * LANES]
    for s in range(nsub):
        for c in range(nslab):
            o_ref[s * steps:(s + 1) * steps, c * LANES:(c + 1) * LANES] = (
                ys_ref[c, pl.ds(s, steps, stride=nsub), :].astype(BF16))


def _s5(u, wb, wc, a_re_t, a_im_t, d_rows, w_glu_all, b_glu_rows, e, n_tiles, *, batch, seq, steps=32):
    t, width = u.shape
    _, cw, sw2 = wb.shape
    rows_per_block = steps * SUBLANES
    nblk = seq // rows_per_block
    layer3 = lambda b, k: (e, 0, 0)
    row = lambda b, k: (b * nblk + k, 0)
    return pl.pallas_call(
        functools.partial(_s5_body, n_tiles, steps),
        grid=(batch, nblk),
        in_specs=[
            pl.BlockSpec((rows_per_block, width), row),
            pl.BlockSpec((n_tiles,) + wb.shape[1:], layer3),
            pl.BlockSpec((n_tiles,) + wc.shape[1:], layer3),
            pl.BlockSpec((n_tiles,) + a_re_t.shape[1:], layer3),
            pl.BlockSpec((n_tiles,) + a_im_t.shape[1:], layer3),
            pl.BlockSpec((None,) + d_rows.shape[1:], layer3),
            pl.BlockSpec((None,) + w_glu_all.shape[1:], layer3),
            pl.BlockSpec((None,) + b_glu_rows.shape[1:], layer3),
        ],
        out_specs=pl.BlockSpec((rows_per_block, width), row),
        out_shape=jax.ShapeDtypeStruct((t, width), BF16),
        scratch_shapes=[
            pltpu.VMEM((width // LANES, rows_per_block, LANES), F32),
            pltpu.VMEM((width // LANES, rows_per_block, LANES), F32),
            pltpu.VMEM((n_tiles, SUBLANES, sw2), F32),
            pltpu.VMEM((n_tiles, steps, sw2), F32),
            pltpu.VMEM((n_tiles, SUBLANES, sw2), F32),
        ],
        compiler_params=_params("parallel", "arbitrary"),
        name="s5_scan_glu",
    )(u, wb, wc, a_re_t, a_im_t, d_rows, w_glu_all, b_glu_rows)


HALO_ROWS = 16


def _odd_inproj_body(n_plain, tiles_per_seq, x_ref, xh_ref, g_ref, w_ref, wdt_ref, cw_ref, cb_ref,
                     o_ref, dt_ref, xn_ref):
    i, j = pl.program_id(0), pl.program_id(1)

    @pl.when(j == 0)
    def _():
        g = g_ref[...]
        xn_ref[0:HALO_ROWS, :] = _rms(xh_ref[...], g).astype(BF16)
        xn_ref[HALO_ROWS:, :] = _rms(x_ref[...], g).astype(BF16)
        dt_ref[...] = jnp.dot(xn_ref[HALO_ROWS:, :], wdt_ref[...], preferred_element_type=F32)

    @pl.when(j < n_plain)
    def _():
        o_ref[...] = jnp.dot(xn_ref[HALO_ROWS:, :], w_ref[...], preferred_element_type=F32)

    @pl.when(j >= n_plain)
    def _():
        p = jnp.dot(xn_ref[...], w_ref[...], preferred_element_type=F32)
        seq_start = (i % tiles_per_seq) == 0
        halo = jnp.where(seq_start, 0.0, p[:HALO_ROWS])
        ext = jnp.concatenate([halo, p[HALO_ROWS:]], axis=0)
        acc = cb_ref[...] + cw_ref[M_CONV - 1:M_CONV, :] * p[HALO_ROWS:]
        for k in range(1, M_CONV):
            acc = acc + cw_ref[M_CONV - 1 - k:M_CONV - k, :] * ext[HALO_ROWS - k:ext.shape[0] - k]
        o_ref[...] = jax.nn.silu(acc)


def _odd_inproj(x, g, w, layer, n, wdt, conv_w, conv_b, *, seq, n_plain_cols, tm=1024, tn=1024):
    t, d = x.shape
    assert seq % tm == 0 and n_plain_cols % tn == 0 and n % tn == 0
    n_plain = n_plain_cols // tn
    halo_blocks = tm // HALO_ROWS
    conv_col = lambda i, j: (0, jnp.maximum(j - n_plain, 0))
    return pl.pallas_call(
        functools.partial(_odd_inproj_body, n_plain, seq // tm),
        grid=(t // tm, n // tn),
        in_specs=[
            pl.BlockSpec((tm, d), lambda i, j: (i, 0)),
            pl.BlockSpec((HALO_ROWS, d), lambda i, j: (jnp.maximum(i * halo_blocks - 1, 0), 0)),
            pl.BlockSpec((1, d), lambda i, j: (0, 0)),
            pl.BlockSpec((None, d, tn), lambda i, j: (layer, 0, j)),
            pl.BlockSpec((None,) + wdt.shape[1:], lambda i, j: (layer, 0, 0)),
            pl.BlockSpec((M_CONV, tn), conv_col),
            pl.BlockSpec((1, tn), conv_col),
        ],
        out_specs=[
            pl.BlockSpec((tm, tn), lambda i, j: (i, j)),
            pl.BlockSpec((tm, wdt.shape[-1]), lambda i, j: (i, 0)),
        ],
        out_shape=[jax.ShapeDtypeStruct((t, n), F32), jax.ShapeDtypeStruct((t, wdt.shape[-1]), F32)],
        scratch_shapes=[pltpu.VMEM((HALO_ROWS + tm, d), BF16)],
        compiler_params=_params("parallel", "arbitrary"),
        name="odd_inproj_conv",
    )(x, x, g, w, wdt, conv_w, conv_b)


def _ssd_body(n_groups, z_ref, xs_ref, bm_ref, cm_ref, dt_ref, dtb_ref, alog_ref, dsk_ref, ng_ref,
              o_ref, state_ref, y_ref):
    c = pl.program_id(1)
    L = M_CHUNK
    inner = xs_ref.shape[-1]
    gw = bm_ref.shape[-1] // n_groups
    hw = inner // n_groups

    @pl.when(c == 0)
    def _():
        state_ref[...] = jnp.zeros_like(state_ref)

    dt = jax.nn.softplus(dt_ref[...] + dtb_ref[...])
    a = -jnp.exp(alog_ref[...])
    acum = a * dt
    rowi = lax.broadcasted_iota(jnp.int32, acum.shape, 0)
    k = 1
    while k < L:
        acum = acum + jnp.where(rowi >= k, pltpu.roll(acum, k, axis=0), 0.0)
        k *= 2
    src = acum - jnp.log(dt)
    src_t = src.T
    a_last = acum[L - 1:L, :]

    li = lax.broadcasted_iota(jnp.int32, (L, L), 0)
    si = lax.broadcasted_iota(jnp.int32, (L, L), 1)
    causal = li >= si
    lane = lax.broadcasted_iota(jnp.int32, (L, LANES), 1)
    lower = lane < M_HEAD_DIM
    heads_per_group = hw // M_HEAD_DIM

    for g in range(n_groups):
        cgb = cm_ref[:, g * gw:(g + 1) * gw].astype(BF16)
        bgb = bm_ref[:, g * gw:(g + 1) * gw].astype(BF16)
        cb = lax.dot_general(cgb, bgb, (((1,), (1,)), ((), ())), preferred_element_type=F32)
        prev = state_ref[:, g * hw:(g + 1) * hw]
        y_off = jnp.dot(cgb, prev.astype(BF16), preferred_element_type=F32)
        xd_parts = []
        for pr in range(heads_per_group // 2):
            h0 = g * heads_per_group + 2 * pr
            col0 = g * hw + pr * LANES
            xp = xs_ref[:, col0:col0 + LANES]
            ms = []
            for h in (h0, h0 + 1):
                seg = acum[:, h:h + 1] - src_t[h:h + 1, :]
                ms.append((cb * jnp.exp(jnp.where(causal, seg, -jnp.inf))).astype(BF16))
            lhs = jnp.concatenate(ms, axis=1)
            xpb = xp.astype(BF16)
            rhs = jnp.concatenate([jnp.where(lower, xpb, jnp.zeros_like(xpb)),
                                   jnp.where(lower, jnp.zeros_like(xpb), xpb)], axis=0)
            y_diag = jnp.dot(lhs, rhs, preferred_element_type=F32)
            al = jnp.where(lower, acum[:, h0:h0 + 1], acum[:, h0 + 1:h0 + 2])
            srcl = jnp.where(lower, src[:, h0:h0 + 1], src[:, h0 + 1:h0 + 2])
            alast = jnp.where(lower[0:1], a_last[:, h0:h0 + 1], a_last[:, h0 + 1:h0 + 2])
            y_ref[:, col0:col0 + LANES] = (y_diag + jnp.exp(al) * y_off[:, pr * LANES:(pr + 1) * LANES]
                                           + dsk_ref[:, col0:col0 + LANES] * xp)
            xd_parts.append((xp * jnp.exp(alast - srcl)).astype(BF16))
            state_ref[:, col0:col0 + LANES] = state_ref[:, col0:col0 + LANES] * jnp.exp(alast)
        xd = jnp.concatenate(xd_parts, axis=1)
        state_ref[:, g * hw:(g + 1) * hw] += lax.dot_general(
            bgb, xd, (((0,), (0,)), ((), ())), preferred_element_type=F32)

    for g in range(n_groups):
        sl = slice(g * hw, (g + 1) * hw)
        y = y_ref[:, sl] * jax.nn.silu(z_ref[:, sl])
        y = y * lax.rsqrt(jnp.mean(y * y, axis=-1, keepdims=True) + NORM_EPS)
        o_ref[:, sl] = (y * ng_ref[:, sl]).astype(BF16)


def _ssd(zxbc, dt_raw, dt_bias, a_log, d_row, norm_g, *, batch, seq, inner, n_groups):
    t = zxbc.shape[0]
    nchunks = seq // M_CHUNK
    gn = n_groups * M_STATE
    row = lambda b, c: (b * nchunks + c, 0)
    const = lambda b, c: (0, 0)
    return pl.pallas_call(
        functools.partial(_ssd_body, n_groups),
        grid=(batch, nchunks),
        in_specs=[
            pl.BlockSpec((M_CHUNK, inner), row),
            pl.BlockSpec((M_CHUNK, inner), lambda b, c: (b * nchunks + c, 1)),
            pl.BlockSpec((M_CHUNK, gn), lambda b, c: (b * nchunks + c, 2 * inner // gn)),
            pl.BlockSpec((M_CHUNK, gn), lambda b, c: (b * nchunks + c, 2 * inner // gn + 1)),
            pl.BlockSpec((M_CHUNK, LANES), row),
            pl.BlockSpec((1, LANES), const),
            pl.BlockSpec((1, LANES), const),
            pl.BlockSpec((1, inner), const),
            pl.BlockSpec((1, inner), const),
        ],
        out_specs=pl.BlockSpec((M_CHUNK, inner), row),
        out_shape=jax.ShapeDtypeStruct((t, inner), BF16),
        scratch_shapes=[pltpu.VMEM((M_STATE, inner), F32),
                        pltpu.VMEM((M_CHUNK, inner), F32)],
        compiler_params=_params("parallel", "arbitrary"),
        name="ssd_chunk",
    )(zxbc, zxbc, zxbc, zxbc, dt_raw, dt_bias, a_log, d_row, norm_g)


def _cast_body(wt_ref, o_ref):
    o_ref[...] = wt_ref[...].T.astype(BF16)


def _cast_leading_cols(w, n_cols, *, tn=1024):
    layers, rows, _ = w.shape
    assert n_cols % tn == 0
    return pl.pallas_call(
        _cast_body,
        grid=(layers, n_cols // tn),
        in_specs=[pl.BlockSpec((None, tn, rows), lambda l, j: (l, j, 0))],
        out_specs=pl.BlockSpec((None, rows, tn), lambda l, j: (l, 0, j)),
        out_shape=jax.ShapeDtypeStruct((layers, rows, n_cols), BF16),
        compiler_params=_params("parallel", "parallel"),
        name="cast_weight_columns",
    )(jnp.swapaxes(w, 1, 2))


def _block_diag_tiles(blocks, n_tiles):
    g, r, c = blocks.shape
    gpt = g // n_tiles
    eye = jnp.eye(gpt, dtype=blocks.dtype)
    b = blocks.reshape(n_tiles, gpt, r, c)
    out = jnp.einsum('tgrc,gh->tgrhc', b, eye)
    return out.reshape(n_tiles, gpt * r, gpt * c)


def _rope_inv_freq_row():
    half = ROPE_DIM // 2
    inv_freq = jnp.exp(-math.log(ROPE_THETA) * jnp.arange(half, dtype=F32) * (2.0 / ROPE_DIM))
    lane = jnp.arange(LANES) % HEAD_DIM
    return jnp.where(lane < ROPE_DIM, inv_freq[lane % half], 0.0).reshape(1, LANES).astype(F32)


def _even_prep(w_in, sinks, a_re, a_im, log_dt, b_re, b_im, c_re, c_im, d_skip, w_glu, b_glu, w_out):
    n_layers, d, _ = w_in.shape
    n_q = sinks.shape[-1]
    a_width = n_q * HEAD_DIM
    groups, state = a_re.shape[1:]
    s5_width = groups * S5_GROUP
    kv_width = (w_in.shape[-1] - a_width - s5_width) // 2
    n_kv = kv_width // HEAD_DIM
    grp = n_q // n_kv
    assert grp % 2 == 0 and a_width % LANES == 0 and a_width == s5_width

    assert kv_width % LANES == 0
    w_all = w_in.astype(BF16)

    sink_rows = jnp.repeat(sinks.astype(F32).reshape(n_layers * n_q // 2, 2), ATTN_BLOCK, axis=1)
    sink_rows = sink_rows.reshape(n_layers * n_q // 2, 1, 2 * ATTN_BLOCK)

    fold = lambda v: v.astype(F32).reshape((n_layers * groups,) + v.shape[2:])
    abr, abi, bbr, bbi = _s5_discretise(fold(a_re), fold(a_im), fold(log_dt), fold(b_re), fold(b_im))
    n_tiles = groups // S5_GROUPS_PER_TILE
    all_tiles = n_layers * n_tiles
    to_in = lambda bb: _block_diag_tiles(jnp.swapaxes(bb, 1, 2), all_tiles)
    wb = jnp.concatenate([to_in(bbr), to_in(bbi)], axis=-1).astype(BF16)
    to_out = lambda cc: _block_diag_tiles(jnp.swapaxes(fold(cc), 1, 2), all_tiles)
    wc = jnp.concatenate([to_out(c_re), -to_out(c_im)], axis=1).astype(BF16)
    return dict(
        w_all=w_all, sink_rows=sink_rows, wb=wb, wc=wc, n_tiles=n_tiles,
        a_re_t=abr.reshape(all_tiles, 1, -1), a_im_t=abi.reshape(all_tiles, 1, -1),
        d_rows=d_skip.astype(F32).reshape(n_layers, 1, s5_width), b_glu_rows=b_glu.astype(F32).reshape(n_layers, 1, s5_width),
        w_glu=w_glu.astype(BF16), w_out=w_out.astype(BF16),
        a_width=a_width, kv_width=kv_width, s5_width=s5_width, pairs_per_kv=grp // 2)


def _even_mixer(x, pos_b, norm_g, p, e, *, batch, seq):
    d = x.shape[-1]
    a_width, kv_width, s5_width = p["a_width"], p["kv_width"], p["s5_width"]
    q, kd, vd, u = _even_inproj(x, norm_g, p["w_all"], e, pos_b, _rope_inv_freq_row(),
                                qw=a_width, kw=2 * kv_width, vw=2 * kv_width, uw=s5_width)
    attn = _attention(q, kd, vd, p["sink_rows"], e, batch=batch, seq=seq, pairs_per_kv=p["pairs_per_kv"])
    ssm = _s5(u, p["wb"], p["wc"], p["a_re_t"], p["a_im_t"], p["d_rows"], p["w_glu"], p["b_glu_rows"], e,
              p["n_tiles"], batch=batch, seq=seq)
    tm = 512
    terms = [
        (attn, pl.BlockSpec((tm, a_width), lambda i: (i, 0)),
         p["w_out"], pl.BlockSpec((None, a_width, d), lambda i: (e, 0, 0))),
        (ssm, pl.BlockSpec((tm, s5_width), lambda i: (i, 0)),
         p["w_out"], pl.BlockSpec((None, s5_width, d), lambda i: (e, 1, 0))),
    ]
    return _outproj(x, terms, tm=tm)


def _odd_mixer(x, norm_g, w_main_all, w_dt_all, conv_w, conv_b, dt_bias, a_log, d_skip, norm_out, w_out_all, o, *,
               batch, seq):
    d = x.shape[-1]
    inner = norm_out.shape[0]
    heads = a_log.shape[0]
    conv_dim = conv_w.shape[-1]
    n_groups = (conv_dim - inner) // (2 * M_STATE)
    assert heads <= LANES and inner // heads == M_HEAD_DIM
    pad = LANES - heads
    zxbc, dt_raw = _odd_inproj(x, norm_g, w_main_all, o, inner + conv_dim, w_dt_all, conv_w.astype(F32),
                               conv_b.astype(F32).reshape(1, conv_dim), seq=seq, n_plain_cols=inner)
    row = lambda v: jnp.pad(v.astype(F32), (0, pad)).reshape(1, LANES)
    y = _ssd(zxbc, dt_raw, row(dt_bias), row(a_log), jnp.repeat(d_skip.astype(F32), M_HEAD_DIM).reshape(1, inner),
             norm_out.astype(F32).reshape(1, inner), batch=batch, seq=seq, inner=inner, n_groups=n_groups)
    tm = 512
    return _outproj(x, [(y, pl.BlockSpec((tm, inner), lambda i: (i, 0)),
                         w_out_all, pl.BlockSpec((None, inner, d), lambda i: (o, 0, 0)))], tm=tm)


def kernel(x, positions, norm_ffn1, ffn1_gate, ffn1_up, ffn1_down, norm_mix, norm_ffn2, ffn2_gate, ffn2_up,
           ffn2_down, ev_w_in, ev_sinks, s5_a_re, s5_a_im, s5_log_dt, s5_b_re, s5_b_im, s5_c_re, s5_c_im, s5_d,
           s5_w_glu, s5_b_glu, ev_w_out, m_w_in, m_conv_w, m_conv_b, m_dt_bias, m_a_log, m_d, m_norm, m_w_out,
           final_norm):
    batch, seq, d = x.shape
    depth = norm_ffn1.shape[0]
    t = batch * seq
    h = x.reshape(t, d).astype(F32)
    pos_b = jnp.broadcast_to(positions.reshape(t, 1).astype(F32), (t, LANES))
    g1, g2, gm = (v.astype(F32).reshape(depth, 1, d) for v in (norm_ffn1, norm_ffn2, norm_mix))
    even = _even_prep(ev_w_in, ev_sinks, s5_a_re, s5_a_im, s5_log_dt, s5_b_re, s5_b_im, s5_c_re, s5_c_im, s5_d,
                      s5_w_glu, s5_b_glu, ev_w_out)
    m_w_out_all = m_w_out.astype(BF16)
    m_main = m_norm.shape[-1] + m_conv_w.shape[-1]
    m_w_main_all = _cast_leading_cols(m_w_in, m_main)
    m_w_dt_all = jnp.pad(m_w_in[:, :, m_main:], ((0, 0), (0, 0), (0, LANES - m_a_log.shape[-1]))).astype(BF16)
    ffn_f32 = [(ffn1_gate, ffn1_up, ffn1_down), (ffn2_gate, ffn2_up, ffn2_down)]
    w_cur = tuple(w[0].astype(BF16) for w in ffn_f32[0])
    for layer in range(depth):
        h, w_cur = _ffn(h, g1, *w_cur, layer, next_f32=ffn_f32[1] + (layer,))
        if layer % 2 == 0:
            e = layer // 2
            h = _even_mixer(h, pos_b, gm[layer], even, e, batch=batch, seq=seq)
        else:
            o = layer // 2
            h = _odd_mixer(h, gm[layer], m_w_main_all, m_w_dt_all, m_conv_w[o], m_conv_b[o], m_dt_bias[o], m_a_log[o],
                           m_d[o], m_norm[o], m_w_out_all, o, batch=batch, seq=seq)
        if layer == depth - 1:
            h, _ = _ffn(h, g2, *w_cur, layer, final_g=final_norm.astype(F32).reshape(1, d))
        else:
            h, w_cur = _ffn(h, g2, *w_cur, layer, next_f32=ffn_f32[0] + (layer + 1,))
    return h.reshape(batch, seq, d).astype(x.dtype)
```

```python
import functools
import math

import jax
import jax.numpy as jnp
from jax import lax
from jax.experimental import pallas as pl
from jax.experimental.pallas import tpu as pltpu

F32 = jnp.float32
BF16 = jnp.bfloat16

NORM_EPS = 1e-5
NEG_INF = -1e30
LANES = 128
SUBLANES = 8
VMEM_LIMIT_BYTES = 60 * 1024 * 1024

HEAD_DIM = 64
ATTN_BLOCK = 128
ROPE_THETA = 500000.0
ROPE_DIM = HEAD_DIM // 4
ATTN_BLOCKS_PER_STEP = 4
S5_GROUP = 16
S5_STATE = 64
S5_GROUPS_PER_TILE = 16
M_HEAD_DIM = 64
M_STATE = 128
M_CHUNK = 128
M_CONV = 4


def _params(*sem):
    return pltpu.CompilerParams(dimension_semantics=sem, vmem_limit_bytes=VMEM_LIMIT_BYTES)


def _rms(x, g):
    return x * lax.rsqrt(jnp.mean(x * x, axis=-1, keepdims=True) + NORM_EPS) * g


def _ffn_body(final, cast_next, x_ref, g_ref, wgu_ref, wd_ref, *rest):
    xn_ref = rest[-1]
    o_ref = rest[-4] if cast_next else rest[-2]
    j = pl.program_id(1)
    tf = wd_ref.shape[0]
    if cast_next:
        n_in = 1 if final else 0
        ng_ref, nu_ref, nd_ref = rest[n_in:n_in + 3]
        ngu_out, nd_out = rest[-3:-1]
        half = ng_ref.shape[-1]
        ngu_out[:, :half] = ng_ref[...].astype(BF16)
        ngu_out[:, half:] = nu_ref[...].astype(BF16)
        nd_out[...] = nd_ref[...].astype(BF16)

    @pl.when(j == 0)
    def _():
        x = x_ref[...]
        xn_ref[...] = _rms(x, g_ref[...]).astype(BF16)
        o_ref[...] = x

    gu = jnp.dot(xn_ref[...], wgu_ref[...], preferred_element_type=F32)
    h = (jax.nn.silu(gu[:, :tf]) * gu[:, tf:] * 0.5).astype(BF16)
    o_ref[...] += jnp.dot(h, wd_ref[...], preferred_element_type=F32)

    if final:
        @pl.when(j == pl.num_programs(1) - 1)
        def _():
            o_ref[...] = _rms(o_ref[...], rest[0][...])


def _interleave_gate_up(wg, wu, tf):
    d, dff = wg.shape
    return jnp.stack([wg.reshape(d, dff // tf, tf), wu.reshape(d, dff // tf, tf)], axis=2).reshape(d, 2 * dff)


def _ffn(x, g, wgu, wd, layer, final_g=None, next_f32=None, *, tm=1024, tf=512):
    t, d = x.shape
    dff = wd.shape[0]
    ni, nj = t // tm, dff // tf
    in_specs = [
        pl.BlockSpec((tm, d), lambda i, j: (i, 0)),
        pl.BlockSpec((None, 1, d), lambda i, j: (layer, 0, 0)),
        pl.BlockSpec((d, 2 * tf), lambda i, j: (0, j)),
        pl.BlockSpec((tf, d), lambda i, j: (j, 0)),
    ]
    args = [x, g, wgu, wd]
    out_specs = [pl.BlockSpec((tm, d), lambda i, j: (i, 0))]
    out_shape = [jax.ShapeDtypeStruct((t, d), F32)]
    if final_g is not None:
        in_specs.append(pl.BlockSpec((1, d), lambda i, j: (0, 0)))
        args.append(final_g)
    if next_f32 is not None:
        ng, nu, nd, nl = next_f32
        assert d % ni == 0 and dff % nj == 0 and dff // nj == tf
        up_blk, dn_blk = (d // ni, tf), (tf, d // ni)
        in_specs += [pl.BlockSpec((None,) + up_blk, lambda i, j: (nl, i, j)),
                     pl.BlockSpec((None,) + up_blk, lambda i, j: (nl, i, j)),
                     pl.BlockSpec((None,) + dn_blk, lambda i, j: (nl, j, i))]
        args += [ng, nu, nd]
        out_specs += [pl.BlockSpec((d // ni, 2 * tf), lambda i, j: (i, j)), pl.BlockSpec(dn_blk, lambda i, j: (j, i))]
        out_shape += [jax.ShapeDtypeStruct((d, 2 * dff), BF16), jax.ShapeDtypeStruct((dff, d), BF16)]
    outs = pl.pallas_call(
        functools.partial(_ffn_body, final_g is not None, next_f32 is not None),
        grid=(ni, nj),
        in_specs=in_specs,
        out_specs=out_specs,
        out_shape=out_shape,
        scratch_shapes=[pltpu.VMEM((tm, d), BF16)],
        compiler_params=_params("parallel", "arbitrary"),
        name="swiglu_ffn",
    )(*args)
    return outs[0], tuple(outs[1:])


def _outproj_body(n_terms, x_ref, *refs):
    o_ref = refs[-1]
    acc = x_ref[...]
    for i in range(n_terms):
        acc = acc + jnp.dot(refs[2 * i][...], refs[2 * i + 1][...], preferred_element_type=F32)
    o_ref[...] = acc


def _outproj(x, terms, *, tm):
    t, d = x.shape
    in_specs = [pl.BlockSpec((tm, d), lambda i: (i, 0))]
    args = [x]
    for a, a_spec, w, w_spec in terms:
        in_specs += [a_spec, w_spec]
        args += [a, w]
    return pl.pallas_call(
        functools.partial(_outproj_body, len(terms)),
        grid=(t // tm,),
        in_specs=in_specs,
        out_specs=pl.BlockSpec((tm, d), lambda i: (i, 0)),
        out_shape=jax.ShapeDtypeStruct((t, d), F32),
        compiler_params=_params("parallel"),
        name="residual_outproj",
    )(*args)


def _even_inproj_body(n_rot, x_ref, g_ref, w_ref, pos_ref, invf_ref, q_ref, k_ref, v_ref, u_ref, xn_ref):
    xn_ref[...] = _rms(x_ref[...], g_ref[...]).astype(BF16)
    ang = pos_ref[...] * invf_ref[...]
    cos, sin = jnp.cos(ang), jnp.sin(ang)
    lane = lax.broadcasted_iota(jnp.int32, ang.shape, 1) % HEAD_DIM
    half = ROPE_DIM // 2
    sin_from_lower = jnp.where((lane >= half) & (lane < ROPE_DIM), sin, 0.0)
    sin_from_upper = jnp.where(lane < half, sin, 0.0)

    first_head = lax.broadcasted_iota(jnp.int32, ang.shape, 1) < HEAD_DIM

    def store_doubled(ref, c, t):
        swapped = pltpu.roll(t, HEAD_DIM, axis=1)
        ref[:, 2 * c:2 * c + LANES] = jnp.where(first_head, t, swapped).astype(BF16)
        ref[:, 2 * c + LANES:2 * c + 2 * LANES] = jnp.where(first_head, swapped, t).astype(BF16)

    cw = 512
    qw, kw, vw = q_ref.shape[-1], k_ref.shape[-1] // 2, v_ref.shape[-1] // 2
    for c0 in range(0, w_ref.shape[-1], cw):
        p = jnp.dot(xn_ref[...], w_ref[:, c0:c0 + cw], preferred_element_type=F32)
        for l0 in range(0, cw, LANES):
            col = c0 + l0
            t = p[:, l0:l0 + LANES]
            if col < n_rot:
                t = (t * cos + pltpu.roll(t, half, axis=1) * sin_from_lower
                     - pltpu.roll(t, LANES - half, axis=1) * sin_from_upper)
            if col < qw:
                q_ref[:, col:col + LANES] = t.astype(BF16)
            elif col < qw + kw:
                store_doubled(k_ref, col - qw, t)
            elif col < qw + kw + vw:
                store_doubled(v_ref, col - qw - kw, t)
            else:
                c = col - qw - kw - vw
                u_ref[:, c:c + LANES] = t


def _even_inproj(x, g, w_all, e, pos_b, invf, *, qw, kw, vw, uw, tm=512):
    t, d = x.shape
    return pl.pallas_call(
        functools.partial(_even_inproj_body, qw + kw // 2),
        grid=(t // tm,),
        in_specs=[
            pl.BlockSpec((tm, d), lambda i: (i, 0)),
            pl.BlockSpec((1, d), lambda i: (0, 0)),
            pl.BlockSpec((None,) + w_all.shape[1:], lambda i: (e, 0, 0)),
            pl.BlockSpec((tm, LANES), lambda i: (i, 0)),
            pl.BlockSpec((1, LANES), lambda i: (0, 0)),
        ],
        out_specs=[
            pl.BlockSpec((tm, qw), lambda i: (i, 0)),
            pl.BlockSpec((tm, kw), lambda i: (i, 0)),
            pl.BlockSpec((tm, vw), lambda i: (i, 0)),
            pl.BlockSpec((tm, uw), lambda i: (i, 0)),
        ],
        out_shape=[
            jax.ShapeDtypeStruct((t, qw), BF16),
            jax.ShapeDtypeStruct((t, kw), BF16),
            jax.ShapeDtypeStruct((t, vw), BF16),
            jax.ShapeDtypeStruct((t, uw), F32),
        ],
        scratch_shapes=[pltpu.VMEM((tm, d), BF16)],
        compiler_params=_params("parallel"),
        name="even_inproj_rotary",
    )(x, g, w_all, pos_b, invf)


def _attn_body(pairs_per_kv, q_ref, kc_ref, kp_ref, vc_ref, vp_ref, sink_ref, o_ref):
    m_step = pl.program_id(1)
    blk = ATTN_BLOCK
    kk = jnp.concatenate([kp_ref[...], kc_ref[...]], axis=0)
    vv = jnp.concatenate([vp_ref[...], vc_ref[...]], axis=0)
    lower = lax.broadcasted_iota(jnp.int32, (blk, LANES), 1) < HEAD_DIM
    key = lax.broadcasted_iota(jnp.int32, (2 * blk, 2 * blk), 0)
    qrow = lax.broadcasted_iota(jnp.int32, (2 * blk, 2 * blk), 1) % blk
    band = (key > qrow) & (key <= qrow + blk)
    upper_rows = lax.broadcasted_iota(jnp.int32, (LANES, blk), 0) >= HEAD_DIM
    n_pairs = q_ref.shape[-1] // LANES
    scale = 1.0 / math.sqrt(HEAD_DIM)

    def scores(t, c):
        hk = c // pairs_per_kv
        qc = q_ref[t * blk:(t + 1) * blk, c * LANES:(c + 1) * LANES] * scale
        q2 = jnp.concatenate([jnp.where(lower, qc, jnp.zeros_like(qc)),
                              jnp.where(lower, jnp.zeros_like(qc), qc)], axis=0)
        kh = kk[t * blk:t * blk + 2 * blk, hk * LANES:(hk + 1) * LANES]
        return lax.dot_general(kh, q2, (((1,), (1,)), ((), ())), preferred_element_type=F32)

    def finish(t, c, s):
        hk = c // pairs_per_kv
        first = (m_step == 0) if t == 0 else False
        valid = band & ((key >= blk) | jnp.logical_not(first))
        s = jnp.where(valid, s, NEG_INF)
        sink = sink_ref[c]
        mx = jnp.maximum(jnp.max(s, axis=0, keepdims=True), sink)
        p = jnp.exp(s - mx)
        den = jnp.sum(p, axis=0, keepdims=True) + jnp.exp(sink - mx)
        p = (p * (1.0 / den)).astype(BF16)
        vh = vv[t * blk:t * blk + 2 * blk, hk * LANES:(hk + 1) * LANES]
        r = lax.dot_general(vh, p, (((0,), (0,)), ((), ())), preferred_element_type=F32)
        o_t = jnp.where(upper_rows, r[:, blk:], r[:, :blk])
        o_ref[t * blk:(t + 1) * blk, c * LANES:(c + 1) * LANES] = o_t.T.astype(BF16)

    units = [(t, c) for t in range(ATTN_BLOCKS_PER_STEP) for c in range(n_pairs)]
    ahead = 4
    pending = [scores(*u) for u in units[:ahead]]
    for n, (t, c) in enumerate(units):
        if n + ahead < len(units):
            pending.append(scores(*units[n + ahead]))
        finish(t, c, pending.pop(0))


def _attention(q, kd, vd, sink_rows_all, e, *, batch, seq, pairs_per_kv):
    t, qw = q.shape
    kw = kd.shape[-1]
    rows = ATTN_BLOCK * ATTN_BLOCKS_PER_STEP
    steps = seq // rows
    bps = ATTN_BLOCKS_PER_STEP
    nblk = seq // ATTN_BLOCK

    def cur(b, m):
        return (b * steps + m, 0)

    def prev(b, m):
        return (b * nblk + jnp.maximum(m * bps - 1, 0), 0)

    return pl.pallas_call(
        functools.partial(_attn_body, pairs_per_kv),
        grid=(batch, steps),
        in_specs=[
            pl.BlockSpec((rows, qw), cur),
            pl.BlockSpec((rows, kw), cur),
            pl.BlockSpec((ATTN_BLOCK, kw), prev),
            pl.BlockSpec((rows, kw), cur),
            pl.BlockSpec((ATTN_BLOCK, kw), prev),
            pl.BlockSpec((qw // LANES,) + sink_rows_all.shape[1:], lambda b, m: (e, 0, 0)),
        ],
        out_specs=pl.BlockSpec((rows, qw), cur),
        out_shape=jax.ShapeDtypeStruct((t, qw), BF16),
        compiler_params=_params("parallel", "arbitrary"),
        name="swa_attention",
    )(q, kd, kd, vd, vd, sink_rows_all)


def _s5_discretise_body(are_ref, aim_ref, logdt_ref, bre_ref, bim_ref, abr_ref, abi_ref, bbr_ref, bbi_ref):
    are, aim = are_ref[...], aim_ref[...]
    dt = jnp.exp(logdt_ref[...])
    mag = jnp.exp(are * dt)
    abar_re, abar_im = mag * jnp.cos(aim * dt), mag * jnp.sin(aim * dt)
    nr, ni = abar_re - 1.0, abar_im
    den = are * are + aim * aim
    coef_re = (nr * are + ni * aim) / den
    coef_im = (ni * are - nr * aim) / den
    bre, bim = bre_ref[...], bim_ref[...]
    abr_ref[...] = abar_re
    abi_ref[...] = abar_im
    bbr_ref[...] = coef_re * bre - coef_im * bim
    bbi_ref[...] = coef_re * bim + coef_im * bre


def _s5_discretise(a_re, a_im, log_dt, b_re, b_im):
    g, p = a_re.shape
    c = b_re.shape[-1]
    col = lambda v: v.reshape(g * p, 1)
    logdt_col = jnp.broadcast_to(log_dt[:, None], (g, p)).reshape(g * p, 1)
    outs = pl.pallas_call(
        _s5_discretise_body,
        out_shape=[jax.ShapeDtypeStruct((g * p, 1), F32)] * 2 + [jax.ShapeDtypeStruct((g * p, c), F32)] * 2,
        name="s5_discretise",
    )(col(a_re), col(a_im), logdt_col, b_re.reshape(g * p, c), b_im.reshape(g * p, c))
    abr, abi, bbr, bbi = outs
    return abr.reshape(g, p), abi.reshape(g, p), bbr.reshape(g, p, c), bbi.reshape(g, p, c)


def _cmul(ar, ai, br, bi):
    return ar * br - ai * bi, ar * bi + ai * br


def _s5_body(n_tiles, steps, u_ref, wb_ref, wc_ref, are_ref, aim_ref, d_ref, wglu_ref, bglu_ref,
             o_ref, us_ref, ys_ref, carry_ref, pw_ref, p8_ref):
    blk = pl.program_id(1)
    sw = are_ref.shape[-1]
    cw = wc_ref.shape[-1]
    nslab = us_ref.shape[0]
    nsub = SUBLANES

    @pl.when(blk == 0)
    def _():
        carry_ref[...] = jnp.zeros_like(carry_ref)
        for kt in range(n_tiles):
            ar, ai = are_ref[kt], aim_ref[kt]
            pr, pi = ar, ai
            for j in range(steps):
                pw_ref[kt, j:j + 1, :sw] = pr
                pw_ref[kt, j:j + 1, sw:] = pi
                if j + 1 < steps:
                    pr, pi = _cmul(pr, pi, ar, ai)
            qr, qi = pr, pi
            for s in range(nsub):
                p8_ref[kt, s:s + 1, :sw] = qr
                p8_ref[kt, s:s + 1, sw:] = qi
                if s + 1 < nsub:
                    qr, qi = _cmul(qr, qi, pr, pi)

    for s in range(nsub):
        for c in range(nslab):
            us_ref[c, pl.ds(s, steps, stride=nsub), :] = u_ref[s * steps:(s + 1) * steps, c * LANES:(c + 1) * LANES]

    def u_cols(c0, n):
        return jnp.concatenate([us_ref[c0 + i] for i in range(n)], axis=1)

    def input_drive(kt):
        ub = u_cols(kt * (cw // LANES), cw // LANES).astype(BF16)
        return jnp.dot(ub, wb_ref[kt], preferred_element_type=F32)

    srow = lax.broadcasted_iota(jnp.int32, (nsub, sw), 0)

    def scan_tile(kt, bu):
        ar = jnp.broadcast_to(are_ref[kt], (nsub, sw))
        ai = jnp.broadcast_to(aim_ref[kt], (nsub, sw))
        hr, hi = jnp.zeros((nsub, sw), F32), jnp.zeros((nsub, sw), F32)
        zero_start = []
        for j in range(steps):
            r0 = j * nsub
            pr, pi = _cmul(ar, ai, hr, hi)
            hr, hi = pr + bu[r0:r0 + nsub, :sw], pi + bu[r0:r0 + nsub, sw:]
            zero_start.append((hr, hi))
        xr, xi = hr, hi
        for d in (1, 2, 4):
            pr, pi = p8_ref[kt, d - 1:d, :sw], p8_ref[kt, d - 1:d, sw:]
            sr = jnp.where(srow >= d, pltpu.roll(xr, d, axis=0), 0.0)
            si = jnp.where(srow >= d, pltpu.roll(xi, d, axis=0), 0.0)
            qr, qi = _cmul(pr, pi, sr, si)
            xr, xi = xr + qr, xi + qi
        cr, ci = carry_ref[kt, :, :sw], carry_ref[kt, :, sw:]
        qr, qi = _cmul(p8_ref[kt, :, :sw], p8_ref[kt, :, sw:], cr, ci)
        xr, xi = xr + qr, xi + qi
        init_r = jnp.where(srow == 0, cr, pltpu.roll(xr, 1, axis=0))
        init_i = jnp.where(srow == 0, ci, pltpu.roll(xi, 1, axis=0))
        carry_ref[kt, :, :sw] = jnp.broadcast_to(xr[nsub - 1:nsub], (nsub, sw))
        carry_ref[kt, :, sw:] = jnp.broadcast_to(xi[nsub - 1:nsub], (nsub, sw))
        rows = []
        for j in range(steps):
            pr, pi = _cmul(pw_ref[kt, j:j + 1, :sw], pw_ref[kt, j:j + 1, sw:], init_r, init_i)
            rows.append(jnp.concatenate([zero_start[j][0] + pr, zero_start[j][1] + pi], axis=1))
        return jnp.concatenate(rows, axis=0)

    bus = [input_drive(kt) for kt in range(n_tiles)]
    ys = [jnp.dot(scan_tile(kt, bus[kt]).astype(BF16), wc_ref[kt], preferred_element_type=F32)
          for kt in range(n_tiles)]
    y = jnp.concatenate(ys, axis=1) + d_ref[...] * u_cols(0, nslab)
    gl = jax.nn.gelu(y)
    gate = jnp.dot(gl.astype(BF16), wglu_ref[...], preferred_element_type=F32) + bglu_ref[...]
    out = gl * jax.nn.sigmoid(gate)
    for c in range(nslab):
        ys_ref[c] = out[:, c * LANES:(c + 1) * LANES]
    for s in range(nsub):
        for c in range(nslab):
            o_ref[s * steps:(s + 1) * steps, c * LANES:(c + 1) * LANES] = (
                ys_ref[c, pl.ds(s, steps, stride=nsub), :].astype(BF16))


def _s5(u, wb, wc, a_re_t, a_im_t, d_rows, w_glu_all, b_glu_rows, e, n_tiles, *, batch, seq, steps=32):
    t, width = u.shape
    _, cw, sw2 = wb.shape
    rows_per_block = steps * SUBLANES
    nblk = seq // rows_per_block
    layer3 = lambda b, k: (e, 0, 0)
    row = lambda b, k: (b * nblk + k, 0)
    return pl.pallas_call(
        functools.partial(_s5_body, n_tiles, steps),
        grid=(batch, nblk),
        in_specs=[
            pl.BlockSpec((rows_per_block, width), row),
            pl.BlockSpec((n_tiles,) + wb.shape[1:], layer3),
            pl.BlockSpec((n_tiles,) + wc.shape[1:], layer3),
            pl.BlockSpec((n_tiles,) + a_re_t.shape[1:], layer3),
            pl.BlockSpec((n_tiles,) + a_im_t.shape[1:], layer3),
            pl.BlockSpec((None,) + d_rows.shape[1:], layer3),
            pl.BlockSpec((None,) + w_glu_all.shape[1:], layer3),
            pl.BlockSpec((None,) + b_glu_rows.shape[1:], layer3),
        ],
        out_specs=pl.BlockSpec((rows_per_block, width), row),
        out_shape=jax.ShapeDtypeStruct((t, width), BF16),
        scratch_shapes=[
            pltpu.VMEM((width // LANES, rows_per_block, LANES), F32),
            pltpu.VMEM((width // LANES, rows_per_block, LANES), F32),
            pltpu.VMEM((n_tiles, SUBLANES, sw2), F32),
            pltpu.VMEM((n_tiles, steps, sw2), F32),
            pltpu.VMEM((n_tiles, SUBLANES, sw2), F32),
        ],
        compiler_params=_params("parallel", "arbitrary"),
        name="s5_scan_glu",
    )(u, wb, wc, a_re_t, a_im_t, d_rows, w_glu_all, b_glu_rows)


HALO_ROWS = 16


def _odd_inproj_body(n_plain, tiles_per_seq, x_ref, xh_ref, g_ref, w_ref, wdt_ref, cw_ref, cb_ref,
                     o_ref, dt_ref, xn_ref):
    i, j = pl.program_id(0), pl.program_id(1)

    @pl.when(j == 0)
    def _():
        g = g_ref[...]
        xn_ref[0:HALO_ROWS, :] = _rms(xh_ref[...], g).astype(BF16)
        xn_ref[HALO_ROWS:, :] = _rms(x_ref[...], g).astype(BF16)
        dt_ref[...] = jnp.dot(xn_ref[HALO_ROWS:, :], wdt_ref[...], preferred_element_type=F32)

    @pl.when(j < n_plain)
    def _():
        o_ref[...] = jnp.dot(xn_ref[HALO_ROWS:, :], w_ref[...], preferred_element_type=F32)

    @pl.when(j >= n_plain)
    def _():
        p = jnp.dot(xn_ref[...], w_ref[...], preferred_element_type=F32)
        seq_start = (i % tiles_per_seq) == 0
        halo = jnp.where(seq_start, 0.0, p[:HALO_ROWS])
        ext = jnp.concatenate([halo, p[HALO_ROWS:]], axis=0)
        acc = cb_ref[...] + cw_ref[M_CONV - 1:M_CONV, :] * p[HALO_ROWS:]
        for k in range(1, M_CONV):
            acc = acc + cw_ref[M_CONV - 1 - k:M_CONV - k, :] * ext[HALO_ROWS - k:ext.shape[0] - k]
        o_ref[...] = jax.nn.silu(acc)


def _odd_inproj(x, g, w, layer, n, wdt, conv_w, conv_b, *, seq, n_plain_cols, tm=1024, tn=1024):
    t, d = x.shape
    assert seq % tm == 0 and n_plain_cols % tn == 0 and n % tn == 0
    n_plain = n_plain_cols // tn
    halo_blocks = tm // HALO_ROWS
    conv_col = lambda i, j: (0, jnp.maximum(j - n_plain, 0))
    return pl.pallas_call(
        functools.partial(_odd_inproj_body, n_plain, seq // tm),
        grid=(t // tm, n // tn),
        in_specs=[
            pl.BlockSpec((tm, d), lambda i, j: (i, 0)),
            pl.BlockSpec((HALO_ROWS, d), lambda i, j: (jnp.maximum(i * halo_blocks - 1, 0), 0)),
            pl.BlockSpec((1, d), lambda i, j: (0, 0)),
            pl.BlockSpec((None, d, tn), lambda i, j: (layer, 0, j)),
            pl.BlockSpec((None,) + wdt.shape[1:], lambda i, j: (layer, 0, 0)),
            pl.BlockSpec((M_CONV, tn), conv_col),
            pl.BlockSpec((1, tn), conv_col),
        ],
        out_specs=[
            pl.BlockSpec((tm, tn), lambda i, j: (i, j)),
            pl.BlockSpec((tm, wdt.shape[-1]), lambda i, j: (i, 0)),
        ],
        out_shape=[jax.ShapeDtypeStruct((t, n), F32), jax.ShapeDtypeStruct((t, wdt.shape[-1]), F32)],
        scratch_shapes=[pltpu.VMEM((HALO_ROWS + tm, d), BF16)],
        compiler_params=_params("parallel", "arbitrary"),
        name="odd_inproj_conv",
    )(x, x, g, w, wdt, conv_w, conv_b)


def _ssd_body(n_groups, z_ref, xs_ref, bm_ref, cm_ref, dt_ref, dtb_ref, alog_ref, dsk_ref, ng_ref,
              o_ref, state_ref, y_ref):
    c = pl.program_id(1)
    L = M_CHUNK
    inner = xs_ref.shape[-1]
    gw = bm_ref.shape[-1] // n_groups
    hw = inner // n_groups

    @pl.when(c == 0)
    def _():
        state_ref[...] = jnp.zeros_like(state_ref)

    dt = jax.nn.softplus(dt_ref[...] + dtb_ref[...])
    a = -jnp.exp(alog_ref[...])
    acum = a * dt
    rowi = lax.broadcasted_iota(jnp.int32, acum.shape, 0)
    k = 1
    while k < L:
        acum = acum + jnp.where(rowi >= k, pltpu.roll(acum, k, axis=0), 0.0)
        k *= 2
    src = acum - jnp.log(dt)
    src_t = src.T
    a_last = acum[L - 1:L, :]

    li = lax.broadcasted_iota(jnp.int32, (L, L), 0)
    si = lax.broadcasted_iota(jnp.int32, (L, L), 1)
    causal = li >= si
    lane = lax.broadcasted_iota(jnp.int32, (L, LANES), 1)
    lower = lane < M_HEAD_DIM
    heads_per_group = hw // M_HEAD_DIM

    for g in range(n_groups):
        cgb = cm_ref[:, g * gw:(g + 1) * gw].astype(BF16)
        bgb = bm_ref[:, g * gw:(g + 1) * gw].astype(BF16)
        cb = lax.dot_general(cgb, bgb, (((1,), (1,)), ((), ())), preferred_element_type=F32)
        prev = state_ref[:, g * hw:(g + 1) * hw]
        y_off = jnp.dot(cgb, prev.astype(BF16), preferred_element_type=F32)
        xd_parts = []
        for pr in range(heads_per_group // 2):
            h0 = g * heads_per_group + 2 * pr
            col0 = g * hw + pr * LANES
            xp = xs_ref[:, col0:col0 + LANES]
            ms = []
            for h in (h0, h0 + 1):
                seg = acum[:, h:h + 1] - src_t[h:h + 1, :]
                ms.append((cb * jnp.exp(jnp.where(causal, seg, -jnp.inf))).astype(BF16))
            lhs = jnp.concatenate(ms, axis=1)
            xpb = xp.astype(BF16)
            rhs = jnp.concatenate([jnp.where(lower, xpb, jnp.zeros_like(xpb)),
                                   jnp.where(lower, jnp.zeros_like(xpb), xpb)], axis=0)
            y_diag = jnp.dot(lhs, rhs, preferred_element_type=F32)
            al = jnp.where(lower, acum[:, h0:h0 + 1], acum[:, h0 + 1:h0 + 2])
            srcl = jnp.where(lower, src[:, h0:h0 + 1], src[:, h0 + 1:h0 + 2])
            alast = jnp.where(lower[0:1], a_last[:, h0:h0 + 1], a_last[:, h0 + 1:h0 + 2])
            y_ref[:, col0:col0 + LANES] = (y_diag + jnp.exp(al) * y_off[:, pr * LANES:(pr + 1) * LANES]
                                           + dsk_ref[:, col0:col0 + LANES] * xp)
            xd_parts.append((xp * jnp.exp(alast - srcl)).astype(BF16))
            state_ref[:, col0:col0 + LANES] = state_ref[:, col0:col0 + LANES] * jnp.exp(alast)
        xd = jnp.concatenate(xd_parts, axis=1)
        state_ref[:, g * hw:(g + 1) * hw] += lax.dot_general(
            bgb, xd, (((0,), (0,)), ((), ())), preferred_element_type=F32)

    for g in range(n_groups):
        sl = slice(g * hw, (g + 1) * hw)
        y = y_ref[:, sl] * jax.nn.silu(z_ref[:, sl])
        y = y * lax.rsqrt(jnp.mean(y * y, axis=-1, keepdims=True) + NORM_EPS)
        o_ref[:, sl] = (y * ng_ref[:, sl]).astype(BF16)


def _ssd(zxbc, dt_raw, dt_bias, a_log, d_row, norm_g, *, batch, seq, inner, n_groups):
    t = zxbc.shape[0]
    nchunks = seq // M_CHUNK
    gn = n_groups * M_STATE
    row = lambda b, c: (b * nchunks + c, 0)
    const = lambda b, c: (0, 0)
    return pl.pallas_call(
        functools.partial(_ssd_body, n_groups),
        grid=(batch, nchunks),
        in_specs=[
            pl.BlockSpec((M_CHUNK, inner), row),
            pl.BlockSpec((M_CHUNK, inner), lambda b, c: (b * nchunks + c, 1)),
            pl.BlockSpec((M_CHUNK, gn), lambda b, c: (b * nchunks + c, 2 * inner // gn)),
            pl.BlockSpec((M_CHUNK, gn), lambda b, c: (b * nchunks + c, 2 * inner // gn + 1)),
            pl.BlockSpec((M_CHUNK, LANES), row),
            pl.BlockSpec((1, LANES), const),
            pl.BlockSpec((1, LANES), const),
            pl.BlockSpec((1, inner), const),
            pl.BlockSpec((1, inner), const),
        ],
        out_specs=pl.BlockSpec((M_CHUNK, inner), row),
        out_shape=jax.ShapeDtypeStruct((t, inner), BF16),
        scratch_shapes=[pltpu.VMEM((M_STATE, inner), F32),
                        pltpu.VMEM((M_CHUNK, inner), F32)],
        compiler_params=_params("parallel", "arbitrary"),
        name="ssd_chunk",
    )(zxbc, zxbc, zxbc, zxbc, dt_raw, dt_bias, a_log, d_row, norm_g)


def _cast_body(wt_ref, o_ref):
    o_ref[...] = wt_ref[...].T.astype(BF16)


def _cast_leading_cols(w, n_cols, *, tn=1024):
    layers, rows, _ = w.shape
    assert n_cols % tn == 0
    return pl.pallas_call(
        _cast_body,
        grid=(layers, n_cols // tn),
        in_specs=[pl.BlockSpec((None, tn, rows), lambda l, j: (l, j, 0))],
        out_specs=pl.BlockSpec((None, rows, tn), lambda l, j: (l, 0, j)),
        out_shape=jax.ShapeDtypeStruct((layers, rows, n_cols), BF16),
        compiler_params=_params("parallel", "parallel"),
        name="cast_weight_columns",
    )(jnp.swapaxes(w, 1, 2))


def _block_diag_tiles(blocks, n_tiles):
    g, r, c = blocks.shape
    gpt = g // n_tiles
    eye = jnp.eye(gpt, dtype=blocks.dtype)
    b = blocks.reshape(n_tiles, gpt, r, c)
    out = jnp.einsum('tgrc,gh->tgrhc', b, eye)
    return out.reshape(n_tiles, gpt * r, gpt * c)


def _rope_inv_freq_row():
    half = ROPE_DIM // 2
    inv_freq = jnp.exp(-math.log(ROPE_THETA) * jnp.arange(half, dtype=F32) * (2.0 / ROPE_DIM))
    lane = jnp.arange(LANES) % HEAD_DIM
    return jnp.where(lane < ROPE_DIM, inv_freq[lane % half], 0.0).reshape(1, LANES).astype(F32)


def _even_prep(w_in, sinks, a_re, a_im, log_dt, b_re, b_im, c_re, c_im, d_skip, w_glu, b_glu, w_out):
    n_layers, d, _ = w_in.shape
    n_q = sinks.shape[-1]
    a_width = n_q * HEAD_DIM
    groups, state = a_re.shape[1:]
    s5_width = groups * S5_GROUP
    kv_width = (w_in.shape[-1] - a_width - s5_width) // 2
    n_kv = kv_width // HEAD_DIM
    grp = n_q // n_kv
    assert grp % 2 == 0 and a_width % LANES == 0 and a_width == s5_width

    assert kv_width % LANES == 0
    w_all = w_in.astype(BF16)

    sink_rows = jnp.repeat(sinks.astype(F32).reshape(n_layers * n_q // 2, 2), ATTN_BLOCK, axis=1)
    sink_rows = sink_rows.reshape(n_layers * n_q // 2, 1, 2 * ATTN_BLOCK)

    fold = lambda v: v.astype(F32).reshape((n_layers * groups,) + v.shape[2:])
    abr, abi, bbr, bbi = _s5_discretise(fold(a_re), fold(a_im), fold(log_dt), fold(b_re), fold(b_im))
    n_tiles = groups // S5_GROUPS_PER_TILE
    all_tiles = n_layers * n_tiles
    to_in = lambda bb: _block_diag_tiles(jnp.swapaxes(bb, 1, 2), all_tiles)
    wb = jnp.concatenate([to_in(bbr), to_in(bbi)], axis=-1).astype(BF16)
    to_out = lambda cc: _block_diag_tiles(jnp.swapaxes(fold(cc), 1, 2), all_tiles)
    wc = jnp.concatenate([to_out(c_re), -to_out(c_im)], axis=1).astype(BF16)
    return dict(
        w_all=w_all, sink_rows=sink_rows, wb=wb, wc=wc, n_tiles=n_tiles,
        a_re_t=abr.reshape(all_tiles, 1, -1), a_im_t=abi.reshape(all_tiles, 1, -1),
        d_rows=d_skip.astype(F32).reshape(n_layers, 1, s5_width), b_glu_rows=b_glu.astype(F32).reshape(n_layers, 1, s5_width),
        w_glu=w_glu.astype(BF16), w_out=w_out.astype(BF16),
        a_width=a_width, kv_width=kv_width, s5_width=s5_width, pairs_per_kv=grp // 2)


def _even_mixer(x, pos_b, norm_g, p, e, *, batch, seq):
    d = x.shape[-1]
    a_width, kv_width, s5_width = p["a_width"], p["kv_width"], p["s5_width"]
    q, kd, vd, u = _even_inproj(x, norm_g, p["w_all"], e, pos_b, _rope_inv_freq_row(),
                                qw=a_width, kw=2 * kv_width, vw=2 * kv_width, uw=s5_width)
    attn = _attention(q, kd, vd, p["sink_rows"], e, batch=batch, seq=seq, pairs_per_kv=p["pairs_per_kv"])
    ssm = _s5(u, p["wb"], p["wc"], p["a_re_t"], p["a_im_t"], p["d_rows"], p["w_glu"], p["b_glu_rows"], e,
              p["n_tiles"], batch=batch, seq=seq)
    tm = 512
    terms = [
        (attn, pl.BlockSpec((tm, a_width), lambda i: (i, 0)),
         p["w_out"], pl.BlockSpec((None, a_width, d), lambda i: (e, 0, 0))),
        (ssm, pl.BlockSpec((tm, s5_width), lambda i: (i, 0)),
         p["w_out"], pl.BlockSpec((None, s5_width, d), lambda i: (e, 1, 0))),
    ]
    return _outproj(x, terms, tm=tm)


def _odd_mixer(x, norm_g, w_main_all, w_dt_all, conv_w, conv_b, dt_bias, a_log, d_skip, norm_out, w_out_all, o, *,
               batch, seq):
    d = x.shape[-1]
    inner = norm_out.shape[0]
    heads = a_log.shape[0]
    conv_dim = conv_w.shape[-1]
    n_groups = (conv_dim - inner) // (2 * M_STATE)
    assert heads <= LANES and inner // heads == M_HEAD_DIM
    pad = LANES - heads
    zxbc, dt_raw = _odd_inproj(x, norm_g, w_main_all, o, inner + conv_dim, w_dt_all, conv_w.astype(F32),
                               conv_b.astype(F32).reshape(1, conv_dim), seq=seq, n_plain_cols=inner)
    row = lambda v: jnp.pad(v.astype(F32), (0, pad)).reshape(1, LANES)
    y = _ssd(zxbc, dt_raw, row(dt_bias), row(a_log), jnp.repeat(d_skip.astype(F32), M_HEAD_DIM).reshape(1, inner),
             norm_out.astype(F32).reshape(1, inner), batch=batch, seq=seq, inner=inner, n_groups=n_groups)
    tm = 512
    return _outproj(x, [(y, pl.BlockSpec((tm, inner), lambda i: (i, 0)),
                         w_out_all, pl.BlockSpec((None, inner, d), lambda i: (o, 0, 0)))], tm=tm)


def kernel(x, positions, norm_ffn1, ffn1_gate, ffn1_up, ffn1_down, norm_mix, norm_ffn2, ffn2_gate, ffn2_up,
           ffn2_down, ev_w_in, ev_sinks, s5_a_re, s5_a_im, s5_log_dt, s5_b_re, s5_b_im, s5_c_re, s5_c_im, s5_d,
           s5_w_glu, s5_b_glu, ev_w_out, m_w_in, m_conv_w, m_conv_b, m_dt_bias, m_a_log, m_d, m_norm, m_w_out,
           final_norm):
    batch, seq, d = x.shape
    depth = norm_ffn1.shape[0]
    t = batch * seq
    h = x.reshape(t, d).astype(F32)
    pos_b = jnp.broadcast_to(positions.reshape(t, 1).astype(F32), (t, LANES))
    g1, g2, gm = (v.astype(F32).reshape(depth, 1, d) for v in (norm_ffn1, norm_ffn2, norm_mix))
    even = _even_prep(ev_w_in, ev_sinks, s5_a_re, s5_a_im, s5_log_dt, s5_b_re, s5_b_im, s5_c_re, s5_c_im, s5_d,
                      s5_w_glu, s5_b_glu, ev_w_out)
    m_w_out_all = m_w_out.astype(BF16)
    m_main = m_norm.shape[-1] + m_conv_w.shape[-1]
    m_w_main_all = _cast_leading_cols(m_w_in, m_main)
    m_w_dt_all = jnp.pad(m_w_in[:, :, m_main:], ((0, 0), (0, 0), (0, LANES - m_a_log.shape[-1]))).astype(BF16)
    ffn_f32 = [(ffn1_gate, ffn1_up, ffn1_down), (ffn2_gate, ffn2_up, ffn2_down)]
    first = tuple(w[0].astype(BF16) for w in ffn_f32[0])
    w_cur = (_interleave_gate_up(first[0], first[1], 512), first[2])
    for layer in range(depth):
        h, w_cur = _ffn(h, g1, *w_cur, layer, next_f32=ffn_f32[1] + (layer,))
        if layer % 2 == 0:
            e = layer // 2
            h = _even_mixer(h, pos_b, gm[layer], even, e, batch=batch, seq=seq)
        else:
            o = layer // 2
            h = _odd_mixer(h, gm[layer], m_w_main_all, m_w_dt_all, m_conv_w[o], m_conv_b[o], m_dt_bias[o], m_a_log[o],
                           m_d[o], m_norm[o], m_w_out_all, o, batch=batch, seq=seq)
        if layer == depth - 1:
            h, _ = _ffn(h, g2, *w_cur, layer, final_g=final_norm.astype(F32).reshape(1, d))
        else:
            h, w_cur = _ffn(h, g2, *w_cur, layer, next_f32=ffn_f32[0] + (layer + 1,))
    return h.reshape(batch, seq, d).astype(x.dtype)
```

```python
import functools
import math

import jax
import jax.numpy as jnp
from jax import lax
from jax.experimental import pallas as pl
from jax.experimental.pallas import tpu as pltpu

F32 = jnp.float32
BF16 = jnp.bfloat16

NORM_EPS = 1e-5
NEG_INF = -1e30
LANES = 128
SUBLANES = 8
VMEM_LIMIT_BYTES = 60 * 1024 * 1024

HEAD_DIM = 64
ATTN_BLOCK = 128
ROPE_THETA = 500000.0
ROPE_DIM = HEAD_DIM // 4
ATTN_BLOCKS_PER_STEP = 4
S5_GROUP = 16
S5_STATE = 64
S5_GROUPS_PER_TILE = 16
M_HEAD_DIM = 64
M_STATE = 128
M_CHUNK = 128
M_CONV = 4


def _params(*sem):
    return pltpu.CompilerParams(dimension_semantics=sem, vmem_limit_bytes=VMEM_LIMIT_BYTES)


def _rms(x, g):
    return x * lax.rsqrt(jnp.mean(x * x, axis=-1, keepdims=True) + NORM_EPS) * g


def _ffn_body(final, cast_next, x_ref, g_ref, wg_ref, wu_ref, wd_ref, *rest):
    xn_ref = rest[-1]
    o_ref = rest[-5] if cast_next else rest[-2]
    j = pl.program_id(1)
    if cast_next:
        n_in = 1 if final else 0
        for src, dst in zip(rest[n_in:n_in + 3], rest[-4:-1]):
            dst[...] = src[...].astype(BF16)

    @pl.when(j == 0)
    def _():
        x = x_ref[...]
        xn_ref[...] = _rms(x, g_ref[...]).astype(BF16)
        o_ref[...] = x

    xn = xn_ref[...]
    gate = jnp.dot(xn, wg_ref[...], preferred_element_type=F32)
    up = jnp.dot(xn, wu_ref[...], preferred_element_type=F32)
    h = (jax.nn.silu(gate) * up * 0.5).astype(BF16)
    o_ref[...] += jnp.dot(h, wd_ref[...], preferred_element_type=F32)

    if final:
        @pl.when(j == pl.num_programs(1) - 1)
        def _():
            o_ref[...] = _rms(o_ref[...], rest[0][...])


def _ffn(x, g, wg, wu, wd, layer, final_g=None, next_f32=None, *, tm=1024, tf=512):
    t, d = x.shape
    dff = wg.shape[-1]
    ni, nj = t // tm, dff // tf
    in_specs = [
        pl.BlockSpec((tm, d), lambda i, j: (i, 0)),
        pl.BlockSpec((None, 1, d), lambda i, j: (layer, 0, 0)),
        pl.BlockSpec((d, tf), lambda i, j: (0, j)),
        pl.BlockSpec((d, tf), lambda i, j: (0, j)),
        pl.BlockSpec((tf, d), lambda i, j: (j, 0)),
    ]
    args = [x, g, wg, wu, wd]
    out_specs = [pl.BlockSpec((tm, d), lambda i, j: (i, 0))]
    out_shape = [jax.ShapeDtypeStruct((t, d), F32)]
    if final_g is not None:
        in_specs.append(pl.BlockSpec((1, d), lambda i, j: (0, 0)))
        args.append(final_g)
    if next_f32 is not None:
        ng, nu, nd, nl = next_f32
        assert d % ni == 0 and dff % nj == 0
        up_blk, dn_blk = (d // ni, dff // nj), (dff // nj, d // ni)
        in_specs += [pl.BlockSpec((None,) + up_blk, lambda i, j: (nl, i, j)),
                     pl.BlockSpec((None,) + up_blk, lambda i, j: (nl, i, j)),
                     pl.BlockSpec((None,) + dn_blk, lambda i, j: (nl, j, i))]
        args += [ng, nu, nd]
        out_specs += [pl.BlockSpec(up_blk, lambda i, j: (i, j)), pl.BlockSpec(up_blk, lambda i, j: (i, j)),
                      pl.BlockSpec(dn_blk, lambda i, j: (j, i))]
        out_shape += [jax.ShapeDtypeStruct((d, dff), BF16)] * 2 + [jax.ShapeDtypeStruct((dff, d), BF16)]
    outs = pl.pallas_call(
        functools.partial(_ffn_body, final_g is not None, next_f32 is not None),
        grid=(ni, nj),
        in_specs=in_specs,
        out_specs=out_specs,
        out_shape=out_shape,
        scratch_shapes=[pltpu.VMEM((tm, d), BF16)],
        compiler_params=_params("parallel", "arbitrary"),
        name="swiglu_ffn",
    )(*args)
    return outs[0], tuple(outs[1:])


def _outproj_body(n_terms, x_ref, *refs):
    o_ref = refs[-1]
    acc = x_ref[...]
    for i in range(n_terms):
        acc = acc + jnp.dot(refs[2 * i][...], refs[2 * i + 1][...], preferred_element_type=F32)
    o_ref[...] = acc


def _outproj(x, terms, *, tm):
    t, d = x.shape
    in_specs = [pl.BlockSpec((tm, d), lambda i: (i, 0))]
    args = [x]
    for a, a_spec, w, w_spec in terms:
        in_specs += [a_spec, w_spec]
        args += [a, w]
    return pl.pallas_call(
        functools.partial(_outproj_body, len(terms)),
        grid=(t // tm,),
        in_specs=in_specs,
        out_specs=pl.BlockSpec((tm, d), lambda i: (i, 0)),
        out_shape=jax.ShapeDtypeStruct((t, d), F32),
        compiler_params=_params("parallel"),
        name="residual_outproj",
    )(*args)


def _even_inproj_body(n_rot, x_ref, g_ref, w_ref, pos_ref, invf_ref, q_ref, k_ref, v_ref, u_ref, xn_ref):
    xn_ref[...] = _rms(x_ref[...], g_ref[...]).astype(BF16)
    ang = pos_ref[...] * invf_ref[...]
    cos, sin = jnp.cos(ang), jnp.sin(ang)
    lane = lax.broadcasted_iota(jnp.int32, ang.shape, 1) % HEAD_DIM
    half = ROPE_DIM // 2
    sin_from_lower = jnp.where((lane >= half) & (lane < ROPE_DIM), sin, 0.0)
    sin_from_upper = jnp.where(lane < half, sin, 0.0)

    first_head = lax.broadcasted_iota(jnp.int32, ang.shape, 1) < HEAD_DIM

    def store_doubled(ref, c, t):
        swapped = pltpu.roll(t, HEAD_DIM, axis=1)
        ref[:, 2 * c:2 * c + LANES] = jnp.where(first_head, t, swapped).astype(BF16)
        ref[:, 2 * c + LANES:2 * c + 2 * LANES] = jnp.where(first_head, swapped, t).astype(BF16)

    cw = 512
    qw, kw, vw = q_ref.shape[-1], k_ref.shape[-1] // 2, v_ref.shape[-1] // 2
    for c0 in range(0, w_ref.shape[-1], cw):
        p = jnp.dot(xn_ref[...], w_ref[:, c0:c0 + cw], preferred_element_type=F32)
        for l0 in range(0, cw, LANES):
            col = c0 + l0
            t = p[:, l0:l0 + LANES]
            if col < n_rot:
                t = (t * cos + pltpu.roll(t, half, axis=1) * sin_from_lower
                     - pltpu.roll(t, LANES - half, axis=1) * sin_from_upper)
            if col < qw:
                q_ref[:, col:col + LANES] = t.astype(BF16)
            elif col < qw + kw:
                store_doubled(k_ref, col - qw, t)
            elif col < qw + kw + vw:
                store_doubled(v_ref, col - qw - kw, t)
            else:
                c = col - qw - kw - vw
                u_ref[:, c:c + LANES] = t


def _even_inproj(x, g, w_all, e, pos_b, invf, *, qw, kw, vw, uw, tm=1024):
    t, d = x.shape
    return pl.pallas_call(
        functools.partial(_even_inproj_body, qw + kw // 2),
        grid=(t // tm,),
        in_specs=[
            pl.BlockSpec((tm, d), lambda i: (i, 0)),
            pl.BlockSpec((1, d), lambda i: (0, 0)),
            pl.BlockSpec((None,) + w_all.shape[1:], lambda i: (e, 0, 0), pipeline_mode=pl.Buffered(1)),
            pl.BlockSpec((tm, LANES), lambda i: (i, 0)),
            pl.BlockSpec((1, LANES), lambda i: (0, 0)),
        ],
        out_specs=[
            pl.BlockSpec((tm, qw), lambda i: (i, 0)),
            pl.BlockSpec((tm, kw), lambda i: (i, 0)),
            pl.BlockSpec((tm, vw), lambda i: (i, 0)),
            pl.BlockSpec((tm, uw), lambda i: (i, 0)),
        ],
        out_shape=[
            jax.ShapeDtypeStruct((t, qw), BF16),
            jax.ShapeDtypeStruct((t, kw), BF16),
            jax.ShapeDtypeStruct((t, vw), BF16),
            jax.ShapeDtypeStruct((t, uw), F32),
        ],
        scratch_shapes=[pltpu.VMEM((tm, d), BF16)],
        compiler_params=_params("parallel"),
        name="even_inproj_rotary",
    )(x, g, w_all, pos_b, invf)


def _attn_body(pairs_per_kv, q_ref, kc_ref, kp_ref, vc_ref, vp_ref, sink_ref, o_ref):
    m_step = pl.program_id(1)
    blk = ATTN_BLOCK
    kk = jnp.concatenate([kp_ref[...], kc_ref[...]], axis=0)
    vv = jnp.concatenate([vp_ref[...], vc_ref[...]], axis=0)
    lower = lax.broadcasted_iota(jnp.int32, (blk, LANES), 1) < HEAD_DIM
    key = lax.broadcasted_iota(jnp.int32, (2 * blk, 2 * blk), 0)
    qrow = lax.broadcasted_iota(jnp.int32, (2 * blk, 2 * blk), 1) % blk
    band = (key > qrow) & (key <= qrow + blk)
    upper_rows = lax.broadcasted_iota(jnp.int32, (LANES, blk), 0) >= HEAD_DIM
    n_pairs = q_ref.shape[-1] // LANES
    scale = 1.0 / math.sqrt(HEAD_DIM)

    def scores(t, c):
        hk = c // pairs_per_kv
        qc = q_ref[t * blk:(t + 1) * blk, c * LANES:(c + 1) * LANES] * scale
        q2 = jnp.concatenate([jnp.where(lower, qc, jnp.zeros_like(qc)),
                              jnp.where(lower, jnp.zeros_like(qc), qc)], axis=0)
        kh = kk[t * blk:t * blk + 2 * blk, hk * LANES:(hk + 1) * LANES]
        return lax.dot_general(kh, q2, (((1,), (1,)), ((), ())), preferred_element_type=F32)

    def finish(t, c, s):
        hk = c // pairs_per_kv
        first = (m_step == 0) if t == 0 else False
        valid = band & ((key >= blk) | jnp.logical_not(first))
        s = jnp.where(valid, s, NEG_INF)
        sink = sink_ref[c]
        mx = jnp.maximum(jnp.max(s, axis=0, keepdims=True), sink)
        p = jnp.exp(s - mx)
        den = jnp.sum(p, axis=0, keepdims=True) + jnp.exp(sink - mx)
        p = (p * (1.0 / den)).astype(BF16)
        vh = vv[t * blk:t * blk + 2 * blk, hk * LANES:(hk + 1) * LANES]
        r = lax.dot_general(vh, p, (((0,), (0,)), ((), ())), preferred_element_type=F32)
        o_t = jnp.where(upper_rows, r[:, blk:], r[:, :blk])
        o_ref[t * blk:(t + 1) * blk, c * LANES:(c + 1) * LANES] = o_t.T.astype(BF16)

    units = [(t, c) for t in range(ATTN_BLOCKS_PER_STEP) for c in range(n_pairs)]
    ahead = 4
    pending = [scores(*u) for u in units[:ahead]]
    for n, (t, c) in enumerate(units):
        if n + ahead < len(units):
            pending.append(scores(*units[n + ahead]))
        finish(t, c, pending.pop(0))


def _attention(q, kd, vd, sink_rows_all, e, *, batch, seq, pairs_per_kv):
    t, qw = q.shape
    kw = kd.shape[-1]
    rows = ATTN_BLOCK * ATTN_BLOCKS_PER_STEP
    steps = seq // rows
    bps = ATTN_BLOCKS_PER_STEP
    nblk = seq // ATTN_BLOCK

    def cur(b, m):
        return (b * steps + m, 0)

    def prev(b, m):
        return (b * nblk + jnp.maximum(m * bps - 1, 0), 0)

    return pl.pallas_call(
        functools.partial(_attn_body, pairs_per_kv),
        grid=(batch, steps),
        in_specs=[
            pl.BlockSpec((rows, qw), cur),
            pl.BlockSpec((rows, kw), cur),
            pl.BlockSpec((ATTN_BLOCK, kw), prev),
            pl.BlockSpec((rows, kw), cur),
            pl.BlockSpec((ATTN_BLOCK, kw), prev),
            pl.BlockSpec((qw // LANES,) + sink_rows_all.shape[1:], lambda b, m: (e, 0, 0)),
        ],
        out_specs=pl.BlockSpec((rows, qw), cur),
        out_shape=jax.ShapeDtypeStruct((t, qw), BF16),
        compiler_params=_params("parallel", "arbitrary"),
        name="swa_attention",
    )(q, kd, kd, vd, vd, sink_rows_all)


def _s5_discretise_body(are_ref, aim_ref, logdt_ref, bre_ref, bim_ref, abr_ref, abi_ref, bbr_ref, bbi_ref):
    are, aim = are_ref[...], aim_ref[...]
    dt = jnp.exp(logdt_ref[...])
    mag = jnp.exp(are * dt)
    abar_re, abar_im = mag * jnp.cos(aim * dt), mag * jnp.sin(aim * dt)
    nr, ni = abar_re - 1.0, abar_im
    den = are * are + aim * aim
    coef_re = (nr * are + ni * aim) / den
    coef_im = (ni * are - nr * aim) / den
    bre, bim = bre_ref[...], bim_ref[...]
    abr_ref[...] = abar_re
    abi_ref[...] = abar_im
    bbr_ref[...] = coef_re * bre - coef_im * bim
    bbi_ref[...] = coef_re * bim + coef_im * bre


def _s5_discretise(a_re, a_im, log_dt, b_re, b_im):
    g, p = a_re.shape
    c = b_re.shape[-1]
    col = lambda v: v.reshape(g * p, 1)
    logdt_col = jnp.broadcast_to(log_dt[:, None], (g, p)).reshape(g * p, 1)
    outs = pl.pallas_call(
        _s5_discretise_body,
        out_shape=[jax.ShapeDtypeStruct((g * p, 1), F32)] * 2 + [jax.ShapeDtypeStruct((g * p, c), F32)] * 2,
        name="s5_discretise",
    )(col(a_re), col(a_im), logdt_col, b_re.reshape(g * p, c), b_im.reshape(g * p, c))
    abr, abi, bbr, bbi = outs
    return abr.reshape(g, p), abi.reshape(g, p), bbr.reshape(g, p, c), bbi.reshape(g, p, c)


def _cmul(ar, ai, br, bi):
    return ar * br - ai * bi, ar * bi + ai * br


def _s5_body(n_tiles, steps, u_ref, wb_ref, wc_ref, are_ref, aim_ref, d_ref, wglu_ref, bglu_ref,
             o_ref, us_ref, ys_ref, carry_ref, pw_ref, p8_ref):
    blk = pl.program_id(1)
    sw = are_ref.shape[-1]
    cw = wc_ref.shape[-1]
    nslab = us_ref.shape[0]
    nsub = SUBLANES

    @pl.when(blk == 0)
    def _():
        carry_ref[...] = jnp.zeros_like(carry_ref)
        for kt in range(n_tiles):
            ar, ai = are_ref[kt], aim_ref[kt]
            pr, pi = ar, ai
            for j in range(steps):
                pw_ref[kt, j:j + 1, :sw] = pr
                pw_ref[kt, j:j + 1, sw:] = pi
                if j + 1 < steps:
                    pr, pi = _cmul(pr, pi, ar, ai)
            qr, qi = pr, pi
            for s in range(nsub):
                p8_ref[kt, s:s + 1, :sw] = qr
                p8_ref[kt, s:s + 1, sw:] = qi
                if s + 1 < nsub:
                    qr, qi = _cmul(qr, qi, pr, pi)

    for s in range(nsub):
        for c in range(nslab):
            us_ref[c, pl.ds(s, steps, stride=nsub), :] = u_ref[s * steps:(s + 1) * steps, c * LANES:(c + 1) * LANES]

    def u_cols(c0, n):
        return jnp.concatenate([us_ref[c0 + i] for i in range(n)], axis=1)

    def input_drive(kt):
        ub = u_cols(kt * (cw // LANES), cw // LANES).astype(BF16)
        return jnp.dot(ub, wb_ref[kt], preferred_element_type=F32)

    srow = lax.broadcasted_iota(jnp.int32, (nsub, sw), 0)

    def scan_tile(kt, bu):
        ar = jnp.broadcast_to(are_ref[kt], (nsub, sw))
        ai = jnp.broadcast_to(aim_ref[kt], (nsub, sw))
        hr, hi = jnp.zeros((nsub, sw), F32), jnp.zeros((nsub, sw), F32)
        zero_start = []
        for j in range(steps):
            r0 = j * nsub
            pr, pi = _cmul(ar, ai, hr, hi)
            hr, hi = pr + bu[r0:r0 + nsub, :sw], pi + bu[r0:r0 + nsub, sw:]
            zero_start.append((hr, hi))
        xr, xi = hr, hi
        for d in (1, 2, 4):
            pr, pi = p8_ref[kt, d - 1:d, :sw], p8_ref[kt, d - 1:d, sw:]
            sr = jnp.where(srow >= d, pltpu.roll(xr, d, axis=0), 0.0)
            si = jnp.where(srow >= d, pltpu.roll(xi, d, axis=0), 0.0)
            qr, qi = _cmul(pr, pi, sr, si)
            xr, xi = xr + qr, xi + qi
        cr, ci = carry_ref[kt, :, :sw], carry_ref[kt, :, sw:]
        qr, qi = _cmul(p8_ref[kt, :, :sw], p8_ref[kt, :, sw:], cr, ci)
        xr, xi = xr + qr, xi + qi
        init_r = jnp.where(srow == 0, cr, pltpu.roll(xr, 1, axis=0))
        init_i = jnp.where(srow == 0, ci, pltpu.roll(xi, 1, axis=0))
        carry_ref[kt, :, :sw] = jnp.broadcast_to(xr[nsub - 1:nsub], (nsub, sw))
        carry_ref[kt, :, sw:] = jnp.broadcast_to(xi[nsub - 1:nsub], (nsub, sw))
        rows = []
        for j in range(steps):
            pr, pi = _cmul(pw_ref[kt, j:j + 1, :sw], pw_ref[kt, j:j + 1, sw:], init_r, init_i)
            rows.append(jnp.concatenate([zero_start[j][0] + pr, zero_start[j][1] + pi], axis=1))
        return jnp.concatenate(rows, axis=0)

    bus = [input_drive(kt) for kt in range(n_tiles)]
    ys = [jnp.dot(scan_tile(kt, bus[kt]).astype(BF16), wc_ref[kt], preferred_element_type=F32)
          for kt in range(n_tiles)]
    y = jnp.concatenate(ys, axis=1) + d_ref[...] * u_cols(0, nslab)
    gl = jax.nn.gelu(y)
    gate = jnp.dot(gl.astype(BF16), wglu_ref[...], preferred_element_type=F32) + bglu_ref[...]
    out = gl * jax.nn.sigmoid(gate)
    for c in range(nslab):
        ys_ref[c] = out[:, c * LANES:(c + 1) * LANES]
    for s in range(nsub):
        for c in range(nslab):
            o_ref[s * steps:(s + 1) * steps, c * LANES:(c + 1) * LANES] = (
                ys_ref[c, pl.ds(s, steps, stride=nsub), :].astype(BF16))


def _s5(u, wb, wc, a_re_t, a_im_t, d_rows, w_glu_all, b_glu_rows, e, n_tiles, *, batch, seq, steps=32):
    t, width = u.shape
    _, cw, sw2 = wb.shape
    rows_per_block = steps * SUBLANES
    nblk = seq // rows_per_block
    layer3 = lambda b, k: (e, 0, 0)
    row = lambda b, k: (b * nblk + k, 0)
    return pl.pallas_call(
        functools.partial(_s5_body, n_tiles, steps),
        grid=(batch, nblk),
        in_specs=[
            pl.BlockSpec((rows_per_block, width), row),
            pl.BlockSpec((n_tiles,) + wb.shape[1:], layer3),
            pl.BlockSpec((n_tiles,) + wc.shape[1:], layer3),
            pl.BlockSpec((n_tiles,) + a_re_t.shape[1:], layer3),
            pl.BlockSpec((n_tiles,) + a_im_t.shape[1:], layer3),
            pl.BlockSpec((None,) + d_rows.shape[1:], layer3),
            pl.BlockSpec((None,) + w_glu_all.shape[1:], layer3),
            pl.BlockSpec((None,) + b_glu_rows.shape[1:], layer3),
        ],
        out_specs=pl.BlockSpec((rows_per_block, width), row),
        out_shape=jax.ShapeDtypeStruct((t, width), BF16),
        scratch_shapes=[
            pltpu.VMEM((width // LANES, rows_per_block, LANES), F32),
            pltpu.VMEM((width // LANES, rows_per_block, LANES), F32),
            pltpu.VMEM((n_tiles, SUBLANES, sw2), F32),
            pltpu.VMEM((n_tiles, steps, sw2), F32),
            pltpu.VMEM((n_tiles, SUBLANES, sw2), F32),
        ],
        compiler_params=_params("parallel", "arbitrary"),
        name="s5_scan_glu",
    )(u, wb, wc, a_re_t, a_im_t, d_rows, w_glu_all, b_glu_rows)


HALO_ROWS = 16


def _odd_inproj_body(n_plain, tiles_per_seq, x_ref, xh_ref, g_ref, w_ref, wdt_ref, cw_ref, cb_ref,
                     o_ref, dt_ref, xn_ref):
    i, j = pl.program_id(0), pl.program_id(1)

    @pl.when(j == 0)
    def _():
        g = g_ref[...]
        xn_ref[0:HALO_ROWS, :] = _rms(xh_ref[...], g).astype(BF16)
        xn_ref[HALO_ROWS:, :] = _rms(x_ref[...], g).astype(BF16)
        dt_ref[...] = jnp.dot(xn_ref[HALO_ROWS:, :], wdt_ref[...], preferred_element_type=F32)

    @pl.when(j < n_plain)
    def _():
        o_ref[...] = jnp.dot(xn_ref[HALO_ROWS:, :], w_ref[...], preferred_element_type=F32)

    @pl.when(j >= n_plain)
    def _():
        p = jnp.dot(xn_ref[...], w_ref[...], preferred_element_type=F32)
        seq_start = (i % tiles_per_seq) == 0
        halo = jnp.where(seq_start, 0.0, p[:HALO_ROWS])
        ext = jnp.concatenate([halo, p[HALO_ROWS:]], axis=0)
        acc = cb_ref[...] + cw_ref[M_CONV - 1:M_CONV, :] * p[HALO_ROWS:]
        for k in range(1, M_CONV):
            acc = acc + cw_ref[M_CONV - 1 - k:M_CONV - k, :] * ext[HALO_ROWS - k:ext.shape[0] - k]
        o_ref[...] = jax.nn.silu(acc)


def _odd_inproj(x, g, w, layer, n, wdt, conv_w, conv_b, *, seq, n_plain_cols, tm=1024, tn=1024):
    t, d = x.shape
    assert seq % tm == 0 and n_plain_cols % tn == 0 and n % tn == 0
    n_plain = n_plain_cols // tn
    halo_blocks = tm // HALO_ROWS
    conv_col = lambda i, j: (0, jnp.maximum(j - n_plain, 0))
    return pl.pallas_call(
        functools.partial(_odd_inproj_body, n_plain, seq // tm),
        grid=(t // tm, n // tn),
        in_specs=[
            pl.BlockSpec((tm, d), lambda i, j: (i, 0)),
            pl.BlockSpec((HALO_ROWS, d), lambda i, j: (jnp.maximum(i * halo_blocks - 1, 0), 0)),
            pl.BlockSpec((1, d), lambda i, j: (0, 0)),
            pl.BlockSpec((None, d, tn), lambda i, j: (layer, 0, j)),
            pl.BlockSpec((None,) + wdt.shape[1:], lambda i, j: (layer, 0, 0)),
            pl.BlockSpec((M_CONV, tn), conv_col),
            pl.BlockSpec((1, tn), conv_col),
        ],
        out_specs=[
            pl.BlockSpec((tm, tn), lambda i, j: (i, j)),
            pl.BlockSpec((tm, wdt.shape[-1]), lambda i, j: (i, 0)),
        ],
        out_shape=[jax.ShapeDtypeStruct((t, n), F32), jax.ShapeDtypeStruct((t, wdt.shape[-1]), F32)],
        scratch_shapes=[pltpu.VMEM((HALO_ROWS + tm, d), BF16)],
        compiler_params=_params("parallel", "arbitrary"),
        name="odd_inproj_conv",
    )(x, x, g, w, wdt, conv_w, conv_b)


def _ssd_body(n_groups, z_ref, xs_ref, bm_ref, cm_ref, dt_ref, dtb_ref, alog_ref, dsk_ref, ng_ref,
              o_ref, state_ref, y_ref):
    c = pl.program_id(1)
    L = M_CHUNK
    inner = xs_ref.shape[-1]
    gw = bm_ref.shape[-1] // n_groups
    hw = inner // n_groups

    @pl.when(c == 0)
    def _():
        state_ref[...] = jnp.zeros_like(state_ref)

    dt = jax.nn.softplus(dt_ref[...] + dtb_ref[...])
    a = -jnp.exp(alog_ref[...])
    acum = a * dt
    rowi = lax.broadcasted_iota(jnp.int32, acum.shape, 0)
    k = 1
    while k < L:
        acum = acum + jnp.where(rowi >= k, pltpu.roll(acum, k, axis=0), 0.0)
        k *= 2
    src = acum - jnp.log(dt)
    src_t = src.T
    a_last = acum[L - 1:L, :]

    li = lax.broadcasted_iota(jnp.int32, (L, L), 0)
    si = lax.broadcasted_iota(jnp.int32, (L, L), 1)
    causal = li >= si
    lane = lax.broadcasted_iota(jnp.int32, (L, LANES), 1)
    lower = lane < M_HEAD_DIM
    heads_per_group = hw // M_HEAD_DIM

    for g in range(n_groups):
        cgb = cm_ref[:, g * gw:(g + 1) * gw].astype(BF16)
        bgb = bm_ref[:, g * gw:(g + 1) * gw].astype(BF16)
        cb = lax.dot_general(cgb, bgb, (((1,), (1,)), ((), ())), preferred_element_type=F32)
        prev = state_ref[:, g * hw:(g + 1) * hw]
        y_off = jnp.dot(cgb, prev.astype(BF16), preferred_element_type=F32)
        xd_parts = []
        for pr in range(heads_per_group // 2):
            h0 = g * heads_per_group + 2 * pr
            col0 = g * hw + pr * LANES
            xp = xs_ref[:, col0:col0 + LANES]
            ms = []
            for h in (h0, h0 + 1):
                seg = acum[:, h:h + 1] - src_t[h:h + 1, :]
                ms.append((cb * jnp.exp(jnp.where(causal, seg, -jnp.inf))).astype(BF16))
            lhs = jnp.concatenate(ms, axis=1)
            xpb = xp.astype(BF16)
            rhs = jnp.concatenate([jnp.where(lower, xpb, jnp.zeros_like(xpb)),
                                   jnp.where(lower, jnp.zeros_like(xpb), xpb)], axis=0)
            y_diag = jnp.dot(lhs, rhs, preferred_element_type=F32)
            al = jnp.where(lower, acum[:, h0:h0 + 1], acum[:, h0 + 1:h0 + 2])
            srcl = jnp.where(lower, src[:, h0:h0 + 1], src[:, h0 + 1:h0 + 2])
            alast = jnp.where(lower[0:1], a_last[:, h0:h0 + 1], a_last[:, h0 + 1:h0 + 2])
            y_ref[:, col0:col0 + LANES] = (y_diag + jnp.exp(al) * y_off[:, pr * LANES:(pr + 1) * LANES]
                                           + dsk_ref[:, col0:col0 + LANES] * xp)
            xd_parts.append((xp * jnp.exp(alast - srcl)).astype(BF16))
            state_ref[:, col0:col0 + LANES] = state_ref[:, col0:col0 + LANES] * jnp.exp(alast)
        xd = jnp.concatenate(xd_parts, axis=1)
        state_ref[:, g * hw:(g + 1) * hw] += lax.dot_general(
            bgb, xd, (((0,), (0,)), ((), ())), preferred_element_type=F32)

    for g in range(n_groups):
        sl = slice(g * hw, (g + 1) * hw)
        y = y_ref[:, sl] * jax.nn.silu(z_ref[:, sl])
        y = y * lax.rsqrt(jnp.mean(y * y, axis=-1, keepdims=True) + NORM_EPS)
        o_ref[:, sl] = (y * ng_ref[:, sl]).astype(BF16)


def _ssd(zxbc, dt_raw, dt_bias, a_log, d_row, norm_g, *, batch, seq, inner, n_groups):
    t = zxbc.shape[0]
    nchunks = seq // M_CHUNK
    gn = n_groups * M_STATE
    row = lambda b, c: (b * nchunks + c, 0)
    const = lambda b, c: (0, 0)
    return pl.pallas_call(
        functools.partial(_ssd_body, n_groups),
        grid=(batch, nchunks),
        in_specs=[
            pl.BlockSpec((M_CHUNK, inner), row),
            pl.BlockSpec((M_CHUNK, inner), lambda b, c: (b * nchunks + c, 1)),
            pl.BlockSpec((M_CHUNK, gn), lambda b, c: (b * nchunks + c, 2 * inner // gn)),
            pl.BlockSpec((M_CHUNK, gn), lambda b, c: (b * nchunks + c, 2 * inner // gn + 1)),
            pl.BlockSpec((M_CHUNK, LANES), row),
            pl.BlockSpec((1, LANES), const),
            pl.BlockSpec((1, LANES), const),
            pl.BlockSpec((1, inner), const),
            pl.BlockSpec((1, inner), const),
        ],
        out_specs=pl.BlockSpec((M_CHUNK, inner), row),
        out_shape=jax.ShapeDtypeStruct((t, inner), BF16),
        scratch_shapes=[pltpu.VMEM((M_STATE, inner), F32),
                        pltpu.VMEM((M_CHUNK, inner), F32)],
        compiler_params=_params("parallel", "arbitrary"),
        name="ssd_chunk",
    )(zxbc, zxbc, zxbc, zxbc, dt_raw, dt_bias, a_log, d_row, norm_g)


def _cast_body(wt_ref, o_ref):
    o_ref[...] = wt_ref[...].T.astype(BF16)


def _cast_leading_cols(w, n_cols, *, tn=1024):
    layers, rows, _ = w.shape
    assert n_cols % tn == 0
    return pl.pallas_call(
        _cast_body,
        grid=(layers, n_cols // tn),
        in_specs=[pl.BlockSpec((None, tn, rows), lambda l, j: (l, j, 0))],
        out_specs=pl.BlockSpec((None, rows, tn), lambda l, j: (l, 0, j)),
        out_shape=jax.ShapeDtypeStruct((layers, rows, n_cols), BF16),
        compiler_params=_params("parallel", "parallel"),
        name="cast_weight_columns",
    )(jnp.swapaxes(w, 1, 2))


def _block_diag_tiles(blocks, n_tiles):
    g, r, c = blocks.shape
    gpt = g // n_tiles
    eye = jnp.eye(gpt, dtype=blocks.dtype)
    b = blocks.reshape(n_tiles, gpt, r, c)
    out = jnp.einsum('tgrc,gh->tgrhc', b, eye)
    return out.reshape(n_tiles, gpt * r, gpt * c)


def _rope_inv_freq_row():
    half = ROPE_DIM // 2
    inv_freq = jnp.exp(-math.log(ROPE_THETA) * jnp.arange(half, dtype=F32) * (2.0 / ROPE_DIM))
    lane = jnp.arange(LANES) % HEAD_DIM
    return jnp.where(lane < ROPE_DIM, inv_freq[lane % half], 0.0).reshape(1, LANES).astype(F32)


def _even_prep(w_in, sinks, a_re, a_im, log_dt, b_re, b_im, c_re, c_im, d_skip, w_glu, b_glu, w_out):
    n_layers, d, _ = w_in.shape
    n_q = sinks.shape[-1]
    a_width = n_q * HEAD_DIM
    groups, state = a_re.shape[1:]
    s5_width = groups * S5_GROUP
    kv_width = (w_in.shape[-1] - a_width - s5_width) // 2
    n_kv = kv_width // HEAD_DIM
    grp = n_q // n_kv
    assert grp % 2 == 0 and a_width % LANES == 0 and a_width == s5_width

    assert kv_width % LANES == 0
    w_all = w_in.astype(BF16)

    sink_rows = jnp.repeat(sinks.astype(F32).reshape(n_layers * n_q // 2, 2), ATTN_BLOCK, axis=1)
    sink_rows = sink_rows.reshape(n_layers * n_q // 2, 1, 2 * ATTN_BLOCK)

    fold = lambda v: v.astype(F32).reshape((n_layers * groups,) + v.shape[2:])
    abr, abi, bbr, bbi = _s5_discretise(fold(a_re), fold(a_im), fold(log_dt), fold(b_re), fold(b_im))
    n_tiles = groups // S5_GROUPS_PER_TILE
    all_tiles = n_layers * n_tiles
    to_in = lambda bb: _block_diag_tiles(jnp.swapaxes(bb, 1, 2), all_tiles)
    wb = jnp.concatenate([to_in(bbr), to_in(bbi)], axis=-1).astype(BF16)
    to_out = lambda cc: _block_diag_tiles(jnp.swapaxes(fold(cc), 1, 2), all_tiles)
    wc = jnp.concatenate([to_out(c_re), -to_out(c_im)], axis=1).astype(BF16)
    return dict(
        w_all=w_all, sink_rows=sink_rows, wb=wb, wc=wc, n_tiles=n_tiles,
        a_re_t=abr.reshape(all_tiles, 1, -1), a_im_t=abi.reshape(all_tiles, 1, -1),
        d_rows=d_skip.astype(F32).reshape(n_layers, 1, s5_width), b_glu_rows=b_glu.astype(F32).reshape(n_layers, 1, s5_width),
        w_glu=w_glu.astype(BF16), w_out=w_out.astype(BF16),
        a_width=a_width, kv_width=kv_width, s5_width=s5_width, pairs_per_kv=grp // 2)


def _even_mixer(x, pos_b, norm_g, p, e, *, batch, seq):
    d = x.shape[-1]
    a_width, kv_width, s5_width = p["a_width"], p["kv_width"], p["s5_width"]
    q, kd, vd, u = _even_inproj(x, norm_g, p["w_all"], e, pos_b, _rope_inv_freq_row(),
                                qw=a_width, kw=2 * kv_width, vw=2 * kv_width, uw=s5_width)
    attn = _attention(q, kd, vd, p["sink_rows"], e, batch=batch, seq=seq, pairs_per_kv=p["pairs_per_kv"])
    ssm = _s5(u, p["wb"], p["wc"], p["a_re_t"], p["a_im_t"], p["d_rows"], p["w_glu"], p["b_glu_rows"], e,
              p["n_tiles"], batch=batch, seq=seq)
    tm = 512
    terms = [
        (attn, pl.BlockSpec((tm, a_width), lambda i: (i, 0)),
         p["w_out"], pl.BlockSpec((None, a_width, d), lambda i: (e, 0, 0))),
        (ssm, pl.BlockSpec((tm, s5_width), lambda i: (i, 0)),
         p["w_out"], pl.BlockSpec((None, s5_width, d), lambda i: (e, 1, 0))),
    ]
    return _outproj(x, terms, tm=tm)


def _odd_mixer(x, norm_g, w_main_all, w_dt_all, conv_w, conv_b, dt_bias, a_log, d_skip, norm_out, w_out_all, o, *,
               batch, seq):
    d = x.shape[-1]
    inner = norm_out.shape[0]
    heads = a_log.shape[0]
    conv_dim = conv_w.shape[-1]
    n_groups = (conv_dim - inner) // (2 * M_STATE)
    assert heads <= LANES and inner // heads == M_HEAD_DIM
    pad = LANES - heads
    zxbc, dt_raw = _odd_inproj(x, norm_g, w_main_all, o, inner + conv_dim, w_dt_all, conv_w.astype(F32),
                               conv_b.astype(F32).reshape(1, conv_dim), seq=seq, n_plain_cols=inner)
    row = lambda v: jnp.pad(v.astype(F32), (0, pad)).reshape(1, LANES)
    y = _ssd(zxbc, dt_raw, row(dt_bias), row(a_log), jnp.repeat(d_skip.astype(F32), M_HEAD_DIM).reshape(1, inner),
             norm_out.astype(F32).reshape(1, inner), batch=batch, seq=seq, inner=inner, n_groups=n_groups)
    tm = 512
    return _outproj(x, [(y, pl.BlockSpec((tm, inner), lambda i: (i, 0)),
                         w_out_all, pl.BlockSpec((None, inner, d), lambda i: (o, 0, 0)))], tm=tm)


def kernel(x, positions, norm_ffn1, ffn1_gate, ffn1_up, ffn1_down, norm_mix, norm_ffn2, ffn2_gate, ffn2_up,
           ffn2_down, ev_w_in, ev_sinks, s5_a_re, s5_a_im, s5_log_dt, s5_b_re, s5_b_im, s5_c_re, s5_c_im, s5_d,
           s5_w_glu, s5_b_glu, ev_w_out, m_w_in, m_conv_w, m_conv_b, m_dt_bias, m_a_log, m_d, m_norm, m_w_out,
           final_norm):
    batch, seq, d = x.shape
    depth = norm_ffn1.shape[0]
    t = batch * seq
    h = x.reshape(t, d).astype(F32)
    pos_b = jnp.broadcast_to(positions.reshape(t, 1).astype(F32), (t, LANES))
    g1, g2, gm = (v.astype(F32).reshape(depth, 1, d) for v in (norm_ffn1, norm_ffn2, norm_mix))
    even = _even_prep(ev_w_in, ev_sinks, s5_a_re, s5_a_im, s5_log_dt, s5_b_re, s5_b_im, s5_c_re, s5_c_im, s5_d,
                      s5_w_glu, s5_b_glu, ev_w_out)
    m_w_out_all = m_w_out.astype(BF16)
    m_main = m_norm.shape[-1] + m_conv_w.shape[-1]
    m_w_main_all = _cast_leading_cols(m_w_in, m_main)
    m_w_dt_all = jnp.pad(m_w_in[:, :, m_main:], ((0, 0), (0, 0), (0, LANES - m_a_log.shape[-1]))).astype(BF16)
    ffn_f32 = [(ffn1_gate, ffn1_up, ffn1_down), (ffn2_gate, ffn2_up, ffn2_down)]
    w_cur = tuple(w[0].astype(BF16) for w in ffn_f32[0])
    for layer in range(depth):
        h, w_cur = _ffn(h, g1, *w_cur, layer, next_f32=ffn_f32[1] + (layer,))
        if layer % 2 == 0:
            e = layer // 2
            h = _even_mixer(h, pos_b, gm[layer], even, e, batch=batch, seq=seq)
        else:
            o = layer // 2
            h = _odd_mixer(h, gm[layer], m_w_main_all, m_w_dt_all, m_conv_w[o], m_conv_b[o], m_dt_bias[o], m_a_log[o],
                           m_d[o], m_norm[o], m_w_out_all, o, batch=batch, seq=seq)
        if layer == depth - 1:
            h, _ = _ffn(h, g2, *w_cur, layer, final_g=final_norm.astype(F32).reshape(1, d))
        else:
            h, w_cur = _ffn(h, g2, *w_cur, layer, next_f32=ffn_f32[0] + (layer + 1,))
    return h.reshape(batch, seq, d).astype(x.dtype)
```

```python
import functools
import math

import jax
import jax.numpy as jnp
from jax import lax
from jax.experimental import pallas as pl
from jax.experimental.pallas import tpu as pltpu

F32 = jnp.float32
BF16 = jnp.bfloat16

NORM_EPS = 1e-5
NEG_INF = -1e30
LANES = 128
SUBLANES = 8
VMEM_LIMIT_BYTES = 60 * 1024 * 1024

HEAD_DIM = 64
ATTN_BLOCK = 128
ROPE_THETA = 500000.0
ROPE_DIM = HEAD_DIM // 4
ATTN_BLOCKS_PER_STEP = 4
S5_GROUP = 16
S5_STATE = 64
S5_GROUPS_PER_TILE = 16
M_HEAD_DIM = 64
M_STATE = 128
M_CHUNK = 128
M_CONV = 4


def _params(*sem):
    return pltpu.CompilerParams(dimension_semantics=sem, vmem_limit_bytes=VMEM_LIMIT_BYTES)


def _rms(x, g):
    return x * lax.rsqrt(jnp.mean(x * x, axis=-1, keepdims=True) + NORM_EPS) * g


def _ffn_body(final, cast_next, x_ref, g_ref, wg_ref, wu_ref, wd_ref, *rest):
    xn_ref = rest[-1]
    o_ref = rest[-5] if cast_next else rest[-2]
    j = pl.program_id(1)
    if cast_next:
        n_in = 1 if final else 0
        for src, dst in zip(rest[n_in:n_in + 3], rest[-4:-1]):
            dst[...] = src[...].astype(BF16)

    @pl.when(j == 0)
    def _():
        x = x_ref[...]
        xn_ref[...] = _rms(x, g_ref[...]).astype(BF16)
        o_ref[...] = x

    xn = xn_ref[...]
    gate = jnp.dot(xn, wg_ref[...], preferred_element_type=F32)
    up = jnp.dot(xn, wu_ref[...], preferred_element_type=F32)
    h = (jax.nn.silu(gate) * up * 0.5).astype(BF16)
    o_ref[...] += jnp.dot(h, wd_ref[...], preferred_element_type=F32)

    if final:
        @pl.when(j == pl.num_programs(1) - 1)
        def _():
            o_ref[...] = _rms(o_ref[...], rest[0][...])


def _ffn(x, g, wg, wu, wd, layer, final_g=None, next_f32=None, *, tm=1024, tf=512):
    t, d = x.shape
    dff = wg.shape[-1]
    ni, nj = t // tm, dff // tf
    in_specs = [
        pl.BlockSpec((tm, d), lambda i, j: (i, 0)),
        pl.BlockSpec((None, 1, d), lambda i, j: (layer, 0, 0)),
        pl.BlockSpec((d, tf), lambda i, j: (0, j)),
        pl.BlockSpec((d, tf), lambda i, j: (0, j)),
        pl.BlockSpec((tf, d), lambda i, j: (j, 0)),
    ]
    args = [x, g, wg, wu, wd]
    out_specs = [pl.BlockSpec((tm, d), lambda i, j: (i, 0))]
    out_shape = [jax.ShapeDtypeStruct((t, d), F32)]
    if final_g is not None:
        in_specs.append(pl.BlockSpec((1, d), lambda i, j: (0, 0)))
        args.append(final_g)
    if next_f32 is not None:
        ng, nu, nd, nl = next_f32
        assert d % ni == 0 and dff % nj == 0
        up_blk, dn_blk = (d // ni, dff // nj), (dff // nj, d // ni)
        in_specs += [pl.BlockSpec((None,) + up_blk, lambda i, j: (nl, i, j)),
                     pl.BlockSpec((None,) + up_blk, lambda i, j: (nl, i, j)),
                     pl.BlockSpec((None,) + dn_blk, lambda i, j: (nl, j, i))]
        args += [ng, nu, nd]
        out_specs += [pl.BlockSpec(up_blk, lambda i, j: (i, j)), pl.BlockSpec(up_blk, lambda i, j: (i, j)),
                      pl.BlockSpec(dn_blk, lambda i, j: (j, i))]
        out_shape += [jax.ShapeDtypeStruct((d, dff), BF16)] * 2 + [jax.ShapeDtypeStruct((dff, d), BF16)]
    outs = pl.pallas_call(
        functools.partial(_ffn_body, final_g is not None, next_f32 is not None),
        grid=(ni, nj),
        in_specs=in_specs,
        out_specs=out_specs,
        out_shape=out_shape,
        scratch_shapes=[pltpu.VMEM((tm, d), BF16)],
        compiler_params=_params("parallel", "arbitrary"),
        name="swiglu_ffn",
    )(*args)
    return outs[0], tuple(outs[1:])


def _outproj_body(n_terms, x_ref, *refs):
    o_ref = refs[-1]
    acc = x_ref[...]
    for i in range(n_terms):
        acc = acc + jnp.dot(refs[2 * i][...], refs[2 * i + 1][...], preferred_element_type=F32)
    o_ref[...] = acc


def _outproj(x, terms, *, tm):
    t, d = x.shape
    in_specs = [pl.BlockSpec((tm, d), lambda i: (i, 0))]
    args = [x]
    for a, a_spec, w, w_spec in terms:
        in_specs += [a_spec, w_spec]
        args += [a, w]
    return pl.pallas_call(
        functools.partial(_outproj_body, len(terms)),
        grid=(t // tm,),
        in_specs=in_specs,
        out_specs=pl.BlockSpec((tm, d), lambda i: (i, 0)),
        out_shape=jax.ShapeDtypeStruct((t, d), F32),
        compiler_params=_params("parallel"),
        name="residual_outproj",
    )(*args)


def _even_inproj_body(n_rot, x_ref, g_ref, w_ref, pos_ref, invf_ref, q_ref, k_ref, v_ref, u_ref, xn_ref):
    xn_ref[...] = _rms(x_ref[...], g_ref[...]).astype(BF16)
    ang = pos_ref[...] * invf_ref[...]
    cos, sin = jnp.cos(ang), jnp.sin(ang)
    lane = lax.broadcasted_iota(jnp.int32, ang.shape, 1) % HEAD_DIM
    half = ROPE_DIM // 2
    sin_from_lower = jnp.where((lane >= half) & (lane < ROPE_DIM), sin, 0.0)
    sin_from_upper = jnp.where(lane < half, sin, 0.0)

    first_head = lax.broadcasted_iota(jnp.int32, ang.shape, 1) < HEAD_DIM

    def store_doubled(ref, c, t):
        swapped = pltpu.roll(t, HEAD_DIM, axis=1)
        ref[:, 2 * c:2 * c + LANES] = jnp.where(first_head, t, swapped).astype(BF16)
        ref[:, 2 * c + LANES:2 * c + 2 * LANES] = jnp.where(first_head, swapped, t).astype(BF16)

    cw = 512
    qw, kw, vw = q_ref.shape[-1], k_ref.shape[-1] // 2, v_ref.shape[-1] // 2
    for c0 in range(0, w_ref.shape[-1], cw):
        p = jnp.dot(xn_ref[...], w_ref[:, c0:c0 + cw], preferred_element_type=F32)
        for l0 in range(0, cw, LANES):
            col = c0 + l0
            t = p[:, l0:l0 + LANES]
            if col < n_rot:
                t = (t * cos + pltpu.roll(t, half, axis=1) * sin_from_lower
                     - pltpu.roll(t, LANES - half, axis=1) * sin_from_upper)
            if col < qw:
                q_ref[:, col:col + LANES] = t.astype(BF16)
            elif col < qw + kw:
                store_doubled(k_ref, col - qw, t)
            elif col < qw + kw + vw:
                store_doubled(v_ref, col - qw - kw, t)
            else:
                c = col - qw - kw - vw
                u_ref[:, c:c + LANES] = t


def _even_inproj(x, g, w_all, e, pos_b, invf, *, qw, kw, vw, uw, tm=512):
    t, d = x.shape
    return pl.pallas_call(
        functools.partial(_even_inproj_body, qw + kw // 2),
        grid=(t // tm,),
        in_specs=[
            pl.BlockSpec((tm, d), lambda i: (i, 0)),
            pl.BlockSpec((1, d), lambda i: (0, 0)),
            pl.BlockSpec((None,) + w_all.shape[1:], lambda i: (e, 0, 0)),
            pl.BlockSpec((tm, LANES), lambda i: (i, 0)),
            pl.BlockSpec((1, LANES), lambda i: (0, 0)),
        ],
        out_specs=[
            pl.BlockSpec((tm, qw), lambda i: (i, 0)),
            pl.BlockSpec((tm, kw), lambda i: (i, 0)),
            pl.BlockSpec((tm, vw), lambda i: (i, 0)),
            pl.BlockSpec((tm, uw), lambda i: (i, 0)),
        ],
        out_shape=[
            jax.ShapeDtypeStruct((t, qw), BF16),
            jax.ShapeDtypeStruct((t, kw), BF16),
            jax.ShapeDtypeStruct((t, vw), BF16),
            jax.ShapeDtypeStruct((t, uw), F32),
        ],
        scratch_shapes=[pltpu.VMEM((tm, d), BF16)],
        compiler_params=_params("parallel"),
        name="even_inproj_rotary",
    )(x, g, w_all, pos_b, invf)


def _attn_body(pairs_per_kv, q_ref, kc_ref, kp_ref, vc_ref, vp_ref, sink_ref, o_ref):
    m_step = pl.program_id(1)
    blk = ATTN_BLOCK
    kk = jnp.concatenate([kp_ref[...], kc_ref[...]], axis=0)
    vv = jnp.concatenate([vp_ref[...], vc_ref[...]], axis=0)
    lower = lax.broadcasted_iota(jnp.int32, (blk, LANES), 1) < HEAD_DIM
    key = lax.broadcasted_iota(jnp.int32, (2 * blk, 2 * blk), 0)
    qrow = lax.broadcasted_iota(jnp.int32, (2 * blk, 2 * blk), 1) % blk
    band = (key > qrow) & (key <= qrow + blk)
    upper_rows = lax.broadcasted_iota(jnp.int32, (LANES, blk), 0) >= HEAD_DIM
    n_pairs = q_ref.shape[-1] // LANES
    scale = 1.0 / math.sqrt(HEAD_DIM)

    def scores(t, c):
        hk = c // pairs_per_kv
        qc = q_ref[t * blk:(t + 1) * blk, c * LANES:(c + 1) * LANES] * scale
        q2 = jnp.concatenate([jnp.where(lower, qc, jnp.zeros_like(qc)),
                              jnp.where(lower, jnp.zeros_like(qc), qc)], axis=0)
        kh = kk[t * blk:t * blk + 2 * blk, hk * LANES:(hk + 1) * LANES]
        return lax.dot_general(kh, q2, (((1,), (1,)), ((), ())), preferred_element_type=F32)

    def finish(t, c, s):
        hk = c // pairs_per_kv
        first = (m_step == 0) if t == 0 else False
        valid = band & ((key >= blk) | jnp.logical_not(first))
        s = jnp.where(valid, s, NEG_INF)
        sink = sink_ref[c]
        mx = jnp.maximum(jnp.max(s, axis=0, keepdims=True), sink)
        p = jnp.exp(s - mx)
        den = jnp.sum(p, axis=0, keepdims=True) + jnp.exp(sink - mx)
        p = (p * (1.0 / den)).astype(BF16)
        vh = vv[t * blk:t * blk + 2 * blk, hk * LANES:(hk + 1) * LANES]
        r = lax.dot_general(vh, p, (((0,), (0,)), ((), ())), preferred_element_type=F32)
        o_t = jnp.where(upper_rows, r[:, blk:], r[:, :blk])
        o_ref[t * blk:(t + 1) * blk, c * LANES:(c + 1) * LANES] = o_t.T.astype(BF16)

    units = [(t, c) for t in range(ATTN_BLOCKS_PER_STEP) for c in range(n_pairs)]
    ahead = 4
    pending = [scores(*u) for u in units[:ahead]]
    for n, (t, c) in enumerate(units):
        if n + ahead < len(units):
            pending.append(scores(*units[n + ahead]))
        finish(t, c, pending.pop(0))


def _attention(q, kd, vd, sink_rows_all, e, *, batch, seq, pairs_per_kv):
    t, qw = q.shape
    kw = kd.shape[-1]
    rows = ATTN_BLOCK * ATTN_BLOCKS_PER_STEP
    steps = seq // rows
    bps = ATTN_BLOCKS_PER_STEP
    nblk = seq // ATTN_BLOCK

    def cur(b, m):
        return (b * steps + m, 0)

    def prev(b, m):
        return (b * nblk + jnp.maximum(m * bps - 1, 0), 0)

    return pl.pallas_call(
        functools.partial(_attn_body, pairs_per_kv),
        grid=(batch, steps),
        in_specs=[
            pl.BlockSpec((rows, qw), cur),
            pl.BlockSpec((rows, kw), cur),
            pl.BlockSpec((ATTN_BLOCK, kw), prev),
            pl.BlockSpec((rows, kw), cur),
            pl.BlockSpec((ATTN_BLOCK, kw), prev),
            pl.BlockSpec((qw // LANES,) + sink_rows_all.shape[1:], lambda b, m: (e, 0, 0)),
        ],
        out_specs=pl.BlockSpec((rows, qw), cur),
        out_shape=jax.ShapeDtypeStruct((t, qw), BF16),
        compiler_params=_params("parallel", "arbitrary"),
        name="swa_attention",
    )(q, kd, kd, vd, vd, sink_rows_all)


def _s5_discretise_body(are_ref, aim_ref, logdt_ref, bre_ref, bim_ref, abr_ref, abi_ref, bbr_ref, bbi_ref):
    are, aim = are_ref[...], aim_ref[...]
    dt = jnp.exp(logdt_ref[...])
    mag = jnp.exp(are * dt)
    abar_re, abar_im = mag * jnp.cos(aim * dt), mag * jnp.sin(aim * dt)
    nr, ni = abar_re - 1.0, abar_im
    den = are * are + aim * aim
    coef_re = (nr * are + ni * aim) / den
    coef_im = (ni * are - nr * aim) / den
    bre, bim = bre_ref[...], bim_ref[...]
    abr_ref[...] = abar_re
    abi_ref[...] = abar_im
    bbr_ref[...] = coef_re * bre - coef_im * bim
    bbi_ref[...] = coef_re * bim + coef_im * bre


def _s5_discretise(a_re, a_im, log_dt, b_re, b_im):
    g, p = a_re.shape
    c = b_re.shape[-1]
    col = lambda v: v.reshape(g * p, 1)
    logdt_col = jnp.broadcast_to(log_dt[:, None], (g, p)).reshape(g * p, 1)
    outs = pl.pallas_call(
        _s5_discretise_body,
        out_shape=[jax.ShapeDtypeStruct((g * p, 1), F32)] * 2 + [jax.ShapeDtypeStruct((g * p, c), F32)] * 2,
        name="s5_discretise",
    )(col(a_re), col(a_im), logdt_col, b_re.reshape(g * p, c), b_im.reshape(g * p, c))
    abr, abi, bbr, bbi = outs
    return abr.reshape(g, p), abi.reshape(g, p), bbr.reshape(g, p, c), bbi.reshape(g, p, c)


def _cmul(ar, ai, br, bi):
    return ar * br - ai * bi, ar * bi + ai * br


def _s5_body(n_tiles, steps, u_ref, wb_ref, wc_ref, are_ref, aim_ref, d_ref, wglu_ref, bglu_ref,
             o_ref, us_ref, ys_ref, carry_ref, pw_ref, p8_ref):
    blk = pl.program_id(1)
    sw = are_ref.shape[-1]
    cw = wc_ref.shape[-1]
    nslab = us_ref.shape[0]
    nsub = SUBLANES

    @pl.when(blk == 0)
    def _():
        carry_ref[...] = jnp.zeros_like(carry_ref)
        for kt in range(n_tiles):
            ar, ai = are_ref[kt], aim_ref[kt]
            pr, pi = ar, ai
            for j in range(steps):
                pw_ref[kt, j:j + 1, :sw] = pr
                pw_ref[kt, j:j + 1, sw:] = pi
                if j + 1 < steps:
                    pr, pi = _cmul(pr, pi, ar, ai)
            qr, qi = pr, pi
            for s in range(nsub):
                p8_ref[kt, s:s + 1, :sw] = qr
                p8_ref[kt, s:s + 1, sw:] = qi
                if s + 1 < nsub:
                    qr, qi = _cmul(qr, qi, pr, pi)

    for s in range(nsub):
        for c in range(nslab):
            us_ref[c, pl.ds(s, steps, stride=nsub), :] = u_ref[s * steps:(s + 1) * steps, c * LANES:(c + 1) * LANES]

    def u_cols(c0, n):
        return jnp.concatenate([us_ref[c0 + i] for i in range(n)], axis=1)

    def input_drive(kt):
        ub = u_cols(kt * (cw // LANES), cw // LANES).astype(BF16)
        return jnp.dot(ub, wb_ref[kt], preferred_element_type=F32)

    srow = lax.broadcasted_iota(jnp.int32, (nsub, sw), 0)

    def scan_tile(kt, bu):
        ar = jnp.broadcast_to(are_ref[kt], (nsub, sw))
        ai = jnp.broadcast_to(aim_ref[kt], (nsub, sw))
        hr, hi = jnp.zeros((nsub, sw), F32), jnp.zeros((nsub, sw), F32)
        zero_start = []
        for j in range(steps):
            r0 = j * nsub
            pr, pi = _cmul(ar, ai, hr, hi)
            hr, hi = pr + bu[r0:r0 + nsub, :sw], pi + bu[r0:r0 + nsub, sw:]
            zero_start.append((hr, hi))
        xr, xi = hr, hi
        for d in (1, 2, 4):
            pr, pi = p8_ref[kt, d - 1:d, :sw], p8_ref[kt, d - 1:d, sw:]
            sr = jnp.where(srow >= d, pltpu.roll(xr, d, axis=0), 0.0)
            si = jnp.where(srow >= d, pltpu.roll(xi, d, axis=0), 0.0)
            qr, qi = _cmul(pr, pi, sr, si)
            xr, xi = xr + qr, xi + qi
        cr, ci = carry_ref[kt, :, :sw], carry_ref[kt, :, sw:]
        qr, qi = _cmul(p8_ref[kt, :, :sw], p8_ref[kt, :, sw:], cr, ci)
        xr, xi = xr + qr, xi + qi
        init_r = jnp.where(srow == 0, cr, pltpu.roll(xr, 1, axis=0))
        init_i = jnp.where(srow == 0, ci, pltpu.roll(xi, 1, axis=0))
        carry_ref[kt, :, :sw] = jnp.broadcast_to(xr[nsub - 1:nsub], (nsub, sw))
        carry_ref[kt, :, sw:] = jnp.broadcast_to(xi[nsub - 1:nsub], (nsub, sw))
        rows = []
        for j in range(steps):
            pr, pi = _cmul(pw_ref[kt, j:j + 1, :sw], pw_ref[kt, j:j + 1, sw:], init_r, init_i)
            rows.append(jnp.concatenate([zero_start[j][0] + pr, zero_start[j][1] + pi], axis=1))
        return jnp.concatenate(rows, axis=0)

    bus = [input_drive(kt) for kt in range(n_tiles)]
    ys = [jnp.dot(scan_tile(kt, bus[kt]).astype(BF16), wc_ref[kt], preferred_element_type=F32)
          for kt in range(n_tiles)]
    y = jnp.concatenate(ys, axis=1) + d_ref[...] * u_cols(0, nslab)
    gl = jax.nn.gelu(y)
    gate = jnp.dot(gl.astype(BF16), wglu_ref[...], preferred_element_type=F32) + bglu_ref[...]
    out = gl * jax.nn.sigmoid(gate)
    for c in range(nslab):
        ys_ref[c] = out[:, c * LANES:(c + 1) * LANES]
    for s in range(nsub):
        for c in range(nslab):
            o_ref[s * steps:(s + 1) * steps, c * LANES:(c + 1) * LANES] = (
                ys_ref[c, pl.ds(s, steps, stride=nsub), :].astype(BF16))


def _s5(u, wb, wc, a_re_t, a_im_t, d_rows, w_glu_all, b_glu_rows, e, n_tiles, *, batch, seq, steps=32):
    t, width = u.shape
    _, cw, sw2 = wb.shape
    rows_per_block = steps * SUBLANES
    nblk = seq // rows_per_block
    layer3 = lambda b, k: (e, 0, 0)
    row = lambda b, k: (b * nblk + k, 0)
    return pl.pallas_call(
        functools.partial(_s5_body, n_tiles, steps),
        grid=(batch, nblk),
        in_specs=[
            pl.BlockSpec((rows_per_block, width), row),
            pl.BlockSpec((n_tiles,) + wb.shape[1:], layer3),
            pl.BlockSpec((n_tiles,) + wc.shape[1:], layer3),
            pl.BlockSpec((n_tiles,) + a_re_t.shape[1:], layer3),
            pl.BlockSpec((n_tiles,) + a_im_t.shape[1:], layer3),
            pl.BlockSpec((None,) + d_rows.shape[1:], layer3),
            pl.BlockSpec((None,) + w_glu_all.shape[1:], layer3),
            pl.BlockSpec((None,) + b_glu_rows.shape[1:], layer3),
        ],
        out_specs=pl.BlockSpec((rows_per_block, width), row),
        out_shape=jax.ShapeDtypeStruct((t, width), BF16),
        scratch_shapes=[
            pltpu.VMEM((width // LANES, rows_per_block, LANES), F32),
            pltpu.VMEM((width // LANES, rows_per_block, LANES), F32),
            pltpu.VMEM((n_tiles, SUBLANES, sw2), F32),
            pltpu.VMEM((n_tiles, steps, sw2), F32),
            pltpu.VMEM((n_tiles, SUBLANES, sw2), F32),
        ],
        compiler_params=_params("parallel", "arbitrary"),
        name="s5_scan_glu",
    )(u, wb, wc, a_re_t, a_im_t, d_rows, w_glu_all, b_glu_rows)


HALO_ROWS = 16


def _odd_inproj_body(n_plain, tiles_per_seq, x_ref, xh_ref, g_ref, w_ref, wdt_ref, cw_ref, cb_ref,
                     o_ref, dt_ref, xn_ref):
    i, j = pl.program_id(0), pl.program_id(1)

    @pl.when(j == 0)
    def _():
        g = g_ref[...]
        xn_ref[0:HALO_ROWS, :] = _rms(xh_ref[...], g).astype(BF16)
        xn_ref[HALO_ROWS:, :] = _rms(x_ref[...], g).astype(BF16)
        dt_ref[...] = jnp.dot(xn_ref[HALO_ROWS:, :], wdt_ref[...], preferred_element_type=F32)

    @pl.when(j < n_plain)
    def _():
        o_ref[...] = jnp.dot(xn_ref[HALO_ROWS:, :], w_ref[...], preferred_element_type=F32)

    @pl.when(j >= n_plain)
    def _():
        p = jnp.dot(xn_ref[...], w_ref[...], preferred_element_type=F32)
        seq_start = (i % tiles_per_seq) == 0
        halo = jnp.where(seq_start, 0.0, p[:HALO_ROWS])
        ext = jnp.concatenate([halo, p[HALO_ROWS:]], axis=0)
        acc = cb_ref[...] + cw_ref[M_CONV - 1:M_CONV, :] * p[HALO_ROWS:]
        for k in range(1, M_CONV):
            acc = acc + cw_ref[M_CONV - 1 - k:M_CONV - k, :] * ext[HALO_ROWS - k:ext.shape[0] - k]
        o_ref[...] = jax.nn.silu(acc)


def _odd_inproj(x, g, w, layer, n, wdt, conv_w, conv_b, *, seq, n_plain_cols, tm=1024, tn=1024):
    t, d = x.shape
    assert seq % tm == 0 and n_plain_cols % tn == 0 and n % tn == 0
    n_plain = n_plain_cols // tn
    halo_blocks = tm // HALO_ROWS
    conv_col = lambda i, j: (0, jnp.maximum(j - n_plain, 0))
    return pl.pallas_call(
        functools.partial(_odd_inproj_body, n_plain, seq // tm),
        grid=(t // tm, n // tn),
        in_specs=[
            pl.BlockSpec((tm, d), lambda i, j: (i, 0)),
            pl.BlockSpec((HALO_ROWS, d), lambda i, j: (jnp.maximum(i * halo_blocks - 1, 0), 0)),
            pl.BlockSpec((1, d), lambda i, j: (0, 0)),
            pl.BlockSpec((None, d, tn), lambda i, j: (layer, 0, j)),
            pl.BlockSpec((None,) + wdt.shape[1:], lambda i, j: (layer, 0, 0)),
            pl.BlockSpec((M_CONV, tn), conv_col),
            pl.BlockSpec((1, tn), conv_col),
        ],
        out_specs=[
            pl.BlockSpec((tm, tn), lambda i, j: (i, j)),
            pl.BlockSpec((tm, wdt.shape[-1]), lambda i, j: (i, 0)),
        ],
        out_shape=[jax.ShapeDtypeStruct((t, n), F32), jax.ShapeDtypeStruct((t, wdt.shape[-1]), F32)],
        scratch_shapes=[pltpu.VMEM((HALO_ROWS + tm, d), BF16)],
        compiler_params=_params("parallel", "arbitrary"),
        name="odd_inproj_conv",
    )(x, x, g, w, wdt, conv_w, conv_b)


def _ssd_body(n_groups, z_ref, xs_ref, bm_ref, cm_ref, dt_ref, dtb_ref, alog_ref, dsk_ref, ng_ref,
              o_ref, state_ref, y_ref):
    c = pl.program_id(1)
    L = M_CHUNK
    inner = xs_ref.shape[-1]
    gw = bm_ref.shape[-1] // n_groups
    hw = inner // n_groups

    @pl.when(c == 0)
    def _():
        state_ref[...] = jnp.zeros_like(state_ref)

    dt = jax.nn.softplus(dt_ref[...] + dtb_ref[...])
    a = -jnp.exp(alog_ref[...])
    acum = a * dt
    rowi = lax.broadcasted_iota(jnp.int32, acum.shape, 0)
    k = 1
    while k < L:
        acum = acum + jnp.where(rowi >= k, pltpu.roll(acum, k, axis=0), 0.0)
        k *= 2
    src = acum - jnp.log(dt)
    src_t = src.T
    a_last = acum[L - 1:L, :]

    li = lax.broadcasted_iota(jnp.int32, (L, L), 0)
    si = lax.broadcasted_iota(jnp.int32, (L, L), 1)
    causal = li >= si
    lane = lax.broadcasted_iota(jnp.int32, (L, LANES), 1)
    lower = lane < M_HEAD_DIM
    heads_per_group = hw // M_HEAD_DIM

    cbs, y_offs, bgbs = [], [], []
    for g in range(n_groups):
        cgb = cm_ref[:, g * gw:(g + 1) * gw].astype(BF16)
        bgb = bm_ref[:, g * gw:(g + 1) * gw].astype(BF16)
        cbs.append(lax.dot_general(cgb, bgb, (((1,), (1,)), ((), ())), preferred_element_type=F32))
        prev = state_ref[:, g * hw:(g + 1) * hw]
        y_offs.append(jnp.dot(cgb, prev.astype(BF16), preferred_element_type=F32))
        bgbs.append(bgb)

    for g in range(n_groups):
        cb, y_off, bgb = cbs[g], y_offs[g], bgbs[g]
        xd_parts = []
        for pr in range(heads_per_group // 2):
            h0 = g * heads_per_group + 2 * pr
            col0 = g * hw + pr * LANES
            xp = xs_ref[:, col0:col0 + LANES]
            ms = []
            for h in (h0, h0 + 1):
                seg = acum[:, h:h + 1] - src_t[h:h + 1, :]
                ms.append((cb * jnp.exp(jnp.where(causal, seg, -jnp.inf))).astype(BF16))
            lhs = jnp.concatenate(ms, axis=1)
            xpb = xp.astype(BF16)
            rhs = jnp.concatenate([jnp.where(lower, xpb, jnp.zeros_like(xpb)),
                                   jnp.where(lower, jnp.zeros_like(xpb), xpb)], axis=0)
            y_diag = jnp.dot(lhs, rhs, preferred_element_type=F32)
            al = jnp.where(lower, acum[:, h0:h0 + 1], acum[:, h0 + 1:h0 + 2])
            srcl = jnp.where(lower, src[:, h0:h0 + 1], src[:, h0 + 1:h0 + 2])
            alast = jnp.where(lower[0:1], a_last[:, h0:h0 + 1], a_last[:, h0 + 1:h0 + 2])
            y_ref[:, col0:col0 + LANES] = (y_diag + jnp.exp(al) * y_off[:, pr * LANES:(pr + 1) * LANES]
                                           + dsk_ref[:, col0:col0 + LANES] * xp)
            xd_parts.append((xp * jnp.exp(alast - srcl)).astype(BF16))
            state_ref[:, col0:col0 + LANES] = state_ref[:, col0:col0 + LANES] * jnp.exp(alast)
        xd = jnp.concatenate(xd_parts, axis=1)
        state_ref[:, g * hw:(g + 1) * hw] += lax.dot_general(
            bgb, xd, (((0,), (0,)), ((), ())), preferred_element_type=F32)

    for g in range(n_groups):
        sl = slice(g * hw, (g + 1) * hw)
        y = y_ref[:, sl] * jax.nn.silu(z_ref[:, sl])
        y = y * lax.rsqrt(jnp.mean(y * y, axis=-1, keepdims=True) + NORM_EPS)
        o_ref[:, sl] = (y * ng_ref[:, sl]).astype(BF16)


def _ssd(zxbc, dt_raw, dt_bias, a_log, d_row, norm_g, *, batch, seq, inner, n_groups):
    t = zxbc.shape[0]
    nchunks = seq // M_CHUNK
    gn = n_groups * M_STATE
    row = lambda b, c: (b * nchunks + c, 0)
    const = lambda b, c: (0, 0)
    return pl.pallas_call(
        functools.partial(_ssd_body, n_groups),
        grid=(batch, nchunks),
        in_specs=[
            pl.BlockSpec((M_CHUNK, inner), row),
            pl.BlockSpec((M_CHUNK, inner), lambda b, c: (b * nchunks + c, 1)),
            pl.BlockSpec((M_CHUNK, gn), lambda b, c: (b * nchunks + c, 2 * inner // gn)),
            pl.BlockSpec((M_CHUNK, gn), lambda b, c: (b * nchunks + c, 2 * inner // gn + 1)),
            pl.BlockSpec((M_CHUNK, LANES), row),
            pl.BlockSpec((1, LANES), const),
            pl.BlockSpec((1, LANES), const),
            pl.BlockSpec((1, inner), const),
            pl.BlockSpec((1, inner), const),
        ],
        out_specs=pl.BlockSpec((M_CHUNK, inner), row),
        out_shape=jax.ShapeDtypeStruct((t, inner), BF16),
        scratch_shapes=[pltpu.VMEM((M_STATE, inner), F32),
                        pltpu.VMEM((M_CHUNK, inner), F32)],
        compiler_params=_params("parallel", "arbitrary"),
        name="ssd_chunk",
    )(zxbc, zxbc, zxbc, zxbc, dt_raw, dt_bias, a_log, d_row, norm_g)


def _cast_body(wt_ref, o_ref):
    o_ref[...] = wt_ref[...].T.astype(BF16)


def _cast_leading_cols(w, n_cols, *, tn=1024):
    layers, rows, _ = w.shape
    assert n_cols % tn == 0
    return pl.pallas_call(
        _cast_body,
        grid=(layers, n_cols // tn),
        in_specs=[pl.BlockSpec((None, tn, rows), lambda l, j: (l, j, 0))],
        out_specs=pl.BlockSpec((None, rows, tn), lambda l, j: (l, 0, j)),
        out_shape=jax.ShapeDtypeStruct((layers, rows, n_cols), BF16),
        compiler_params=_params("parallel", "parallel"),
        name="cast_weight_columns",
    )(jnp.swapaxes(w, 1, 2))


def _block_diag_tiles(blocks, n_tiles):
    g, r, c = blocks.shape
    gpt = g // n_tiles
    eye = jnp.eye(gpt, dtype=blocks.dtype)
    b = blocks.reshape(n_tiles, gpt, r, c)
    out = jnp.einsum('tgrc,gh->tgrhc', b, eye)
    return out.reshape(n_tiles, gpt * r, gpt * c)


def _rope_inv_freq_row():
    half = ROPE_DIM // 2
    inv_freq = jnp.exp(-math.log(ROPE_THETA) * jnp.arange(half, dtype=F32) * (2.0 / ROPE_DIM))
    lane = jnp.arange(LANES) % HEAD_DIM
    return jnp.where(lane < ROPE_DIM, inv_freq[lane % half], 0.0).reshape(1, LANES).astype(F32)


def _even_prep(w_in, sinks, a_re, a_im, log_dt, b_re, b_im, c_re, c_im, d_skip, w_glu, b_glu, w_out):
    n_layers, d, _ = w_in.shape
    n_q = sinks.shape[-1]
    a_width = n_q * HEAD_DIM
    groups, state = a_re.shape[1:]
    s5_width = groups * S5_GROUP
    kv_width = (w_in.shape[-1] - a_width - s5_width) // 2
    n_kv = kv_width // HEAD_DIM
    grp = n_q // n_kv
    assert grp % 2 == 0 and a_width % LANES == 0 and a_width == s5_width

    assert kv_width % LANES == 0
    w_all = w_in.astype(BF16)

    sink_rows = jnp.repeat(sinks.astype(F32).reshape(n_layers * n_q // 2, 2), ATTN_BLOCK, axis=1)
    sink_rows = sink_rows.reshape(n_layers * n_q // 2, 1, 2 * ATTN_BLOCK)

    fold = lambda v: v.astype(F32).reshape((n_layers * groups,) + v.shape[2:])
    abr, abi, bbr, bbi = _s5_discretise(fold(a_re), fold(a_im), fold(log_dt), fold(b_re), fold(b_im))
    n_tiles = groups // S5_GROUPS_PER_TILE
    all_tiles = n_layers * n_tiles
    to_in = lambda bb: _block_diag_tiles(jnp.swapaxes(bb, 1, 2), all_tiles)
    wb = jnp.concatenate([to_in(bbr), to_in(bbi)], axis=-1).astype(BF16)
    to_out = lambda cc: _block_diag_tiles(jnp.swapaxes(fold(cc), 1, 2), all_tiles)
    wc = jnp.concatenate([to_out(c_re), -to_out(c_im)], axis=1).astype(BF16)
    return dict(
        w_all=w_all, sink_rows=sink_rows, wb=wb, wc=wc, n_tiles=n_tiles,
        a_re_t=abr.reshape(all_tiles, 1, -1), a_im_t=abi.reshape(all_tiles, 1, -1),
        d_rows=d_skip.astype(F32).reshape(n_layers, 1, s5_width), b_glu_rows=b_glu.astype(F32).reshape(n_layers, 1, s5_width),
        w_glu=w_glu.astype(BF16), w_out=w_out.astype(BF16),
        a_width=a_width, kv_width=kv_width, s5_width=s5_width, pairs_per_kv=grp // 2)


def _even_mixer(x, pos_b, norm_g, p, e, *, batch, seq):
    d = x.shape[-1]
    a_width, kv_width, s5_width = p["a_width"], p["kv_width"], p["s5_width"]
    q, kd, vd, u = _even_inproj(x, norm_g, p["w_all"], e, pos_b, _rope_inv_freq_row(),
                                qw=a_width, kw=2 * kv_width, vw=2 * kv_width, uw=s5_width)
    attn = _attention(q, kd, vd, p["sink_rows"], e, batch=batch, seq=seq, pairs_per_kv=p["pairs_per_kv"])
    ssm = _s5(u, p["wb"], p["wc"], p["a_re_t"], p["a_im_t"], p["d_rows"], p["w_glu"], p["b_glu_rows"], e,
              p["n_tiles"], batch=batch, seq=seq)
    tm = 512
    terms = [
        (attn, pl.BlockSpec((tm, a_width), lambda i: (i, 0)),
         p["w_out"], pl.BlockSpec((None, a_width, d), lambda i: (e, 0, 0))),
        (ssm, pl.BlockSpec((tm, s5_width), lambda i: (i, 0)),
         p["w_out"], pl.BlockSpec((None, s5_width, d), lambda i: (e, 1, 0))),
    ]
    return _outproj(x, terms, tm=tm)


def _odd_mixer(x, norm_g, w_main_all, w_dt_all, conv_w, conv_b, dt_bias, a_log, d_skip, norm_out, w_out_all, o, *,
               batch, seq):
    d = x.shape[-1]
    inner = norm_out.shape[0]
    heads = a_log.shape[0]
    conv_dim = conv_w.shape[-1]
    n_groups = (conv_dim - inner) // (2 * M_STATE)
    assert heads <= LANES and inner // heads == M_HEAD_DIM
    pad = LANES - heads
    zxbc, dt_raw = _odd_inproj(x, norm_g, w_main_all, o, inner + conv_dim, w_dt_all, conv_w.astype(F32),
                               conv_b.astype(F32).reshape(1, conv_dim), seq=seq, n_plain_cols=inner)
    row = lambda v: jnp.pad(v.astype(F32), (0, pad)).reshape(1, LANES)
    y = _ssd(zxbc, dt_raw, row(dt_bias), row(a_log), jnp.repeat(d_skip.astype(F32), M_HEAD_DIM).reshape(1, inner),
             norm_out.astype(F32).reshape(1, inner), batch=batch, seq=seq, inner=inner, n_groups=n_groups)
    tm = 512
    return _outproj(x, [(y, pl.BlockSpec((tm, inner), lambda i: (i, 0)),
                         w_out_all, pl.BlockSpec((None, inner, d), lambda i: (o, 0, 0)))], tm=tm)


def kernel(x, positions, norm_ffn1, ffn1_gate, ffn1_up, ffn1_down, norm_mix, norm_ffn2, ffn2_gate, ffn2_up,
           ffn2_down, ev_w_in, ev_sinks, s5_a_re, s5_a_im, s5_log_dt, s5_b_re, s5_b_im, s5_c_re, s5_c_im, s5_d,
           s5_w_glu, s5_b_glu, ev_w_out, m_w_in, m_conv_w, m_conv_b, m_dt_bias, m_a_log, m_d, m_norm, m_w_out,
           final_norm):
    batch, seq, d = x.shape
    depth = norm_ffn1.shape[0]
    t = batch * seq
    h = x.reshape(t, d).astype(F32)
    pos_b = jnp.broadcast_to(positions.reshape(t, 1).astype(F32), (t, LANES))
    g1, g2, gm = (v.astype(F32).reshape(depth, 1, d) for v in (norm_ffn1, norm_ffn2, norm_mix))
    even = _even_prep(ev_w_in, ev_sinks, s5_a_re, s5_a_im, s5_log_dt, s5_b_re, s5_b_im, s5_c_re, s5_c_im, s5_d,
                      s5_w_glu, s5_b_glu, ev_w_out)
    m_w_out_all = m_w_out.astype(BF16)
    m_main = m_norm.shape[-1] + m_conv_w.shape[-1]
    m_w_main_all = _cast_leading_cols(m_w_in, m_main)
    m_w_dt_all = jnp.pad(m_w_in[:, :, m_main:], ((0, 0), (0, 0), (0, LANES - m_a_log.shape[-1]))).astype(BF16)
    ffn_f32 = [(ffn1_gate, ffn1_up, ffn1_down), (ffn2_gate, ffn2_up, ffn2_down)]
    w_cur = tuple(w[0].astype(BF16) for w in ffn_f32[0])
    for layer in range(depth):
        h, w_cur = _ffn(h, g1, *w_cur, layer, next_f32=ffn_f32[1] + (layer,))
        if layer % 2 == 0:
            e = layer // 2
            h = _even_mixer(h, pos_b, gm[layer], even, e, batch=batch, seq=seq)
        else:
            o = layer // 2
            h = _odd_mixer(h, gm[layer], m_w_main_all, m_w_dt_all, m_conv_w[o], m_conv_b[o], m_dt_bias[o], m_a_log[o],
                           m_d[o], m_norm[o], m_w_out_all, o, batch=batch, seq=seq)
        if layer == depth - 1:
            h, _ = _ffn(h, g2, *w_cur, layer, final_g=final_norm.astype(F32).reshape(1, d))
        else:
            h, w_cur = _ffn(h, g2, *w_cur, layer, next_f32=ffn_f32[0] + (layer + 1,))
    return h.reshape(batch, seq, d).astype(x.dtype)
```
